```python
import math
import jax
import jax.numpy as jnp
from jax import lax
import numpy as np

D_MODEL = 4096
BATCH = 8
SEQ = 2048
DEPTH = 2

GRID_W = 64
CTX_LEN = 256
F32 = jnp.float32
NORM_EPS = 1e-6

DA_HEADS = 16
DA_HEAD_DIM = 64
DA_V_DIM = 2 * DA_HEAD_DIM
DA_WIDTH = DA_HEADS * DA_V_DIM
DA_SCALE = DA_HEAD_DIM ** -0.5
Q_BLOCK = 128
ROPE_BASE = 10000.0

RW_HEADS = 16
RW_HEAD_DIM = 64
RW_WIDTH = RW_HEADS * RW_HEAD_DIM
RW_DECAY_RANK = 64
RW_A_RANK = 64
RW_GATE_RANK = 128
RW_GN_EPS = 64e-5
RW_STATE_SIZES = (RW_WIDTH, RW_WIDTH, RW_DECAY_RANK, RW_DECAY_RANK, RW_A_RANK, RW_A_RANK)
RW_STATE_COLS = 2 * RW_WIDTH + 2 * RW_DECAY_RANK + 2 * RW_A_RANK
RW_OUT_COLS = RW_WIDTH + RW_GATE_RANK

S5_GROUP = 16
S5_GROUPS = 64
S5_WIDTH = S5_GROUPS * S5_GROUP
S5_STATE = 64
S5_DT_MIN = 1e-3
S5_DT_MAX = 1e-1

N_BRANCH = 3
COL_SIZES = (DA_WIDTH, DA_WIDTH, RW_STATE_COLS, S5_WIDTH, DA_WIDTH, RW_OUT_COLS, N_BRANCH * D_MODEL)
STATE_COLS = 2 * DA_WIDTH + RW_STATE_COLS + S5_WIDTH
N_IN_COLS = STATE_COLS + DA_WIDTH + RW_OUT_COLS + N_BRANCH * D_MODEL

N_EXPERTS = 64
TOP_K = 8
N_GROUPS = 8
TOPK_GROUPS = 4
EXPERT_FF = 256
SHARED_FF = 1024
ROUTED_SCALE = 2.5
EXPERT_CHUNK = 8

kernel_name = 'hybrid_diffattn_rwkv7_s5_moe_dit'


def _split(t, sizes):
    out, off = [], 0
    for s in sizes:
        out.append(t[..., off:off + s])
        off += s
    return out


def _rms(t, eps=NORM_EPS):
    tf = t.astype(F32)
    return (tf * lax.rsqrt(jnp.mean(tf * tf, axis=-1, keepdims=True) + eps)).astype(t.dtype)


def _modulate(t, shift, scale):
    return _rms(t) * (1.0 + scale) + shift


def _axial_rope_tables(rows):
    n_tok = rows * GRID_W
    row = jnp.repeat(jnp.arange(rows, dtype=F32), GRID_W)
    col = jnp.tile(jnp.arange(GRID_W, dtype=F32), rows)
    half = DA_HEAD_DIM // 2
    inv_freq = 1.0 / (ROPE_BASE ** (jnp.arange(0, half, 2, dtype=F32) / half))
    shp = (n_tok, 1, 1, half // 2)
    ang_r = (row[:, None] * inv_freq).reshape(shp)
    ang_c = (col[:, None] * inv_freq).reshape(shp)
    return (jnp.cos(ang_r), jnp.sin(ang_r), jnp.cos(ang_c), jnp.sin(ang_c))


def _rope_rotate(t, cos, sin):
    n = t.shape[-1] // 2
    t1, t2 = t[..., :n], t[..., n:]
    return jnp.concatenate([t1 * cos - t2 * sin, t2 * cos + t1 * sin], axis=-1)


def _axial_rope(t, tabs):
    cr, sr, cc, sc = tabs
    half = DA_HEAD_DIM // 2
    out = jnp.concatenate([_rope_rotate(t[..., :half], cr, sr), _rope_rotate(t[..., half:], cc, sc)], axis=-1)
    return out.astype(t.dtype)


def _da_qk(t, gain, tabs):
    b, n, _ = t.shape
    t = _rms(t.reshape(b, n, DA_HEADS, 2, DA_HEAD_DIM)) * gain
    if tabs is not None:
        t = _axial_rope(t, tabs)
    return jnp.transpose(t, (0, 2, 3, 1, 4))


def _da_v(t):
    b, n, _ = t.shape
    return jnp.transpose(t.reshape(b, n, DA_HEADS, DA_V_DIM), (0, 2, 1, 3))


def _diff_softmax_mix(scores, v, lam):
    pr = jax.nn.softmax(scores.astype(F32) * DA_SCALE, axis=-1)
    a = pr[:, :, 0] - lam * pr[:, :, 1]
    return jnp.einsum('bhqk,bhkv->bhqv', a.astype(v.dtype), v)


def _da_out(o, lp, lam_init):
    b, _, n, _ = o.shape
    o = _rms(o) * lp['da_subln'] * (1.0 - lam_init)
    return jnp.transpose(o, (0, 2, 1, 3)).reshape(b, n, DA_WIDTH)


def _diff_attention(q, k, v, qc, kc, vc, lp, lam_init, tabs):
    b, n, _ = q.shape
    lq = lp['da_lambda_q'].astype(F32)
    lk = lp['da_lambda_k'].astype(F32)
    lam = jnp.exp(jnp.sum(lq[0] * lk[0])) - jnp.exp(jnp.sum(lq[1] * lk[1])) + lam_init
    q_l = _da_qk(q, lp['da_q_norm'], tabs)
    k_l = _da_qk(k, lp['da_k_norm'], tabs)
    k_c = _da_qk(kc, lp['da_k_norm'], None)
    v_l = _da_v(v)
    v_c = _da_v(vc)
    k_all = jnp.concatenate([k_l, k_c], axis=3)
    v_all = jnp.concatenate([v_l, v_c], axis=2)
    nb = n // Q_BLOCK
    q_blocks = jnp.moveaxis(q_l.reshape(b, DA_HEADS, 2, nb, Q_BLOCK, DA_HEAD_DIM), 3, 0)

    def block(qb):
        return _diff_softmax_mix(jnp.einsum('bhmqd,bhmkd->bhmqk', qb, k_all), v_all, lam)

    o = lax.map(block, q_blocks)
    o = jnp.moveaxis(o, 0, 2).reshape(b, DA_HEADS, n, DA_V_DIM)
    y = _da_out(o, lp, lam_init)
    y_c = None
    if qc is not None:
        q_c = _da_qk(qc, lp['da_q_norm'], None)
        o_c = _diff_softmax_mix(jnp.einsum('bhmqd,bhmkd->bhmqk', q_c, k_c), v_c, lam)
        y_c = _da_out(o_c, lp, lam_init)
    return y, y_c


def _shift_conv(t, w):
    n = t.shape[1]
    tp = jnp.pad(t, ((0, 0), (1, 1), (0, 0)))
    return tp[:, :n] * w[0] + tp[:, 1:n + 1] * w[1] + tp[:, 2:] * w[2]


def _heads(t):
    return t.reshape(t.shape[0], t.shape[1], RW_HEADS, RW_HEAD_DIM)


def _rwkv_state_terms(st, lp):
    st = st.astype(F32)
    k, v, wdf, wdb, adf, adb = _split(st, RW_STATE_SIZES)
    kk = _heads(k * lp['rw_k_k'].astype(F32))
    kk = kk * lax.rsqrt(jnp.sum(kk * kk, axis=-1, keepdims=True) + 1e-12)
    dirs = []
    for d, (wd, ad) in enumerate(((wdf, adf), (wdb, adb))):
        w0 = lp['rw_w0'][d].astype(F32)
        w2 = lp['rw_w2'][d].astype(F32)
        a0 = lp['rw_a0'][d].astype(F32)
        a2 = lp['rw_a2'][d].astype(F32)
        w_log = -jax.nn.softplus(-(w0 + jnp.tanh(wd) @ w2)) - 0.5
        decay = jnp.exp(-jnp.exp(w_log))
        a = jax.nn.sigmoid(a0 + ad @ a2)
        k_d = k * (1.0 + (a - 1.0) * lp['rw_k_a'].astype(F32))
        dirs.append((_heads(decay), _heads(a), _heads(k_d)))
    return kk, _heads(v), dirs


def _rwkv_scan(s0, decay, kk, a, k, v, r, reverse):
    seqs = [decay, kk, kk * a, k, v] + ([] if r is None else [r])
    seqs = tuple(jnp.moveaxis(t, 1, 0) for t in seqs)

    def step(s, inp):
        w_t, kk_t, kka_t, k_t, v_t = inp[:5]
        sk = jnp.einsum('bhvk,bhk->bhv', s, kk_t)
        s = s * w_t[:, :, None, :] - sk[..., None] * kka_t[:, :, None, :] + v_t[..., None] * k_t[:, :, None, :]
        out = None if r is None else jnp.einsum('bhvk,bhk->bhv', s, inp[5])
        return s, out

    s_fin, ys = lax.scan(step, s0, seqs, reverse=reverse)
    return s_fin, (None if r is None else jnp.moveaxis(ys, 0, 1))


def _rwkv_readout(o, r, dirs, v, gd, lp):
    mu = jnp.mean(o, axis=-1, keepdims=True)
    var = jnp.mean(jnp.square(o - mu), axis=-1, keepdims=True)
    on = (o - mu) * lax.rsqrt(var + RW_GN_EPS)
    on = on * lp['rw_gn_w'].astype(F32).reshape(RW_HEADS, RW_HEAD_DIM) + lp['rw_gn_b'].astype(F32).reshape(RW_HEADS, RW_HEAD_DIM)
    r_k = lp['rw_r_k'].astype(F32)
    bonus = sum(jnp.sum(r * k_d * r_k, axis=-1, keepdims=True) for _, _, k_d in dirs)
    y = (on + bonus * v).reshape(o.shape[0], o.shape[1], RW_WIDTH)
    g = jax.nn.sigmoid(gd.astype(F32)) @ lp['rw_g2'].astype(F32)
    return y * g


def _rwkv_branch(st, ou, st_c, ou_c, lp):
    conv = lp['rw_conv']
    w_st, w_ou = conv[:, :RW_STATE_COLS], conv[:, RW_STATE_COLS:]
    kk, v, dirs = _rwkv_state_terms(_shift_conv(st, w_st), lp)
    kk_c, v_c, dirs_c = _rwkv_state_terms(_shift_conv(st_c, w_st), lp)
    r, gd = _split(_shift_conv(ou, w_ou), (RW_WIDTH, RW_GATE_RANK))
    r = _heads(r.astype(F32))
    need_ctx = ou_c is not None
    r_c = None
    if need_ctx:
        r_c, gd_c = _split(_shift_conv(ou_c, w_ou), (RW_WIDTH, RW_GATE_RANK))
        r_c = _heads(r_c.astype(F32))
    b = st.shape[0]
    o_lat, o_ctx = 0.0, 0.0
    for d, reverse in enumerate((False, True)):
        s0 = jnp.zeros((b, RW_HEADS, RW_HEAD_DIM, RW_HEAD_DIM), F32)
        dec_c, a_c, k_c = dirs_c[d]
        s_ctx, oc = _rwkv_scan(s0, dec_c, kk_c, a_c, k_c, v_c, r_c, reverse)
        dec, a, k_d = dirs[d]
        _, ol = _rwkv_scan(s_ctx, dec, kk, a, k_d, v, r, reverse)
        o_lat = o_lat + ol
        if need_ctx:
            o_ctx = o_ctx + oc
    y = _rwkv_readout(o_lat, r, dirs, v, gd, lp).astype(st.dtype)
    y_c = _rwkv_readout(o_ctx, r_c, dirs_c, v_c, gd_c, lp).astype(st.dtype) if need_ctx else None
    return y, y_c


def _cmul(ar, ai, br, bi):
    return ar * br - ai * bi, ar * bi + ai * br


def _ssm_combine(e1, e2):
    a1r, a1i, b1r, b1i = e1
    a2r, a2i, b2r, b2i = e2
    ar, ai = _cmul(a2r, a2i, a1r, a1i)
    br, bi = _cmul(a2r, a2i, b1r, b1i)
    return ar, ai, br + b2r, bi + b2i


def _s5_discretize(lam_re, lam_im, log_dt):
    lam_re = lam_re.astype(F32)
    lam_im = lam_im.astype(F32)
    dt = jnp.exp(log_dt.astype(F32))[:, None]
    mag = jnp.exp(lam_re * dt)
    abar_re = mag * jnp.cos(lam_im * dt)
    abar_im = mag * jnp.sin(lam_im * dt)
    den = lam_re * lam_re + lam_im * lam_im
    nr = abar_re - 1.0
    g_re = (nr * lam_re + abar_im * lam_im) / den
    g_im = (abar_im * lam_re - nr * lam_im) / den
    return abar_re, abar_im, g_re, g_im


def _s5_drive(u, lp):
    b, n, _ = u.shape
    ug = u.astype(F32).reshape(b, n, S5_GROUPS, S5_GROUP)
    return (jnp.einsum('btgh,gph->btgp', ug, lp['s5_b_re'].astype(F32)),
            jnp.einsum('btgh,gph->btgp', ug, lp['s5_b_im'].astype(F32)))


def _s5_scan(drive, disc, h0, reverse):
    ar, ai, gr, gi = disc
    ur, ui = _cmul(gr, gi, drive[0], drive[1])
    if h0 is not None:
        hr, hi = _cmul(ar, ai, h0[0], h0[1])
        t0 = ur.shape[1] - 1 if reverse else 0
        ur = ur.at[:, t0].add(hr)
        ui = ui.at[:, t0].add(hi)
    elems = (jnp.broadcast_to(ar, ur.shape), jnp.broadcast_to(ai, ur.shape), ur, ui)
    _, _, xr, xi = lax.associative_scan(_ssm_combine, elems, reverse=reverse, axis=1)
    return xr, xi


def _s5_readout(xr, xi, c_re, c_im):
    b, n = xr.shape[:2]
    y = jnp.einsum('btgp,ghp->btgh', xr, c_re) - jnp.einsum('btgp,ghp->btgh', xi, c_im)
    return y.reshape(b, n, S5_WIDTH)


def _s5_glu(y, lp):
    y = jax.nn.gelu(y)
    return y * jax.nn.sigmoid(y @ lp['s5_glu_w'].astype(F32) + lp['s5_glu_b'].astype(F32))


def _s5_branch(u, u_c, lp, need_ctx):
    drive = _s5_drive(u, lp)
    drive_c = _s5_drive(u_c, lp)
    d_skip = lp['s5_d'].astype(F32)
    y = u.astype(F32) * d_skip
    y_c = u_c.astype(F32) * d_skip if need_ctx else None
    for d, reverse in enumerate((False, True)):
        disc = _s5_discretize(lp['s5_lambda_re'][d], lp['s5_lambda_im'][d], lp['s5_log_dt'][d])
        c_re = lp['s5_c_re'][d].astype(F32)
        c_im = lp['s5_c_im'][d].astype(F32)
        xcr, xci = _s5_scan(drive_c, disc, None, reverse)
        t_end = 0 if reverse else -1
        xr, xi = _s5_scan(drive, disc, (xcr[:, t_end], xci[:, t_end]), reverse)
        y = y + _s5_readout(xr, xi, c_re, c_im)
        if need_ctx:
            y_c = y_c + _s5_readout(xcr, xci, c_re, c_im)
    out = _s5_glu(y, lp).astype(u.dtype)
    out_c = _s5_glu(y_c, lp).astype(u.dtype) if need_ctx else None
    return out, out_c


def _merge(y_a, y_r, y_s, gates, lp):
    g = jax.nn.sigmoid(gates.astype(F32)).astype(gates.dtype)
    g_a, g_r, g_s = _split(g, (D_MODEL, D_MODEL, D_MODEL))
    m = g_a * (y_a @ lp['w_branch_a']) + g_r * (y_r @ lp['w_branch_r']) + g_s * (y_s @ lp['w_branch_s'])
    return m @ lp['w_out']


def _token_mixer(h, hc, lp, lam_init, tabs, need_ctx):
    p = h @ lp['w_in']
    a_k, a_v, rw_st, s_u, a_q, rw_ou, gates = _split(p, COL_SIZES)
    if need_ctx:
        pc = hc @ lp['w_in']
        a_kc, a_vc, rw_stc, s_uc, a_qc, rw_ouc, gates_c = _split(pc, COL_SIZES)
    else:
        pc = hc @ lp['w_in'][:, :STATE_COLS]
        a_kc, a_vc, rw_stc, s_uc = _split(pc, COL_SIZES[:4])
        a_qc, rw_ouc, gates_c = None, None, None
    y_a, y_ac = _diff_attention(a_q, a_k, a_v, a_qc, a_kc, a_vc, lp, lam_init, tabs)
    y_r, y_rc = _rwkv_branch(rw_st, rw_ou, rw_stc, rw_ouc, lp)
    y_s, y_sc = _s5_branch(s_u, s_uc, lp, need_ctx)
    y = _merge(y_a, y_r, y_s, gates, lp)
    y_c = _merge(y_ac, y_rc, y_sc, gates_c, lp) if need_ctx else None
    return y, y_c


def _swiglu(h, wg, wu, wd):
    return (jax.nn.silu(h @ wg) * (h @ wu)) @ wd


def _moe(h, lp):
    n = h.shape[0]
    scores = jax.nn.sigmoid((h @ lp['router_w']).astype(F32))
    sel = scores + lp['router_bias'].astype(F32)
    per_group = N_EXPERTS // N_GROUPS
    group_score = jnp.sum(lax.top_k(sel.reshape(n, N_GROUPS, per_group), 2)[0], axis=-1)
    _, gidx = lax.top_k(group_score, TOPK_GROUPS)
    gmask = jnp.sum(jax.nn.one_hot(gidx, N_GROUPS, dtype=F32), axis=1)
    emask = jnp.repeat(gmask, per_group, axis=1) > 0
    _, eidx = lax.top_k(jnp.where(emask, sel, -jnp.inf), TOP_K)
    w = jnp.take_along_axis(scores, eidx, axis=1)
    w = w / jnp.sum(w, axis=-1, keepdims=True) * ROUTED_SCALE
    combine = jnp.sum(jax.nn.one_hot(eidx, N_EXPERTS, dtype=F32) * w[..., None], axis=1).astype(h.dtype)
    out = _swiglu(h, lp['sh_w_gate'], lp['sh_w_up'], lp['sh_w_down'])
    for e0 in range(0, N_EXPERTS, EXPERT_CHUNK):
        sl = slice(e0, e0 + EXPERT_CHUNK)
        hg = jnp.einsum('nd,edf->nef', h, lp['exp_w_gate'][sl])
        hu = jnp.einsum('nd,edf->nef', h, lp['exp_w_up'][sl])
        act = jax.nn.silu(hg) * hu * combine[:, sl, None]
        out = out + jnp.einsum('nef,efd->nd', act, lp['exp_w_down'][sl])
    return out


def _nrm(k, shape, scale):
    return jax.random.normal(k, shape, F32) * scale


def setup_inputs(seed: int = 0) -> dict:
    key = jax.random.key(seed)
    k = jax.random.split(key, 45)
    L, D = DEPTH, D_MODEL
    G, P = S5_GROUPS, S5_STATE
    centre_tap = jnp.array([0.0, 1.0, 0.0], F32)[None, :, None]
    return {
        'x': _nrm(k[0], (BATCH, SEQ, D), 1.0),
        'c': _nrm(k[1], (BATCH, D), 1.0),
        'ctx': _nrm(k[2], (BATCH, CTX_LEN, D), 1.0),
        'c_ctx': _nrm(k[3], (D,), 1.0),
        'ada_w': _nrm(k[4], (L, D, 6 * D), 0.5 * D ** -0.5),
        'ada_b': _nrm(k[5], (L, 6 * D), 0.01),
        'w_in': _nrm(k[6], (L, D, N_IN_COLS), D ** -0.5),
        'da_q_norm': 1.0 + _nrm(k[7], (L, DA_HEAD_DIM), 0.02),
        'da_k_norm': 1.0 + _nrm(k[8], (L, DA_HEAD_DIM), 0.02),
        'da_lambda_q': _nrm(k[9], (L, 2, DA_HEAD_DIM), 0.1),
        'da_lambda_k': _nrm(k[10], (L, 2, DA_HEAD_DIM), 0.1),
        'da_subln': 1.0 + _nrm(k[11], (L, DA_V_DIM), 0.02),
        'rw_conv': _nrm(k[12], (L, 3, RW_STATE_COLS + RW_OUT_COLS), 0.1) + centre_tap,
        'rw_w0': jax.random.uniform(k[13], (L, 2, RW_WIDTH), F32, -6.5, -1.5),
        'rw_w2': _nrm(k[14], (L, 2, RW_DECAY_RANK, RW_WIDTH), 0.5 * RW_DECAY_RANK ** -0.5),
        'rw_a0': _nrm(k[15], (L, 2, RW_WIDTH), 0.1),
        'rw_a2': _nrm(k[16], (L, 2, RW_A_RANK, RW_WIDTH), 0.5 * RW_A_RANK ** -0.5),
        'rw_g2': _nrm(k[17], (L, RW_GATE_RANK, RW_WIDTH), RW_GATE_RANK ** -0.5),
        'rw_k_k': 0.85 + _nrm(k[18], (L, RW_WIDTH), 0.02),
        'rw_k_a': 1.0 + _nrm(k[19], (L, RW_WIDTH), 0.02),
        'rw_r_k': _nrm(k[20], (L, RW_HEADS, RW_HEAD_DIM), 0.1),
        'rw_gn_w': 1.0 + _nrm(k[21], (L, RW_WIDTH), 0.02),
        'rw_gn_b': _nrm(k[22], (L, RW_WIDTH), 0.01),
        's5_lambda_re': -0.5 + _nrm(k[23], (L, 2, G, P), 0.01),
        's5_lambda_im': math.pi * jnp.arange(P, dtype=F32) + _nrm(k[24], (L, 2, G, P), 0.01),
        's5_log_dt': jax.random.uniform(k[25], (L, 2, G), F32, math.log(S5_DT_MIN), math.log(S5_DT_MAX)),
        's5_b_re': _nrm(k[26], (L, G, P, S5_GROUP), (2 * S5_GROUP) ** -0.5),
        's5_b_im': _nrm(k[27], (L, G, P, S5_GROUP), (2 * S5_GROUP) ** -0.5),
        's5_c_re': _nrm(k[28], (L, 2, G, S5_GROUP, P), P ** -0.5),
        's5_c_im': _nrm(k[29], (L, 2, G, S5_GROUP, P), P ** -0.5),
        's5_d': _nrm(k[30], (L, S5_WIDTH), 1.0),
        's5_glu_w': _nrm(k[31], (L, S5_WIDTH, S5_WIDTH), S5_WIDTH ** -0.5),
        's5_glu_b': _nrm(k[32], (L, S5_WIDTH), 0.01),
        'w_branch_a': _nrm(k[33], (L, DA_WIDTH, D), DA_WIDTH ** -0.5),
        'w_branch_r': _nrm(k[34], (L, RW_WIDTH, D), RW_WIDTH ** -0.5),
        'w_branch_s': _nrm(k[35], (L, S5_WIDTH, D), S5_WIDTH ** -0.5),
        'w_out': _nrm(k[36], (L, D, D), D ** -0.5),
        'router_w': _nrm(k[37], (L, D, N_EXPERTS), D ** -0.5),
        'router_bias': _nrm(k[38], (L, N_EXPERTS), 0.01),
        'exp_w_gate': _nrm(k[39], (L, N_EXPERTS, D, EXPERT_FF), D ** -0.5),
        'exp_w_up': _nrm(k[40], (L, N_EXPERTS, D, EXPERT_FF), D ** -0.5),
        'exp_w_down': _nrm(k[41], (L, N_EXPERTS, EXPERT_FF, D), EXPERT_FF ** -0.5),
        'sh_w_gate': _nrm(k[42], (L, D, SHARED_FF), D ** -0.5),
        'sh_w_up': _nrm(k[43], (L, D, SHARED_FF), D ** -0.5),
        'sh_w_down': _nrm(k[44], (L, SHARED_FF, D), SHARED_FF ** -0.5),
    }


def reference(x, c, ctx, c_ctx, ada_w, ada_b, w_in, da_q_norm, da_k_norm, da_lambda_q, da_lambda_k,
              da_subln, rw_conv, rw_w0, rw_w2, rw_a0, rw_a2, rw_g2, rw_k_k, rw_k_a, rw_r_k, rw_gn_w,
              rw_gn_b, s5_lambda_re, s5_lambda_im, s5_log_dt, s5_b_re, s5_b_im, s5_c_re, s5_c_im, s5_d,
              s5_glu_w, s5_glu_b, w_branch_a, w_branch_r, w_branch_s, w_out, router_w, router_bias,
              exp_w_gate, exp_w_up, exp_w_down, sh_w_gate, sh_w_up, sh_w_down):
    b, t, _ = x.shape
    n_ctx = ctx.shape[1]
    rows = t // GRID_W
    tabs = _axial_rope_tables(rows)
    layer_params = (
        ('w_in', w_in), ('da_q_norm', da_q_norm), ('da_k_norm', da_k_norm),
        ('da_lambda_q', da_lambda_q), ('da_lambda_k', da_lambda_k), ('da_subln', da_subln),
        ('rw_conv', rw_conv), ('rw_w0', rw_w0), ('rw_w2', rw_w2), ('rw_a0', rw_a0), ('rw_a2', rw_a2),
        ('rw_g2', rw_g2), ('rw_k_k', rw_k_k), ('rw_k_a', rw_k_a), ('rw_r_k', rw_r_k),
        ('rw_gn_w', rw_gn_w), ('rw_gn_b', rw_gn_b),
        ('s5_lambda_re', s5_lambda_re), ('s5_lambda_im', s5_lambda_im), ('s5_log_dt', s5_log_dt),
        ('s5_b_re', s5_b_re), ('s5_b_im', s5_b_im), ('s5_c_re', s5_c_re), ('s5_c_im', s5_c_im),
        ('s5_d', s5_d), ('s5_glu_w', s5_glu_w), ('s5_glu_b', s5_glu_b),
        ('w_branch_a', w_branch_a), ('w_branch_r', w_branch_r), ('w_branch_s', w_branch_s),
        ('w_out', w_out), ('router_w', router_w), ('router_bias', router_bias),
        ('exp_w_gate', exp_w_gate), ('exp_w_up', exp_w_up), ('exp_w_down', exp_w_down),
        ('sh_w_gate', sh_w_gate), ('sh_w_up', sh_w_up), ('sh_w_down', sh_w_down),
    )
    for i in range(DEPTH):
        lp = {name: arr[i] for name, arr in layer_params}
        need_ctx = i < DEPTH - 1
        lam_init = 0.8 - 0.6 * math.exp(-0.3 * i)
        mod = jax.nn.silu(c) @ ada_w[i] + ada_b[i]
        mod_c = jax.nn.silu(c_ctx) @ ada_w[i] + ada_b[i]
        sh1, sc1, g1, sh2, sc2, g2 = jnp.split(mod[:, None, :], 6, axis=-1)
        csh1, csc1, cg1, csh2, csc2, cg2 = jnp.split(mod_c, 6)
        h = _modulate(x, sh1, sc1)
        hc = _modulate(ctx, csh1, csc1)
        y, y_c = _token_mixer(h, hc, lp, lam_init, tabs, need_ctx)
        x = x + g1 * y
        h2 = _modulate(x, sh2, sc2).reshape(b * t, D_MODEL)
        if need_ctx:
            ctx = ctx + cg1 * y_c
            hc2 = _modulate(ctx, csh2, csc2).reshape(b * n_ctx, D_MODEL)
            f = _moe(jnp.concatenate([h2, hc2], axis=0), lp)
            ctx = ctx + cg2 * f[b * t:].reshape(b, n_ctx, D_MODEL)
            f = f[:b * t]
        else:
            f = _moe(h2, lp)
        x = x + g2 * f.reshape(b, t, D_MODEL)
    return x
```

```python
import functools
import math

import jax
import jax.numpy as jnp
from jax import lax
from jax.experimental import pallas as pl
from jax.experimental.pallas import tpu as pltpu

F32 = jnp.float32
BF16 = jnp.bfloat16

GRID_W = 64
NORM_EPS = 1e-6
DA_HEADS = 16
DA_HEAD_DIM = 64
DA_V_DIM = 2 * DA_HEAD_DIM
DA_WIDTH = DA_HEADS * DA_V_DIM
DA_SCALE = DA_HEAD_DIM ** -0.5
ROPE_BASE = 10000.0
RW_HEADS = 16
RW_HEAD_DIM = 64
RW_WIDTH = RW_HEADS * RW_HEAD_DIM
RW_DECAY_RANK = 64
RW_A_RANK = 64
RW_GATE_RANK = 128
RW_GN_EPS = 64e-5
RW_STATE_COLS = 2 * RW_WIDTH + 2 * RW_DECAY_RANK + 2 * RW_A_RANK
RW_OUT_COLS = RW_WIDTH + RW_GATE_RANK
S5_GROUP = 16
S5_GROUPS = 64
S5_WIDTH = S5_GROUPS * S5_GROUP
S5_STATE = 64
S5_LANES = S5_GROUPS * S5_STATE
N_EXPERTS = 64
TOP_K = 8
N_GROUPS = 8
TOPK_GROUPS = 4
EXPERT_FF = 256
ROUTED_SCALE = 2.5

LANES = 128
SUBLANES = 8
VMEM_LIMIT_BYTES = 56 * 1024 * 1024

RW_CHUNK = 64
RW_HEADS_PER_STEP = 8
S5_CHUNK = 64
MOD_CTX_ROW = 8


def _cparams(sem):
    return pltpu.CompilerParams(dimension_semantics=sem, vmem_limit_bytes=VMEM_LIMIT_BYTES)


def _tile(n, pref, mult=SUBLANES):
    if n <= pref:
        return n
    t = (pref // mult) * mult
    while t > mult and n % t:
        t -= mult
    assert n % t == 0, (n, pref)
    return t


def _dot(a, b):
    return jnp.dot(a.astype(BF16), b.astype(BF16), preferred_element_type=F32)


def _dot_nt(a, b):
    return lax.dot_general(a.astype(BF16), b.astype(BF16), (((1,), (1,)), ((), ())),
                           preferred_element_type=F32)


def _dot_tn(a, b):
    return lax.dot_general(a.astype(BF16), b.astype(BF16), (((0,), (0,)), ((), ())),
                           preferred_element_type=F32)


def _hilo(x):
    hi = x.astype(BF16)
    lo = (x - hi.astype(F32)).astype(BF16)
    return hi, lo


def _dot_hp_lhs(a, b_exact):
    hi, lo = _hilo(a)
    return (jnp.dot(hi, b_exact, preferred_element_type=F32)
            + jnp.dot(lo, b_exact, preferred_element_type=F32))


def _dot_hp(a, b):
    ah, al = _hilo(a)
    bh, bl = _hilo(b)
    return (jnp.dot(ah, bh, preferred_element_type=F32) + jnp.dot(al, bh, preferred_element_type=F32)
            + jnp.dot(ah, bl, preferred_element_type=F32))


def _sigmoid(x):
    return 1.0 / (1.0 + jnp.exp(-x))


def _softplus(x):
    return jnp.maximum(x, 0.0) + jnp.log(1.0 + jnp.exp(-jnp.abs(x)))


def _gelu_tanh(x):
    c = math.sqrt(2.0 / math.pi)
    return 0.5 * x * (1.0 + jnp.tanh(c * (x + 0.044715 * (x * x * x))))


def _mm_body(a_ref, b_ref, *rest, n_extra, prologue, epilogue):
    extras = rest[:n_extra]
    o_ref = rest[n_extra]
    a = a_ref[...]
    if prologue is not None:
        a = prologue(a)
    acc = jnp.dot(a.astype(BF16), b_ref[...].astype(BF16), preferred_element_type=F32)
    if epilogue is not None:
        acc = epilogue(acc, *[e[...] for e in extras])
    o_ref[...] = acc.astype(o_ref.dtype)


def _matmul(a, b, out_dtype, *, tm=512, tn=1024, prologue=None, epilogue=None, extras=(),
            a_spec=None, out_spec=None, out_shape=None, grid_m=None, name="matmul"):
    k, n = b.shape
    tn = _tile(n, tn, LANES)
    if a_spec is None:
        m = a.shape[0]
        tm = _tile(m, tm)
        grid_m = m // tm
        a_spec = pl.BlockSpec((tm, k), lambda i, j: (i, 0))
    if out_spec is None:
        out_spec = pl.BlockSpec((tm, tn), lambda i, j: (i, j))
        out_shape = (a.shape[0], n)
    body = functools.partial(_mm_body, n_extra=len(extras), prologue=prologue, epilogue=epilogue)
    return pl.pallas_call(
        body,
        grid=(grid_m, n // tn),
        in_specs=[a_spec, pl.BlockSpec((k, tn), lambda i, j: (0, j))] + [s for _, s in extras],
        out_specs=out_spec,
        out_shape=jax.ShapeDtypeStruct(out_shape, out_dtype),
        compiler_params=_cparams(("parallel", "arbitrary")),
        name=name,
    )(a, b, *[x for x, _ in extras])


def _ada_table(cvec, ada_w_l, ada_b_l):
    d6 = ada_w_l.shape[1]

    def prologue(a):
        return a * _sigmoid(a)

    def epilogue(acc, bias):
        return acc + bias

    out = _matmul(cvec, ada_w_l, F32, tm=16, tn=512, prologue=prologue, epilogue=epilogue,
                  extras=[(ada_b_l.reshape(1, d6), pl.BlockSpec((1, 512), lambda i, j: (0, j)))],
                  name="ada_table")
    return out.reshape(cvec.shape[0], 1, d6)


def _mod_row_fn(rows_per_batch, tm):
    if rows_per_batch is None:
        return lambda i: MOD_CTX_ROW
    nb = rows_per_batch // tm
    return lambda i: i // nb


def _modulate_body(x_ref, sh_ref, sc_ref, o_ref):
    x = x_ref[...]
    ms = jnp.mean(x * x, axis=-1, keepdims=True)
    h = x * lax.rsqrt(ms + NORM_EPS) * (1.0 + sc_ref[0]) + sh_ref[0]
    o_ref[...] = h.astype(o_ref.dtype)


def _modulate(x2, mod, shift_k, scale_k, rows_per_batch):
    n, d = x2.shape
    tm = _tile(n if rows_per_batch is None else rows_per_batch, 256)
    row = _mod_row_fn(rows_per_batch, tm)
    return pl.pallas_call(
        _modulate_body,
        grid=(n // tm,),
        in_specs=[pl.BlockSpec((tm, d), lambda i: (i, 0)),
                  pl.BlockSpec((1, 1, d), lambda i: (row(i), 0, shift_k)),
                  pl.BlockSpec((1, 1, d), lambda i: (row(i), 0, scale_k))],
        out_specs=pl.BlockSpec((tm, d), lambda i: (i, 0)),
        out_shape=jax.ShapeDtypeStruct((n, d), BF16),
        compiler_params=_cparams(("parallel",)),
        name="modulate",
    )(x2, mod, mod)


def _route_body(x_ref, sh_ref, sc_ref, wr_ref, bias_ref, h_ref, comb_ref):
    x = x_ref[...]
    ms = jnp.mean(x * x, axis=-1, keepdims=True)
    h = x * lax.rsqrt(ms + NORM_EPS) * (1.0 + sc_ref[0]) + sh_ref[0]
    h_ref[...] = h.astype(h_ref.dtype)
    tm = x.shape[0]
    wr = wr_ref[...]
    hh, hl = _hilo(h)
    wh, wl = _hilo(wr)
    logits = _dot_nt(wh, hh) + _dot_nt(wl, hh) + _dot_nt(wh, hl)
    scores = _sigmoid(logits[:N_EXPERTS])
    per_group = N_EXPERTS // N_GROUPS
    sc3 = scores.reshape(N_GROUPS, per_group, tm)
    sel = sc3 + bias_ref[...]
    midx = lax.broadcasted_iota(jnp.int32, sel.shape, 1)
    neg = jnp.float32(-jnp.inf)
    m1 = jnp.max(sel, axis=1, keepdims=True)
    first = jnp.min(jnp.where(sel == m1, midx, per_group), axis=1, keepdims=True)
    m2 = jnp.max(jnp.where(midx == first, neg, sel), axis=1, keepdims=True)
    gs = (m1 + m2).reshape(N_GROUPS, tm)
    gidx = lax.broadcasted_iota(jnp.int32, gs.shape, 0)
    gmask = jnp.zeros(gs.shape, jnp.bool_)
    for _ in range(TOPK_GROUPS):
        m = jnp.max(gs, axis=0, keepdims=True)
        f = jnp.min(jnp.where(gs == m, gidx, N_GROUPS), axis=0, keepdims=True)
        pick = gidx == f
        gmask = jnp.logical_or(gmask, pick)
        gs = jnp.where(pick, neg, gs)
    val = jnp.where(gmask.reshape(N_GROUPS, 1, tm), sel, neg)
    eidx = lax.broadcasted_iota(jnp.int32, sel.shape, 0) * per_group + midx
    chosen = jnp.zeros(sel.shape, jnp.bool_)
    for _ in range(TOP_K):
        m = jnp.max(jnp.max(val, axis=1, keepdims=True), axis=0, keepdims=True)
        f = jnp.min(jnp.min(jnp.where(val == m, eidx, N_EXPERTS), axis=1, keepdims=True), axis=0, keepdims=True)
        pick = eidx == f
        chosen = jnp.logical_or(chosen, pick)
        val = jnp.where(pick, neg, val)
    w = jnp.where(chosen, sc3, 0.0)
    wsum = jnp.sum(jnp.sum(w, axis=1, keepdims=True), axis=0, keepdims=True)
    comb = (w / wsum * ROUTED_SCALE).reshape(N_EXPERTS, tm)
    comb = jnp.concatenate([comb, jnp.zeros((LANES - N_EXPERTS, tm), F32)], axis=0)
    comb_ref[...] = comb.T


def _modulate_route(x2, mod, shift_k, scale_k, rows_per_batch, router_w_t, router_bias):
    n, d = x2.shape
    tm = _tile(n if rows_per_batch is None else rows_per_batch, 256, LANES)
    row = _mod_row_fn(rows_per_batch, tm)
    return pl.pallas_call(
        _route_body,
        grid=(n // tm,),
        in_specs=[pl.BlockSpec((tm, d), lambda i: (i, 0)),
                  pl.BlockSpec((1, 1, d), lambda i: (row(i), 0, shift_k)),
                  pl.BlockSpec((1, 1, d), lambda i: (row(i), 0, scale_k)),
                  pl.BlockSpec((LANES, d), lambda i: (0, 0)),
                  pl.BlockSpec((N_GROUPS, N_EXPERTS // N_GROUPS, 1), lambda i: (0, 0, 0))],
        out_specs=[pl.BlockSpec((tm, d), lambda i: (i, 0)),
                   pl.BlockSpec((tm, LANES), lambda i: (i, 0))],
        out_shape=[jax.ShapeDtypeStruct((n, d), BF16), jax.ShapeDtypeStruct((n, LANES), F32)],
        compiler_params=_cparams(("parallel",)),
        name="modulate_route",
    )(x2, mod, mod, router_w_t, router_bias.reshape(N_GROUPS, N_EXPERTS // N_GROUPS, 1))


def _group_ones(width, group):
    r = lax.broadcasted_iota(jnp.int32, (width, width), 0) // group
    c = lax.broadcasted_iota(jnp.int32, (width, width), 1) // group
    return (r == c).astype(BF16)


def _qk_prep_body(p_ref, gain_ref, cos_ref, sin_ref, o_ref, *, scale, per_token_tables):
    x = p_ref[...]
    ss = _dot_hp_lhs(x * x, _group_ones(LANES, DA_HEAD_DIM))
    xn = x * lax.rsqrt(ss * (1.0 / DA_HEAD_DIM) + NORM_EPS) * gain_ref[...]
    lane = lax.broadcasted_iota(jnp.int32, x.shape, 1)
    quarter = DA_HEAD_DIM // 4
    partner = jnp.where((lane % (2 * quarter)) < quarter,
                        pltpu.roll(xn, LANES - quarter, axis=1),
                        pltpu.roll(xn, quarter, axis=1))
    if per_token_tables:
        out = xn * cos_ref[...] + partner * sin_ref[...]
    else:
        out = xn * cos_ref[0:1, :] + partner * sin_ref[0:1, :]
    o_ref[...] = (out * scale).astype(o_ref.dtype)


def _qk_prep(p, gain, cos, sin, scale, t_len):
    n, w = p.shape
    tm = _tile(n if t_len is None else t_len, 512)
    if t_len is None:
        tab = lambda i, h: (0, 0)
        tab_shape = (SUBLANES, LANES)
    else:
        nb = t_len // tm
        tab = lambda i, h: (i % nb, 0)
        tab_shape = (tm, LANES)
    gain2 = jnp.tile(gain.reshape(1, DA_HEAD_DIM), (1, 2))
    return pl.pallas_call(
        functools.partial(_qk_prep_body, scale=scale, per_token_tables=t_len is not None),
        grid=(n // tm, w // LANES),
        in_specs=[pl.BlockSpec((tm, LANES), lambda i, h: (i, h)),
                  pl.BlockSpec((1, LANES), lambda i, h: (0, 0)),
                  pl.BlockSpec(tab_shape, tab),
                  pl.BlockSpec(tab_shape, tab)],
        out_specs=pl.BlockSpec((tm, LANES), lambda i, h: (i, h)),
        out_shape=jax.ShapeDtypeStruct((n, w), BF16),
        compiler_params=_cparams(("parallel", "parallel")),
        name="qk_prep",
    )(p, gain2, cos, sin)


def _rope_tables(t_len):
    rows = t_len // GRID_W
    row = jnp.repeat(jnp.arange(rows, dtype=F32), GRID_W)
    col = jnp.tile(jnp.arange(GRID_W, dtype=F32), rows)
    half = DA_HEAD_DIM // 2
    inv_freq = 1.0 / (ROPE_BASE ** (jnp.arange(0, half, 2, dtype=F32) / half))
    ang_r = row[:, None] * inv_freq
    ang_c = col[:, None] * inv_freq
    cos64 = jnp.concatenate([jnp.cos(ang_r), jnp.cos(ang_r), jnp.cos(ang_c), jnp.cos(ang_c)], axis=1)
    sin64 = jnp.concatenate([-jnp.sin(ang_r), jnp.sin(ang_r), -jnp.sin(ang_c), jnp.sin(ang_c)], axis=1)
    return jnp.tile(cos64, (1, 2)), jnp.tile(sin64, (1, 2))


def _attn_body(*refs, has_lat, lam_init):
    if has_lat:
        q_ref, kl_ref, vl_ref, kc_ref, vc_ref, lq_ref, lk_ref, sub_ref, o_ref = refs
    else:
        q_ref, kc_ref, vc_ref, lq_ref, lk_ref, sub_ref, o_ref = refs
    lqk = lq_ref[...] * lk_ref[...]
    lsum = jnp.sum(lqk, axis=1, keepdims=True)
    e = jnp.exp(lsum)
    lam = e[0:1, :] - e[1:2, :] + lam_init
    q = q_ref[...]
    lane = lax.broadcasted_iota(jnp.int32, q.shape, 1)
    zero = jnp.zeros_like(q)
    probs = []
    for m in range(2):
        in_map = (lane // DA_HEAD_DIM) == m
        qm = jnp.where(in_map, q, zero)
        s_c = _dot_nt(qm, kc_ref[...])
        mx = jnp.max(s_c, axis=-1, keepdims=True)
        if has_lat:
            s_l = _dot_nt(qm, kl_ref[...])
            mx = jnp.maximum(mx, jnp.max(s_l, axis=-1, keepdims=True))
            e_l = jnp.exp(s_l - mx)
        e_c = jnp.exp(s_c - mx)
        den = jnp.sum(e_c, axis=-1, keepdims=True)
        if has_lat:
            den = den + jnp.sum(e_l, axis=-1, keepdims=True)
            probs.append((e_l / den, e_c / den))
        else:
            probs.append((None, e_c / den))
    a_c = probs[0][1] - lam * probs[1][1]
    o = _dot(a_c, vc_ref[...])
    if has_lat:
        a_l = probs[0][0] - lam * probs[1][0]
        o = o + _dot(a_l, vl_ref[...])
    ms = jnp.mean(o * o, axis=-1, keepdims=True)
    o = o * lax.rsqrt(ms + NORM_EPS) * sub_ref[...] * (1.0 - lam_init)
    o_ref[...] = o.astype(o_ref.dtype)


def _diff_attention(q, k_lat, v_lat, k_ctx, v_ctx, lq, lk, subln, lam_init, batch):
    n, w = q.shape
    tq_len = n // batch
    c_len = k_ctx.shape[0] // batch
    tq = _tile(tq_len, 256)
    nq = tq_len // tq
    has_lat = k_lat is not None
    blk = lambda rows: pl.BlockSpec((rows, LANES), lambda b, h, i: (b, h))
    in_specs = [pl.BlockSpec((tq, LANES), lambda b, h, i: (b * nq + i, h))]
    args = [q]
    if has_lat:
        t_len = k_lat.shape[0] // batch
        in_specs += [blk(t_len), blk(t_len)]
        args += [k_lat, v_lat]
    in_specs += [blk(c_len), blk(c_len),
                 pl.BlockSpec((2, DA_HEAD_DIM), lambda b, h, i: (0, 0)),
                 pl.BlockSpec((2, DA_HEAD_DIM), lambda b, h, i: (0, 0)),
                 pl.BlockSpec((1, LANES), lambda b, h, i: (0, 0))]
    args += [k_ctx, v_ctx, lq, lk, subln.reshape(1, DA_V_DIM)]
    return pl.pallas_call(
        functools.partial(_attn_body, has_lat=has_lat, lam_init=lam_init),
        grid=(batch, w // LANES, nq),
        in_specs=in_specs,
        out_specs=pl.BlockSpec((tq, LANES), lambda b, h, i: (b * nq + i, h)),
        out_shape=jax.ShapeDtypeStruct((n, w), BF16),
        compiler_params=_cparams(("parallel", "parallel", "arbitrary")),
        name="diff_attention",
    )(*args)


def _rwkv_prep_body(x_ref, xp_ref, xn_ref, conv_ref, lora_ref, w0a0_ref, kk_w_ref, ka_w_ref,
                    lwf_ref, lwb_ref, kk_ref, kkaf_ref, kkab_ref, kdf_ref, kdb_ref, v_ref, r_ref, gd_ref,
                    *, blocks_per_seq):
    i = pl.program_id(0)
    x = x_ref[...]
    tm = x.shape[0]
    first = (i % blocks_per_seq) == 0
    last = (i % blocks_per_seq) == blocks_per_seq - 1
    xp = jnp.where(first, 0.0, xp_ref[SUBLANES - 1:SUBLANES, :])
    xn = jnp.where(last, 0.0, xn_ref[0:1, :])
    row = lax.broadcasted_iota(jnp.int32, (tm, 1), 0)
    up = jnp.where(row == 0, xp, pltpu.roll(x, 1, axis=0))
    dn = jnp.where(row == tm - 1, xn, pltpu.roll(x, tm - 1, axis=0))
    cw = conv_ref[...]
    cv = up * cw[0:1, :] + x * cw[1:2, :] + dn * cw[2:3, :]
    w = RW_WIDTH
    k = cv[:, :w]
    v = cv[:, w:2 * w]
    lora_in = cv[:, 2 * w:RW_STATE_COLS]
    r = cv[:, RW_STATE_COLS:RW_STATE_COLS + w]
    gd = cv[:, RW_STATE_COLS + w:]
    lane = lax.broadcasted_iota(jnp.int32, lora_in.shape, 1)
    li = jnp.where(lane < 2 * RW_DECAY_RANK, jnp.tanh(lora_in), lora_in)
    pre = _dot_hp(li, lora_ref[...]) + w0a0_ref[...]
    kkr = k * kk_w_ref[...]
    ss = _dot_hp_lhs(kkr * kkr, _group_ones(w, RW_HEAD_DIM))
    kk = kkr * lax.rsqrt(ss + 1e-12)
    kk_ref[...] = kk
    v_ref[...] = v
    r_ref[...] = r
    gd_ref[...] = gd
    ka = ka_w_ref[...]
    for d, (lw_ref, kka_ref, kd_ref) in enumerate(((lwf_ref, kkaf_ref, kdf_ref), (lwb_ref, kkab_ref, kdb_ref))):
        w_log = -_softplus(-pre[:, d * w:(d + 1) * w]) - 0.5
        lw_ref[...] = -jnp.exp(w_log)
        a = _sigmoid(pre[:, (2 + d) * w:(3 + d) * w])
        kka_ref[...] = kk * a
        kd_ref[...] = k * (1.0 + (a - 1.0) * ka)


def _rwkv_prep(p_rw, seq_len, conv, lora_w, w0a0, k_k, k_a):
    n, c = p_rw.shape
    tm = _tile(seq_len, 128)
    bps = seq_len // tm
    sub = tm // SUBLANES
    nsub = n // SUBLANES
    w = RW_WIDTH
    wide = lambda: pl.BlockSpec((tm, w), lambda i: (i, 0))
    outs = [jax.ShapeDtypeStruct((n, w), F32)] * 9 + [jax.ShapeDtypeStruct((n, RW_GATE_RANK), F32)]
    return pl.pallas_call(
        functools.partial(_rwkv_prep_body, blocks_per_seq=bps),
        grid=(n // tm,),
        in_specs=[pl.BlockSpec((tm, c), lambda i: (i, 0)),
                  pl.BlockSpec((SUBLANES, c), lambda i: (jnp.maximum(i * sub - 1, 0), 0)),
                  pl.BlockSpec((SUBLANES, c), lambda i: (jnp.minimum((i + 1) * sub, nsub - 1), 0)),
                  pl.BlockSpec((3, c), lambda i: (0, 0)),
                  pl.BlockSpec(lora_w.shape, lambda i: (0, 0)),
                  pl.BlockSpec((1, 4 * w), lambda i: (0, 0)),
                  pl.BlockSpec((1, w), lambda i: (0, 0)),
                  pl.BlockSpec((1, w), lambda i: (0, 0))],
        out_specs=[wide() for _ in range(9)] + [pl.BlockSpec((tm, RW_GATE_RANK), lambda i: (i, 0))],
        out_shape=outs,
        compiler_params=_cparams(("parallel",)),
        name="rwkv_prep",
    )(p_rw, p_rw, p_rw, conv, lora_w, w0a0, k_k.reshape(1, w), k_a.reshape(1, w))


def _rwkv_scan_body(lw_ref, kk_ref, kka_ref, k_ref, v_ref, r_ref, s0_ref, o_ref, st_ref, *, reverse, want_out):
    ci = pl.program_id(2)

    @pl.when(ci == 0)
    def _():
        st_ref[...] = s0_ref[...]

    lw = lw_ref[...]
    n_l = lw.shape[0]
    hd = RW_HEAD_DIM
    hps = lw.shape[1] // hd
    row = lax.broadcasted_iota(jnp.int32, (n_l, n_l), 0)
    col = lax.broadcasted_iota(jnp.int32, (n_l, n_l), 1)
    if reverse:
        incl, strict = col >= row, col > row
    else:
        incl, strict = col <= row, col < row
    tri = incl.astype(BF16)
    lh, ll = _hilo(lw)
    c = jnp.dot(tri, lh, preferred_element_type=F32) + jnp.dot(tri, ll, preferred_element_type=F32)
    g_end = jnp.exp(c[0:1, :] if reverse else c[n_l - 1:n_l, :])
    e_inv = jnp.exp(-c)
    a_t = kk_ref[...] * jnp.exp(c - lw)
    r_t = r_ref[...] * jnp.exp(c)
    k_h = k_ref[...] * e_inv
    b_h = kka_ref[...] * e_inv
    v_all = v_ref[...]
    eye = (row == col).astype(F32)
    for j in range(hps):
        sl = slice(j * hd, (j + 1) * hd)
        ar = jnp.concatenate([a_t[:, sl], r_t[:, sl]], axis=0)
        kb = jnp.concatenate([k_h[:, sl], b_h[:, sl]], axis=0)
        s = st_ref[j]
        scores = _dot_nt(ar, kb)
        from_s = _dot_nt(ar, s)
        a_ak = jnp.where(strict, scores[:n_l, :n_l], 0.0)
        a_ab = jnp.where(strict, scores[:n_l, n_l:], 0.0)
        p = -a_ab
        t_inv = eye + p
        steps = max(1, (n_l - 1).bit_length()) - 1
        for _ in range(steps):
            p = _dot(p, p)
            t_inv = t_inv + _dot(t_inv, p)
        v = v_all[:, sl]
        u = _dot(t_inv, from_s[:n_l] + _dot(a_ak, v))
        vu = jnp.concatenate([v, u], axis=0)
        if want_out:
            a_rk = jnp.where(incl, scores[n_l:, :n_l], 0.0)
            a_rb = jnp.where(incl, scores[n_l:, n_l:], 0.0)
            o = from_s[n_l:] + _dot(jnp.concatenate([a_rk, -a_rb], axis=1), vu)
            o_ref[:, sl] = o
        ge = g_end[:, sl]
        kbg = jnp.concatenate([k_h[:, sl] * ge, -(b_h[:, sl] * ge)], axis=0)
        st_ref[j] = s * ge + _dot_tn(vu, kbg)
    if not want_out:
        o_ref[...] = jnp.zeros_like(o_ref)


def _rwkv_scan(lw, kk, kka, k, v, r, s0, batch, *, reverse, want_out=True):
    n, w = lw.shape
    seq = n // batch
    n_l = _tile(seq, RW_CHUNK)
    nch = seq // n_l
    hps = RW_HEADS_PER_STEP
    hw = hps * RW_HEAD_DIM
    chunk = (lambda c: nch - 1 - c) if reverse else (lambda c: c)
    seq_spec = pl.BlockSpec((n_l, hw), lambda b, g, c: (b * nch + chunk(c), g))
    st_spec = pl.BlockSpec((hps, RW_HEAD_DIM, RW_HEAD_DIM), lambda b, g, c: (b * (w // hw) + g, 0, 0))
    return pl.pallas_call(
        functools.partial(_rwkv_scan_body, reverse=reverse, want_out=want_out),
        grid=(batch, w // hw, nch),
        in_specs=[seq_spec] * 6 + [st_spec],
        out_specs=[seq_spec, st_spec],
        out_shape=[jax.ShapeDtypeStruct((n, w), F32), jax.ShapeDtypeStruct(s0.shape, F32)],
        compiler_params=_cparams(("parallel", "parallel", "arbitrary")),
        name="rwkv_scan_rev" if reverse else "rwkv_scan_fwd",
    )(lw, kk, kka, k, v, r, s0)


def _rwkv_readout_body(of_ref, ob_ref, r_ref, kdf_ref, kdb_ref, v_ref, gd_ref, gnw_ref, gnb_ref, rk_ref, g2_ref, y_ref):
    ones = _group_ones(RW_WIDTH, RW_HEAD_DIM)
    inv = 1.0 / RW_HEAD_DIM
    o = of_ref[...] + ob_ref[...]
    mu = _dot_hp_lhs(o, ones) * inv
    d = o - mu
    var = _dot_hp_lhs(d * d, ones) * inv
    on = d * lax.rsqrt(var + RW_GN_EPS) * gnw_ref[...] + gnb_ref[...]
    r = r_ref[...]
    rk = rk_ref[...]
    bonus = _dot_hp_lhs(r * kdf_ref[...] * rk, ones) + _dot_hp_lhs(r * kdb_ref[...] * rk, ones)
    y = on + bonus * v_ref[...]
    g = _dot(_sigmoid(gd_ref[...]), g2_ref[...])
    y_ref[...] = (y * g).astype(y_ref.dtype)


def _rwkv_readout(o_f, o_b, r, kd_f, kd_b, v, gd, gn_w, gn_b, r_k, g2):
    n, w = o_f.shape
    tm = _tile(n, 256)
    wide = pl.BlockSpec((tm, w), lambda i: (i, 0))
    vec = pl.BlockSpec((1, w), lambda i: (0, 0))
    return pl.pallas_call(
        _rwkv_readout_body,
        grid=(n // tm,),
        in_specs=[wide] * 6 + [pl.BlockSpec((tm, RW_GATE_RANK), lambda i: (i, 0)), vec, vec, vec,
                               pl.BlockSpec((RW_GATE_RANK, w), lambda i: (0, 0))],
        out_specs=wide,
        out_shape=jax.ShapeDtypeStruct((n, w), BF16),
        compiler_params=_cparams(("parallel",)),
        name="rwkv_readout",
    )(o_f, o_b, r, kd_f, kd_b, v, gd, gn_w.reshape(1, w), gn_b.reshape(1, w), r_k.reshape(1, w), g2)


def _s5_scan_body(u_ref, bre_ref, bim_ref, are_ref, aim_ref, cre_ref, cim_ref, h0_ref, y_ref, ht_ref,
                  dre, dim, *, reverse, want_out):
    ci = pl.program_id(0)

    @pl.when(ci == 0)
    def _():
        ht_ref[...] = h0_ref[...]

    tt, nb, wu = u_ref.shape
    nblk = wu // LANES
    sw = S5_LANES // nblk
    u2 = u_ref[...].reshape(tt * nb, wu).astype(BF16)
    for c in range(nblk):
        uc = u2[:, c * LANES:(c + 1) * LANES]
        dre[:, :, c * sw:(c + 1) * sw] = jnp.dot(uc, bre_ref[c], preferred_element_type=F32).reshape(tt, nb, sw)
        dim[:, :, c * sw:(c + 1) * sw] = jnp.dot(uc, bim_ref[c], preferred_element_type=F32).reshape(tt, nb, sw)
    lw = 1024
    for c in range(S5_LANES // lw):
        ls = slice(c * lw, (c + 1) * lw)
        ar = jnp.broadcast_to(are_ref[:, ls], (nb, lw))
        ai = jnp.broadcast_to(aim_ref[:, ls], (nb, lw))

        def step(s, carry, ls=ls, ar=ar, ai=ai):
            t = (tt - 1 - s) if reverse else s
            hr, hi = carry
            nr = ar * hr - ai * hi + dre[t, :, ls]
            ni = ar * hi + ai * hr + dim[t, :, ls]
            dre[t, :, ls] = nr
            dim[t, :, ls] = ni
            return nr, ni

        hr, hi = lax.fori_loop(0, tt, step, (ht_ref[0, :, ls], ht_ref[1, :, ls]), unroll=2)
        ht_ref[0, :, ls] = hr
        ht_ref[1, :, ls] = hi
    if want_out:
        xr = dre[...].reshape(tt * nb, S5_LANES).astype(BF16)
        xi = dim[...].reshape(tt * nb, S5_LANES).astype(BF16)
        for c in range(nblk):
            yc = (jnp.dot(xr[:, c * sw:(c + 1) * sw], cre_ref[c], preferred_element_type=F32)
                  - jnp.dot(xi[:, c * sw:(c + 1) * sw], cim_ref[c], preferred_element_type=F32))
            y_ref[:, :, c * LANES:(c + 1) * LANES] = yc.reshape(tt, nb, LANES)
    else:
        y_ref[...] = jnp.zeros_like(y_ref)


def _s5_scan(u_tm, h0, p, *, reverse, want_out=True):
    t_len, nb, wu = u_tm.shape
    tt = _tile(t_len, S5_CHUNK)
    nch = t_len // tt
    chunk = (lambda c: (nch - 1 - c, 0, 0)) if reverse else (lambda c: (c, 0, 0))
    const3 = lambda a: pl.BlockSpec(a.shape, lambda c: (0, 0, 0))
    const2 = lambda a: pl.BlockSpec(a.shape, lambda c: (0, 0))
    return pl.pallas_call(
        functools.partial(_s5_scan_body, reverse=reverse, want_out=want_out),
        grid=(nch,),
        in_specs=[pl.BlockSpec((tt, nb, wu), chunk), const3(p["b_re"]), const3(p["b_im"]),
                  const2(p["a_re"]), const2(p["a_im"]), const3(p["c_re"]), const3(p["c_im"]), const3(h0)],
        out_specs=[pl.BlockSpec((tt, nb, wu), chunk), const3(h0)],
        out_shape=[jax.ShapeDtypeStruct(u_tm.shape, F32), jax.ShapeDtypeStruct(h0.shape, F32)],
        scratch_shapes=[pltpu.VMEM((tt, nb, S5_LANES), F32), pltpu.VMEM((tt, nb, S5_LANES), F32)],
        compiler_params=_cparams(("arbitrary",)),
        name="s5_scan_rev" if reverse else "s5_scan_fwd",
    )(u_tm, p["b_re"], p["b_im"], p["a_re"], p["a_im"], p["c_re"], p["c_im"], h0)


def _s5_dir_params(lam_re, lam_im, log_dt, b_re, b_im, c_re, c_im):
    g, pdim = lam_re.shape
    dt = jnp.exp(log_dt.astype(F32))[:, None]
    mag = jnp.exp(lam_re * dt)
    abar_re = mag * jnp.cos(lam_im * dt)
    abar_im = mag * jnp.sin(lam_im * dt)
    den = lam_re * lam_re + lam_im * lam_im
    nr = abar_re - 1.0
    g_re = (nr * lam_re + abar_im * lam_im) / den
    g_im = (abar_im * lam_re - nr * lam_im) / den
    bb_re = g_re[:, :, None] * b_re - g_im[:, :, None] * b_im
    bb_im = g_re[:, :, None] * b_im + g_im[:, :, None] * b_re
    gpb = LANES // S5_GROUP
    nblk = g // gpb
    eye = jnp.eye(gpb, dtype=F32)

    def drive_mat(bb):
        x = bb.reshape(nblk, gpb, pdim, S5_GROUP)
        x = jnp.einsum("cgph,gk->cghkp", x, eye)
        return x.reshape(nblk, gpb * S5_GROUP, gpb * pdim).astype(BF16)

    def read_mat(cc):
        x = cc.reshape(nblk, gpb, S5_GROUP, pdim)
        x = jnp.einsum("cghp,gk->cgpkh", x, eye)
        return x.reshape(nblk, gpb * pdim, gpb * S5_GROUP).astype(BF16)

    return {"a_re": abar_re.reshape(1, g * pdim), "a_im": abar_im.reshape(1, g * pdim),
            "b_re": drive_mat(bb_re), "b_im": drive_mat(bb_im),
            "c_re": read_mat(c_re.astype(F32)), "c_im": read_mat(c_im.astype(F32))}


def _s5_glu_body(u_ref, yf_ref, yb_ref, d_ref, w_ref, b_ref, o_ref):
    y = u_ref[...] * d_ref[...] + yf_ref[...] + yb_ref[...]
    y = _gelu_tanh(y)
    z = _dot(y, w_ref[...]) + b_ref[...]
    o_ref[...] = (y * _sigmoid(z)).astype(o_ref.dtype)


def _s5_glu(u, y_f, y_b, d_skip, glu_w, glu_b):
    n, w = u.shape
    tm = _tile(n, 512)
    wide = pl.BlockSpec((tm, w), lambda i: (i, 0))
    vec = pl.BlockSpec((1, w), lambda i: (0, 0))
    return pl.pallas_call(
        _s5_glu_body,
        grid=(n // tm,),
        in_specs=[wide, wide, wide, vec, pl.BlockSpec((w, w), lambda i: (0, 0)), vec],
        out_specs=wide,
        out_shape=jax.ShapeDtypeStruct((n, w), BF16),
        compiler_params=_cparams(("parallel",)),
        name="s5_glu",
    )(u, y_f, y_b, d_skip.reshape(1, w), glu_w, glu_b.reshape(1, w))


def _merge_body(ya_ref, yr_ref, ys_ref, wa_ref, wr_ref, ws_ref, ga_ref, gr_ref, gs_ref, o_ref):
    m = (ga_ref[...].astype(F32) * jnp.dot(ya_ref[...], wa_ref[...], preferred_element_type=F32)
         + gr_ref[...].astype(F32) * jnp.dot(yr_ref[...], wr_ref[...], preferred_element_type=F32)
         + gs_ref[...].astype(F32) * jnp.dot(ys_ref[...], ws_ref[...], preferred_element_type=F32))
    o_ref[...] = m.astype(o_ref.dtype)


def _merge(y_a, y_r, y_s_tm, gates, w_a, w_r, w_s, seq_len, batch):
    n = y_a.shape[0]
    d = w_a.shape[1]
    tm = _tile(seq_len, 512)
    tn = _tile(d, 1024, LANES)
    nt = seq_len // tm
    nd = d // tn
    row = lambda i, j: (i, 0)
    return pl.pallas_call(
        _merge_body,
        grid=(n // tm, nd),
        in_specs=[pl.BlockSpec((tm, y_a.shape[1]), row),
                  pl.BlockSpec((tm, y_r.shape[1]), row),
                  pl.BlockSpec((tm, S5_WIDTH), lambda i, j: (i % nt, i // nt)),
                  pl.BlockSpec((w_a.shape[0], tn), lambda i, j: (0, j)),
                  pl.BlockSpec((w_r.shape[0], tn), lambda i, j: (0, j)),
                  pl.BlockSpec((w_s.shape[0], tn), lambda i, j: (0, j)),
                  pl.BlockSpec((tm, tn), lambda i, j: (i, j)),
                  pl.BlockSpec((tm, tn), lambda i, j: (i, nd + j)),
                  pl.BlockSpec((tm, tn), lambda i, j: (i, 2 * nd + j))],
        out_specs=pl.BlockSpec((tm, tn), lambda i, j: (i, j)),
        out_shape=jax.ShapeDtypeStruct((n, d), BF16),
        compiler_params=_cparams(("parallel", "arbitrary")),
        name="merge",
    )(y_a, y_r, y_s_tm, w_a, w_r, w_s, gates, gates, gates)


def _moe_body(h_ref, wg_ref, wu_ref, wd_ref, comb_ref, o_ref, acc_ref):
    e = pl.program_id(1)

    @pl.when(e == 0)
    def _():
        acc_ref[...] = jnp.zeros_like(acc_ref)

    h = h_ref[...]
    hg = jnp.dot(h, wg_ref[0], preferred_element_type=F32)
    hu = jnp.dot(h, wu_ref[0], preferred_element_type=F32)
    lane = lax.broadcasted_iota(jnp.int32, comb_ref.shape, 1)
    gate = jnp.sum(jnp.where(lane == e, comb_ref[...], 0.0), axis=1, keepdims=True)
    act = hg * _sigmoid(hg) * hu * gate
    acc_ref[...] += jnp.dot(act.astype(BF16), wd_ref[0], preferred_element_type=F32)

    @pl.when(e == pl.num_programs(1) - 1)
    def _():
        o_ref[...] = acc_ref[...].astype(o_ref.dtype)


def _moe_routed(h, w_gate, w_up, w_down, combine):
    n, d = h.shape
    ne, _, ff = w_gate.shape
    tm = _tile(n, 512)
    row = lambda i, e: (i, 0)
    return pl.pallas_call(
        _moe_body,
        grid=(n // tm, ne),
        in_specs=[pl.BlockSpec((tm, d), row),
                  pl.BlockSpec((1, d, ff), lambda i, e: (e, 0, 0)),
                  pl.BlockSpec((1, d, ff), lambda i, e: (e, 0, 0)),
                  pl.BlockSpec((1, ff, d), lambda i, e: (e, 0, 0)),
                  pl.BlockSpec((tm, LANES), row)],
        out_specs=pl.BlockSpec((tm, d), row),
        out_shape=jax.ShapeDtypeStruct((n, d), F32),
        scratch_shapes=[pltpu.VMEM((tm, d), F32)],
        compiler_params=_cparams(("parallel", "arbitrary")),
        name="moe_routed",
    )(h, w_gate, w_up, w_down, combine)


def _moe_shared_residual(h, w_gate, w_up, w_down, f_routed, x2, mod, gate_k, rows_per_batch):
    n, d = x2.shape
    g_act = _matmul(h, w_gate, F32, name="shared_gate")
    tm = _tile(n, 512)
    tn = _tile(w_up.shape[1], 1024, LANES)
    act = _matmul(h, w_up, BF16, tm=tm, tn=tn, epilogue=lambda acc, g: g * _sigmoid(g) * acc,
                  extras=[(g_act, pl.BlockSpec((tm, tn), lambda i, j: (i, j)))], name="shared_up")
    tm = _tile(n if rows_per_batch is None else rows_per_batch, 512)
    tn = _tile(d, 1024, LANES)
    nd = d // tn
    mrow = _mod_row_fn(rows_per_batch, tm)
    blk = pl.BlockSpec((tm, tn), lambda i, j: (i, j))
    return _matmul(act, w_down, F32, tm=tm, tn=tn,
                   epilogue=lambda acc, fr, xb, g: xb + g[0] * (acc + fr),
                   extras=[(f_routed, blk), (x2, blk),
                           (mod, pl.BlockSpec((1, 1, tn), lambda i, j: (mrow(i), 0, gate_k * nd + j)))],
                   name="shared_down")


def _token_mixer(h, hc, lw, lam_init, tabs, need_ctx, batch, t_len, c_len):
    cos, sin = tabs
    ones_tab = jnp.ones((SUBLANES, LANES), F32)
    zeros_tab = jnp.zeros((SUBLANES, LANES), F32)

    def project(hh, seq_len, full):
        n = hh.shape[0]
        out = {}
        out["k"] = _matmul(hh, lw["w_k"], F32, name="proj_k")
        out["v"] = _matmul(hh, lw["w_v"], BF16, name="proj_v")
        out["rw"] = _matmul(hh, lw["w_rw"], F32, name="proj_rw")
        tm = _tile(seq_len, 512)
        nt = seq_len // tm
        tn = S5_WIDTH
        out["s5"] = _matmul(
            hh, lw["w_s5"], F32, tn=tn, grid_m=n // tm,
            a_spec=pl.BlockSpec((tm, hh.shape[1]), lambda i, j: (i, 0)),
            out_spec=pl.BlockSpec((tm, tn), lambda i, j: (i % nt, i // nt)),
            out_shape=(seq_len, batch * S5_WIDTH), name="proj_s5").reshape(seq_len, batch, S5_WIDTH)
        if full:
            out["q"] = _matmul(hh, lw["w_q"], F32, name="proj_q")
            out["gates"] = _matmul(hh, lw["w_gates"], BF16, epilogue=_sigmoid, name="proj_gates")
        return out

    pl_ = project(h, t_len, True)
    pc_ = project(hc, c_len, need_ctx)

    k_l = _qk_prep(pl_["k"], lw["da_k_norm"], cos, sin, 1.0, t_len)
    q_l = _qk_prep(pl_["q"], lw["da_q_norm"], cos, sin, DA_SCALE, t_len)
    k_c = _qk_prep(pc_["k"], lw["da_k_norm"], ones_tab, zeros_tab, 1.0, None)
    y_a = _diff_attention(q_l, k_l, pl_["v"], k_c, pc_["v"], lw["da_lambda_q"], lw["da_lambda_k"],
                          lw["da_subln"], lam_init, batch)
    y_ac = None
    if need_ctx:
        q_c = _qk_prep(pc_["q"], lw["da_q_norm"], ones_tab, zeros_tab, DA_SCALE, None)
        y_ac = _diff_attention(q_c, None, None, k_c, pc_["v"], lw["da_lambda_q"], lw["da_lambda_k"],
                               lw["da_subln"], lam_init, batch)

    def prep(p_rw, seq_len):
        names = ("lw_f", "lw_b", "kk", "kka_f", "kka_b", "kd_f", "kd_b", "v", "r", "gd")
        vals = _rwkv_prep(p_rw, seq_len, lw["rw_conv"], lw["rw_lora"], lw["rw_w0a0"], lw["rw_k_k"], lw["rw_k_a"])
        return dict(zip(names, vals))

    tl, tc = prep(pl_["rw"], t_len), prep(pc_["rw"], c_len)
    s_zero = jnp.zeros((batch * RW_HEADS, RW_HEAD_DIM, RW_HEAD_DIM), F32)
    o_lat, o_ctx = {}, {}
    for d, rev in (("f", False), ("b", True)):
        oc, s_ctx = _rwkv_scan(tc["lw_" + d], tc["kk"], tc["kka_" + d], tc["kd_" + d], tc["v"], tc["r"], s_zero,
                               batch, reverse=rev, want_out=need_ctx)
        ol, _ = _rwkv_scan(tl["lw_" + d], tl["kk"], tl["kka_" + d], tl["kd_" + d], tl["v"], tl["r"], s_ctx,
                           batch, reverse=rev)
        o_lat[d], o_ctx[d] = ol, oc
    ro = lambda o, t: _rwkv_readout(o["f"], o["b"], t["r"], t["kd_f"], t["kd_b"], t["v"], t["gd"],
                                    lw["rw_gn_w"], lw["rw_gn_b"], lw["rw_r_k"], lw["rw_g2"])
    y_r = ro(o_lat, tl)
    y_rc = ro(o_ctx, tc) if need_ctx else None

    h_zero = jnp.zeros((2, batch, S5_LANES), F32)
    ys_lat, ys_ctx = {}, {}
    for d, rev in (("f", False), ("b", True)):
        yc, h_ctx = _s5_scan(pc_["s5"], h_zero, lw["s5_" + d], reverse=rev, want_out=need_ctx)
        yl, _ = _s5_scan(pl_["s5"], h_ctx, lw["s5_" + d], reverse=rev)
        ys_lat[d], ys_ctx[d] = yl, yc
    flat = lambda a: a.reshape(a.shape[0] * batch, S5_WIDTH)
    glu = lambda p, ys, seq: _s5_glu(flat(p["s5"]), flat(ys["f"]), flat(ys["b"]), lw["s5_d"], lw["s5_glu_w"],
                                     lw["s5_glu_b"]).reshape(seq, batch * S5_WIDTH)
    y_s = glu(pl_, ys_lat, t_len)
    y_sc = glu(pc_, ys_ctx, c_len) if need_ctx else None

    m = _merge(y_a, y_r, y_s, pl_["gates"], lw["w_branch_a"], lw["w_branch_r"], lw["w_branch_s"], t_len, batch)
    m_c = None
    if need_ctx:
        m_c = _merge(y_ac, y_rc, y_sc, pc_["gates"], lw["w_branch_a"], lw["w_branch_r"], lw["w_branch_s"],
                     c_len, batch)
    return m, m_c


def _out_proj_residual(m, w_out, x2, mod, gate_k, rows_per_batch):
    n, d = x2.shape
    tm = _tile(n if rows_per_batch is None else rows_per_batch, 512)
    tn = _tile(d, 1024, LANES)
    nd = d // tn
    mrow = _mod_row_fn(rows_per_batch, tm)

    def epilogue(acc, xb, g):
        return xb + g[0] * acc

    return _matmul(m, w_out, F32, tm=tm, tn=tn, epilogue=epilogue,
                   extras=[(x2, pl.BlockSpec((tm, tn), lambda i, j: (i, j))),
                           (mod, pl.BlockSpec((1, 1, tn), lambda i, j: (mrow(i), 0, gate_k * nd + j)))],
                   name="out_proj")


def _prepare_layer(i, p):
    w_in = p["w_in"][i]
    c0 = DA_WIDTH
    c1 = 2 * DA_WIDTH
    c2 = c1 + RW_STATE_COLS
    c3 = c2 + S5_WIDTH
    c4 = c3 + DA_WIDTH
    c5 = c4 + RW_OUT_COLS
    bf = lambda a: a.astype(BF16)
    lw = {
        "w_k": bf(w_in[:, :c0]), "w_v": bf(w_in[:, c0:c1]),
        "w_rw": bf(jnp.concatenate([w_in[:, c1:c2], w_in[:, c4:c5]], axis=1)),
        "w_s5": bf(w_in[:, c2:c3]), "w_q": bf(w_in[:, c3:c4]), "w_gates": bf(w_in[:, c5:]),
    }
    for name in ("da_q_norm", "da_k_norm", "da_lambda_q", "da_lambda_k", "da_subln", "rw_conv", "rw_k_k", "rw_k_a",
                 "rw_gn_w", "rw_gn_b", "s5_d", "s5_glu_b", "router_bias"):
        lw[name] = p[name][i].astype(F32)
    w = RW_WIDTH
    lora = jnp.zeros((2 * RW_DECAY_RANK + 2 * RW_A_RANK, 4 * w), F32)
    r0 = 0
    for blk, src in enumerate((p["rw_w2"][i][0], p["rw_w2"][i][1], p["rw_a2"][i][0], p["rw_a2"][i][1])):
        lora = lora.at[r0:r0 + src.shape[0], blk * w:(blk + 1) * w].set(src.astype(F32))
        r0 += src.shape[0]
    lw["rw_lora"] = lora
    lw["rw_w0a0"] = jnp.concatenate([p["rw_w0"][i][0], p["rw_w0"][i][1], p["rw_a0"][i][0], p["rw_a0"][i][1]]
                                    ).astype(F32).reshape(1, 4 * w)
    lw["rw_r_k"] = p["rw_r_k"][i].astype(F32).reshape(w)
    lw["rw_g2"] = bf(p["rw_g2"][i])
    for d, name in enumerate(("s5_f", "s5_b")):
        lw[name] = _s5_dir_params(p["s5_lambda_re"][i][d], p["s5_lambda_im"][i][d], p["s5_log_dt"][i][d],
                                  p["s5_b_re"][i].astype(F32), p["s5_b_im"][i].astype(F32),
                                  p["s5_c_re"][i][d], p["s5_c_im"][i][d])
    lw["s5_glu_w"] = bf(p["s5_glu_w"][i])
    for name in ("w_branch_a", "w_branch_r", "w_branch_s", "w_out", "exp_w_gate", "exp_w_up", "exp_w_down"):
        lw[name] = bf(p[name][i])
    d_model = w_in.shape[0]
    rw_t = p["router_w"][i].astype(F32).T
    lw["router_w_t"] = jnp.concatenate([rw_t, jnp.zeros((LANES - N_EXPERTS, d_model), F32)], axis=0)
    for name in ("sh_w_gate", "sh_w_up", "sh_w_down"):
        lw[name] = bf(p[name][i])
    return lw


def kernel(x, c, ctx, c_ctx, ada_w, ada_b, w_in, da_q_norm, da_k_norm, da_lambda_q, da_lambda_k, da_subln, rw_conv, rw_w0, rw_w2, rw_a0, rw_a2, rw_g2, rw_k_k, rw_k_a, rw_r_k, rw_gn_w, rw_gn_b, s5_lambda_re, s5_lambda_im, s5_log_dt, s5_b_re, s5_b_im, s5_c_re, s5_c_im, s5_d, s5_glu_w, s5_glu_b, w_branch_a, w_branch_r, w_branch_s, w_out, router_w, router_bias, exp_w_gate, exp_w_up, exp_w_down, sh_w_gate, sh_w_up, sh_w_down):
    params = dict(w_in=w_in, da_q_norm=da_q_norm, da_k_norm=da_k_norm, da_lambda_q=da_lambda_q,
                  da_lambda_k=da_lambda_k, da_subln=da_subln, rw_conv=rw_conv, rw_w0=rw_w0, rw_w2=rw_w2,
                  rw_a0=rw_a0, rw_a2=rw_a2, rw_g2=rw_g2, rw_k_k=rw_k_k, rw_k_a=rw_k_a, rw_r_k=rw_r_k,
                  rw_gn_w=rw_gn_w, rw_gn_b=rw_gn_b, s5_lambda_re=s5_lambda_re, s5_lambda_im=s5_lambda_im,
                  s5_log_dt=s5_log_dt, s5_b_re=s5_b_re, s5_b_im=s5_b_im, s5_c_re=s5_c_re, s5_c_im=s5_c_im,
                  s5_d=s5_d, s5_glu_w=s5_glu_w, s5_glu_b=s5_glu_b, w_branch_a=w_branch_a, w_branch_r=w_branch_r,
                  w_branch_s=w_branch_s, w_out=w_out, router_w=router_w, router_bias=router_bias,
                  exp_w_gate=exp_w_gate, exp_w_up=exp_w_up, exp_w_down=exp_w_down, sh_w_gate=sh_w_gate,
                  sh_w_up=sh_w_up, sh_w_down=sh_w_down)
    batch, t_len, d_model = x.shape
    c_len = ctx.shape[1]
    depth = ada_w.shape[0]
    assert batch <= MOD_CTX_ROW
    tabs = _rope_tables(t_len)
    cvec = jnp.zeros((2 * SUBLANES, d_model), F32).at[:batch].set(c.astype(F32)).at[MOD_CTX_ROW].set(c_ctx.astype(F32))
    x2 = x.astype(F32).reshape(batch * t_len, d_model)
    ctx2 = ctx.astype(F32).reshape(batch * c_len, d_model)
    for i in range(depth):
        lw = _prepare_layer(i, params)
        need_ctx = i < depth - 1
        lam_init = 0.8 - 0.6 * math.exp(-0.3 * i)
        mod = _ada_table(cvec, ada_w[i], ada_b[i])
        h = _modulate(x2, mod, 0, 1, t_len)
        hc = _modulate(ctx2, mod, 0, 1, None)
        m, m_c = _token_mixer(h, hc, lw, lam_init, tabs, need_ctx, batch, t_len, c_len)
        x2 = _out_proj_residual(m, lw["w_out"], x2, mod, 2, t_len)
        streams = [(x2, t_len)]
        if need_ctx:
            ctx2 = _out_proj_residual(m_c, lw["w_out"], ctx2, mod, 2, None)
            streams.append((ctx2, None))
        outs = []
        for xs, rpb in streams:
            h2, comb = _modulate_route(xs, mod, 3, 4, rpb, lw["router_w_t"], lw["router_bias"])
            f_r = _moe_routed(h2, lw["exp_w_gate"], lw["exp_w_up"], lw["exp_w_down"], comb)
            outs.append(_moe_shared_residual(h2, lw["sh_w_gate"], lw["sh_w_up"], lw["sh_w_down"], f_r,
                                             xs, mod, 5, rpb))
        x2 = outs[0]
        if need_ctx:
            ctx2 = outs[1]
    return x2.reshape(batch, t_len, d_model).astype(x.dtype)
```

```python
import functools
import math

import jax
import jax.numpy as jnp
from jax import lax
from jax.experimental import pallas as pl
from jax.experimental.pallas import tpu as pltpu

F32 = jnp.float32
BF16 = jnp.bfloat16

GRID_W = 64
NORM_EPS = 1e-6
DA_HEADS = 16
DA_HEAD_DIM = 64
DA_V_DIM = 2 * DA_HEAD_DIM
DA_WIDTH = DA_HEADS * DA_V_DIM
DA_SCALE = DA_HEAD_DIM ** -0.5
DA_Q_SCALE = DA_SCALE * math.log2(math.e)
ROPE_BASE = 10000.0
RW_HEADS = 16
RW_HEAD_DIM = 64
RW_WIDTH = RW_HEADS * RW_HEAD_DIM
RW_DECAY_RANK = 64
RW_A_RANK = 64
RW_GATE_RANK = 128
RW_GN_EPS = 64e-5
RW_STATE_COLS = 2 * RW_WIDTH + 2 * RW_DECAY_RANK + 2 * RW_A_RANK
RW_OUT_COLS = RW_WIDTH + RW_GATE_RANK
S5_GROUP = 16
S5_GROUPS = 64
S5_WIDTH = S5_GROUPS * S5_GROUP
S5_STATE = 64
S5_LANES = S5_GROUPS * S5_STATE
N_EXPERTS = 64
TOP_K = 8
N_GROUPS = 8
TOPK_GROUPS = 4
EXPERT_FF = 256
ROUTED_SCALE = 2.5

LANES = 128
SUBLANES = 8
VMEM_LIMIT_BYTES = 56 * 1024 * 1024

RW_CHUNK = 64
RW_GROUP_HEADS = 4
S5_CHUNK = 64
MOD_CTX_ROW = 8
MOE_TILE = 256


def _cparams(sem):
    return pltpu.CompilerParams(dimension_semantics=sem, vmem_limit_bytes=VMEM_LIMIT_BYTES)


def _tile(n, pref, mult=SUBLANES):
    if n <= pref:
        return n
    t = (pref // mult) * mult
    while t > mult and n % t:
        t -= mult
    assert n % t == 0, (n, pref)
    return t


def _dot(a, b):
    return jnp.dot(a.astype(BF16), b.astype(BF16), preferred_element_type=F32)


def _dot_nt(a, b):
    return lax.dot_general(a.astype(BF16), b.astype(BF16), (((1,), (1,)), ((), ())),
                           preferred_element_type=F32)


def _dot_tn(a, b):
    return lax.dot_general(a.astype(BF16), b.astype(BF16), (((0,), (0,)), ((), ())),
                           preferred_element_type=F32)


def _hilo(x):
    hi = x.astype(BF16)
    lo = (x - hi.astype(F32)).astype(BF16)
    return hi, lo


def _dot_hp_lhs(a, b_exact):
    hi, lo = _hilo(a)
    return (jnp.dot(hi, b_exact, preferred_element_type=F32)
            + jnp.dot(lo, b_exact, preferred_element_type=F32))


def _dot_hp(a, b):
    ah, al = _hilo(a)
    bh, bl = _hilo(b)
    return (jnp.dot(ah, bh, preferred_element_type=F32) + jnp.dot(al, bh, preferred_element_type=F32)
            + jnp.dot(ah, bl, preferred_element_type=F32))


def _sigmoid(x):
    return 1.0 / (1.0 + jnp.exp(-x))


def _softplus(x):
    return jnp.maximum(x, 0.0) + jnp.log(1.0 + jnp.exp(-jnp.abs(x)))


def _gelu_tanh(x):
    c = math.sqrt(2.0 / math.pi)
    return 0.5 * x * (1.0 + jnp.tanh(c * (x + 0.044715 * (x * x * x))))


def _mm_body(a_ref, b_ref, *rest, n_extra, prologue, epilogue):
    extras = rest[:n_extra]
    o_ref = rest[n_extra]
    a = a_ref[...]
    if prologue is not None:
        a = prologue(a)
    acc = jnp.dot(a.astype(BF16), b_ref[...].astype(BF16), preferred_element_type=F32)
    if epilogue is not None:
        acc = epilogue(acc, *[e[...] for e in extras])
    o_ref[...] = acc.astype(o_ref.dtype)


def _matmul(a, b, out_dtype, *, tm=512, tn=1024, prologue=None, epilogue=None, extras=(),
            a_spec=None, out_spec=None, out_shape=None, grid_m=None, name="matmul"):
    k, n = b.shape
    tn = _tile(n, tn, LANES)
    if a_spec is None:
        m = a.shape[0]
        tm = _tile(m, tm)
        grid_m = m // tm
        a_spec = pl.BlockSpec((tm, k), lambda i, j: (i, 0))
    if out_spec is None:
        out_spec = pl.BlockSpec((tm, tn), lambda i, j: (i, j))
        out_shape = (a.shape[0], n)
    body = functools.partial(_mm_body, n_extra=len(extras), prologue=prologue, epilogue=epilogue)
    return pl.pallas_call(
        body,
        grid=(grid_m, n // tn),
        in_specs=[a_spec, pl.BlockSpec((k, tn), lambda i, j: (0, j))] + [s for _, s in extras],
        out_specs=out_spec,
        out_shape=jax.ShapeDtypeStruct(out_shape, out_dtype),
        compiler_params=_cparams(("parallel", "arbitrary")),
        name=name,
    )(a, b, *[x for x, _ in extras])


def _ada_table(cvec, ada_w_l, ada_b_l):
    d6 = ada_w_l.shape[1]

    def prologue(a):
        return a * _sigmoid(a)

    def epilogue(acc, bias):
        return acc + bias

    out = _matmul(cvec, ada_w_l, F32, tm=16, tn=512, prologue=prologue, epilogue=epilogue,
                  extras=[(ada_b_l.reshape(1, d6), pl.BlockSpec((1, 512), lambda i, j: (0, j)))],
                  name="ada_table")
    return out.reshape(cvec.shape[0], 1, d6)


def _mod_row_fn(rows_per_batch, tm):
    if rows_per_batch is None:
        return lambda i: MOD_CTX_ROW
    nb = rows_per_batch // tm
    return lambda i: i // nb


def _modulate_body(x_ref, sh_ref, sc_ref, o_ref):
    x = x_ref[...]
    ms = jnp.mean(x * x, axis=-1, keepdims=True)
    h = x * lax.rsqrt(ms + NORM_EPS) * (1.0 + sc_ref[0]) + sh_ref[0]
    o_ref[...] = h.astype(o_ref.dtype)


def _modulate(x2, mod, shift_k, scale_k, rows_per_batch):
    n, d = x2.shape
    tm = _tile(n if rows_per_batch is None else rows_per_batch, 256)
    row = _mod_row_fn(rows_per_batch, tm)
    return pl.pallas_call(
        _modulate_body,
        grid=(n // tm,),
        in_specs=[pl.BlockSpec((tm, d), lambda i: (i, 0)),
                  pl.BlockSpec((1, 1, d), lambda i: (row(i), 0, shift_k)),
                  pl.BlockSpec((1, 1, d), lambda i: (row(i), 0, scale_k))],
        out_specs=pl.BlockSpec((tm, d), lambda i: (i, 0)),
        out_shape=jax.ShapeDtypeStruct((n, d), BF16),
        compiler_params=_cparams(("parallel",)),
        name="modulate",
    )(x2, mod, mod)


def _pack_halves(x):
    w = x.shape[1] // 2
    lo = lax.bitcast_convert_type(x[:, :w].astype(BF16).astype(F32), jnp.uint32)
    hi = lax.bitcast_convert_type(x[:, w:].astype(BF16).astype(F32), jnp.uint32)
    return lax.shift_right_logical(lo, jnp.uint32(16)) | hi


def _unpack_halves(p):
    lo = lax.bitcast_convert_type(lax.shift_left(p, jnp.uint32(16)), F32)
    hi = lax.bitcast_convert_type(p & jnp.uint32(0xFFFF0000), F32)
    return lo, hi


def _route_body(x_ref, sh_ref, sc_ref, wr_ref, bias_ref, h_ref, hp_ref, comb_ref, cnt_ref):
    x = x_ref[...]
    ms = jnp.mean(x * x, axis=-1, keepdims=True)
    h = x * lax.rsqrt(ms + NORM_EPS) * (1.0 + sc_ref[0]) + sh_ref[0]
    h_ref[...] = h.astype(h_ref.dtype)
    hp_ref[...] = _pack_halves(h)
    tm = x.shape[0]
    wr = wr_ref[...]
    hh, hl = _hilo(h)
    wh, wl = _hilo(wr)
    logits = _dot_nt(wh, hh) + _dot_nt(wl, hh) + _dot_nt(wh, hl)
    scores = _sigmoid(logits[:N_EXPERTS])
    per_group = N_EXPERTS // N_GROUPS
    sc3 = scores.reshape(N_GROUPS, per_group, tm)
    sel = sc3 + bias_ref[...]
    midx = lax.broadcasted_iota(jnp.int32, sel.shape, 1)
    neg = jnp.float32(-jnp.inf)
    m1 = jnp.max(sel, axis=1, keepdims=True)
    first = jnp.min(jnp.where(sel == m1, midx, per_group), axis=1, keepdims=True)
    m2 = jnp.max(jnp.where(midx == first, neg, sel), axis=1, keepdims=True)
    gs = (m1 + m2).reshape(N_GROUPS, tm)
    gidx = lax.broadcasted_iota(jnp.int32, gs.shape, 0)
    gmask = jnp.zeros(gs.shape, jnp.bool_)
    for _ in range(TOPK_GROUPS):
        m = jnp.max(gs, axis=0, keepdims=True)
        f = jnp.min(jnp.where(gs == m, gidx, N_GROUPS), axis=0, keepdims=True)
        pick = gidx == f
        gmask = jnp.logical_or(gmask, pick)
        gs = jnp.where(pick, neg, gs)
    val = jnp.where(gmask.reshape(N_GROUPS, 1, tm), sel, neg)
    eidx = lax.broadcasted_iota(jnp.int32, sel.shape, 0) * per_group + midx
    chosen = jnp.zeros(sel.shape, jnp.bool_)
    for _ in range(TOP_K):
        m = jnp.max(jnp.max(val, axis=1, keepdims=True), axis=0, keepdims=True)
        f = jnp.min(jnp.min(jnp.where(val == m, eidx, N_EXPERTS), axis=1, keepdims=True), axis=0, keepdims=True)
        pick = eidx == f
        chosen = jnp.logical_or(chosen, pick)
        val = jnp.where(pick, neg, val)
    w = jnp.where(chosen, sc3, 0.0)
    wsum = jnp.sum(jnp.sum(w, axis=1, keepdims=True), axis=0, keepdims=True)
    comb = (w / wsum * ROUTED_SCALE).reshape(N_EXPERTS, tm)
    comb_ref[...] = comb

    @pl.when(pl.program_id(0) == 0)
    def _():
        cnt_ref[...] = jnp.zeros_like(cnt_ref)

    cnt_ref[...] += jnp.sum((comb > 0.0).astype(F32), axis=1, keepdims=True)


def _modulate_route(x2, mod, shift_k, scale_k, rows_per_batch, router_w_t, router_bias):
    n, d = x2.shape
    tm = _tile(n if rows_per_batch is None else rows_per_batch, 256, LANES)
    row = _mod_row_fn(rows_per_batch, tm)
    return pl.pallas_call(
        _route_body,
        grid=(n // tm,),
        in_specs=[pl.BlockSpec((tm, d), lambda i: (i, 0)),
                  pl.BlockSpec((1, 1, d), lambda i: (row(i), 0, shift_k)),
                  pl.BlockSpec((1, 1, d), lambda i: (row(i), 0, scale_k)),
                  pl.BlockSpec((LANES, d), lambda i: (0, 0)),
                  pl.BlockSpec((N_GROUPS, N_EXPERTS // N_GROUPS, 1), lambda i: (0, 0, 0))],
        out_specs=[pl.BlockSpec((tm, d), lambda i: (i, 0)),
                   pl.BlockSpec((tm, d // 2), lambda i: (i, 0)),
                   pl.BlockSpec((N_EXPERTS, tm), lambda i: (0, i)),
                   pl.BlockSpec((N_EXPERTS, LANES), lambda i: (0, 0))],
        out_shape=[jax.ShapeDtypeStruct((n, d), BF16), jax.ShapeDtypeStruct((n, d // 2), jnp.uint32),
                   jax.ShapeDtypeStruct((N_EXPERTS, n), F32), jax.ShapeDtypeStruct((N_EXPERTS, LANES), F32)],
        compiler_params=_cparams(("arbitrary",)),
        name="modulate_route",
    )(x2, mod, mod, router_w_t, router_bias.reshape(N_GROUPS, N_EXPERTS // N_GROUPS, 1))


def _group_ones(width, group):
    r = lax.broadcasted_iota(jnp.int32, (width, width), 0) // group
    c = lax.broadcasted_iota(jnp.int32, (width, width), 1) // group
    return (r == c).astype(BF16)


def _qk_norm_rope(x, gain, cos, sin, scale):
    ss = _dot_hp_lhs(x * x, _group_ones(LANES, DA_HEAD_DIM))
    xn = x * lax.rsqrt(ss * (1.0 / DA_HEAD_DIM) + NORM_EPS) * gain
    lane = lax.broadcasted_iota(jnp.int32, x.shape, 1)
    quarter = DA_HEAD_DIM // 4
    partner = jnp.where((lane % (2 * quarter)) < quarter,
                        pltpu.roll(xn, LANES - quarter, axis=1),
                        pltpu.roll(xn, quarter, axis=1))
    return (xn * cos + partner * sin) * scale


def _proj_qk(h, w, gain, cos, sin, scale, t_len, name):
    n = h.shape[0]
    tm = _tile(n if t_len is None else t_len, 512)
    nb = 1 if t_len is None else t_len // tm
    tab = pl.BlockSpec((tm, LANES), lambda i, j: (i % nb, 0))
    gain2 = jnp.tile(gain.reshape(1, DA_HEAD_DIM), (1, 2))

    def epilogue(acc, g, c, s):
        heads = [_qk_norm_rope(acc[:, k * LANES:(k + 1) * LANES], g, c, s, scale)
                 for k in range(acc.shape[1] // LANES)]
        return jnp.concatenate(heads, axis=1)

    return _matmul(h, w, BF16, tm=tm, epilogue=epilogue,
                   extras=[(gain2, pl.BlockSpec((1, LANES), lambda i, j: (0, 0))), (cos, tab), (sin, tab)],
                   name=name)


def _rope_tables(t_len):
    rows = t_len // GRID_W
    row = jnp.repeat(jnp.arange(rows, dtype=F32), GRID_W)
    col = jnp.tile(jnp.arange(GRID_W, dtype=F32), rows)
    half = DA_HEAD_DIM // 2
    inv_freq = 1.0 / (ROPE_BASE ** (jnp.arange(0, half, 2, dtype=F32) / half))
    ang_r = row[:, None] * inv_freq
    ang_c = col[:, None] * inv_freq
    cos64 = jnp.concatenate([jnp.cos(ang_r), jnp.cos(ang_r), jnp.cos(ang_c), jnp.cos(ang_c)], axis=1)
    sin64 = jnp.concatenate([-jnp.sin(ang_r), jnp.sin(ang_r), -jnp.sin(ang_c), jnp.sin(ang_c)], axis=1)
    return jnp.tile(cos64, (1, 2)), jnp.tile(sin64, (1, 2))


def _attn_body(*refs, has_lat, lam_init):
    if has_lat:
        q_ref, kl_ref, vl_ref, kc_ref, vc_ref, lq_ref, lk_ref, sub_ref, o_ref = refs
    else:
        q_ref, kc_ref, vc_ref, lq_ref, lk_ref, sub_ref, o_ref = refs
    lqk = lq_ref[...] * lk_ref[...]
    lsum = jnp.sum(lqk, axis=1, keepdims=True)
    e = jnp.exp(lsum)
    lam = e[0:1, :] - e[1:2, :] + lam_init
    q = q_ref[...]
    lane = lax.broadcasted_iota(jnp.int32, q.shape, 1)
    zero = jnp.zeros_like(q)
    exps, dens = [], []
    for m in range(2):
        in_map = (lane // DA_HEAD_DIM) == m
        qm = jnp.where(in_map, q, zero)
        s_c = _dot_nt(qm, kc_ref[...])
        mx = jnp.max(s_c, axis=-1, keepdims=True)
        if has_lat:
            s_l = _dot_nt(qm, kl_ref[...])
            mx = jnp.maximum(mx, jnp.max(s_l, axis=-1, keepdims=True))
        e_c = jnp.exp2(s_c - mx)
        den = jnp.sum(e_c, axis=-1, keepdims=True)
        e_l = None
        if has_lat:
            e_l = jnp.exp2(s_l - mx)
            den = den + jnp.sum(e_l, axis=-1, keepdims=True)
        exps.append((e_l, e_c))
        dens.append(den)
    w0 = 1.0 / dens[0]
    w1 = lam / dens[1]
    o = _dot(exps[0][1] * w0 - exps[1][1] * w1, vc_ref[...])
    if has_lat:
        o = o + _dot(exps[0][0] * w0 - exps[1][0] * w1, vl_ref[...])
    ms = jnp.mean(o * o, axis=-1, keepdims=True)
    o = o * lax.rsqrt(ms + NORM_EPS) * sub_ref[...] * (1.0 - lam_init)
    o_ref[...] = o.astype(o_ref.dtype)


def _diff_attention(q, k_lat, v_lat, k_ctx, v_ctx, lq, lk, subln, lam_init, batch):
    n, w = q.shape
    tq_len = n // batch
    c_len = k_ctx.shape[0] // batch
    tq = _tile(tq_len, 256)
    nq = tq_len // tq
    has_lat = k_lat is not None
    blk = lambda rows: pl.BlockSpec((rows, LANES), lambda b, h, i: (b, h))
    in_specs = [pl.BlockSpec((tq, LANES), lambda b, h, i: (b * nq + i, h))]
    args = [q]
    if has_lat:
        t_len = k_lat.shape[0] // batch
        in_specs += [blk(t_len), blk(t_len)]
        args += [k_lat, v_lat]
    in_specs += [blk(c_len), blk(c_len),
                 pl.BlockSpec((2, DA_HEAD_DIM), lambda b, h, i: (0, 0)),
                 pl.BlockSpec((2, DA_HEAD_DIM), lambda b, h, i: (0, 0)),
                 pl.BlockSpec((1, LANES), lambda b, h, i: (0, 0))]
    args += [k_ctx, v_ctx, lq, lk, subln.reshape(1, DA_V_DIM)]
    return pl.pallas_call(
        functools.partial(_attn_body, has_lat=has_lat, lam_init=lam_init),
        grid=(batch, w // LANES, nq),
        in_specs=in_specs,
        out_specs=pl.BlockSpec((tq, LANES), lambda b, h, i: (b * nq + i, h)),
        out_shape=jax.ShapeDtypeStruct((n, w), BF16),
        compiler_params=_cparams(("parallel", "parallel", "arbitrary")),
        name="diff_attention",
    )(*args)


def _rwkv_prep_body(x_ref, xp_ref, xn_ref, conv_ref, lora_ref, w0a0_ref, kk_w_ref, ka_w_ref,
                    lwf_ref, lwb_ref, kk_ref, kkaf_ref, kkab_ref, kdf_ref, kdb_ref, v_ref, r_ref, gd_ref,
                    *, blocks_per_seq):
    i = pl.program_id(0)
    x = x_ref[...]
    tm = x.shape[0]
    first = (i % blocks_per_seq) == 0
    last = (i % blocks_per_seq) == blocks_per_seq - 1
    xp = jnp.where(first, 0.0, xp_ref[SUBLANES - 1:SUBLANES, :])
    xn = jnp.where(last, 0.0, xn_ref[0:1, :])
    row = lax.broadcasted_iota(jnp.int32, (tm, 1), 0)
    up = jnp.where(row == 0, xp, pltpu.roll(x, 1, axis=0))
    dn = jnp.where(row == tm - 1, xn, pltpu.roll(x, tm - 1, axis=0))
    cw = conv_ref[...]
    cv = up * cw[0:1, :] + x * cw[1:2, :] + dn * cw[2:3, :]
    w = RW_WIDTH
    k = cv[:, :w]
    v = cv[:, w:2 * w]
    lora_in = cv[:, 2 * w:RW_STATE_COLS]
    r = cv[:, RW_STATE_COLS:RW_STATE_COLS + w]
    gd = cv[:, RW_STATE_COLS + w:]
    lane = lax.broadcasted_iota(jnp.int32, lora_in.shape, 1)
    li = jnp.where(lane < 2 * RW_DECAY_RANK, jnp.tanh(lora_in), lora_in)
    pre = _dot_hp(li, lora_ref[...]) + w0a0_ref[...]
    kkr = k * kk_w_ref[...]
    ss = _dot_hp_lhs(kkr * kkr, _group_ones(w, RW_HEAD_DIM))
    kk = kkr * lax.rsqrt(ss + 1e-12)
    kk_ref[...] = kk
    v_ref[...] = v
    r_ref[...] = r
    gd_ref[...] = gd
    ka = ka_w_ref[...]
    for d, (lw_ref, kka_ref, kd_ref) in enumerate(((lwf_ref, kkaf_ref, kdf_ref), (lwb_ref, kkab_ref, kdb_ref))):
        w_log = -_softplus(-pre[:, d * w:(d + 1) * w]) - 0.5
        lw_ref[...] = -jnp.exp(w_log)
        a = _sigmoid(pre[:, (2 + d) * w:(3 + d) * w])
        kka_ref[...] = kk * a
        kd_ref[...] = k * (1.0 + (a - 1.0) * ka)


def _rwkv_prep(p_rw, seq_len, conv, lora_w, w0a0, k_k, k_a):
    n, c = p_rw.shape
    tm = _tile(seq_len, 128)
    bps = seq_len // tm
    sub = tm // SUBLANES
    nsub = n // SUBLANES
    w = RW_WIDTH
    wide = lambda: pl.BlockSpec((tm, w), lambda i: (i, 0))
    outs = [jax.ShapeDtypeStruct((n, w), F32)] * 9 + [jax.ShapeDtypeStruct((n, RW_GATE_RANK), F32)]
    return pl.pallas_call(
        functools.partial(_rwkv_prep_body, blocks_per_seq=bps),
        grid=(n // tm,),
        in_specs=[pl.BlockSpec((tm, c), lambda i: (i, 0)),
                  pl.BlockSpec((SUBLANES, c), lambda i: (jnp.maximum(i * sub - 1, 0), 0)),
                  pl.BlockSpec((SUBLANES, c), lambda i: (jnp.minimum((i + 1) * sub, nsub - 1), 0)),
                  pl.BlockSpec((3, c), lambda i: (0, 0)),
                  pl.BlockSpec(lora_w.shape, lambda i: (0, 0)),
                  pl.BlockSpec((1, 4 * w), lambda i: (0, 0)),
                  pl.BlockSpec((1, w), lambda i: (0, 0)),
                  pl.BlockSpec((1, w), lambda i: (0, 0))],
        out_specs=[wide() for _ in range(9)] + [pl.BlockSpec((tm, RW_GATE_RANK), lambda i: (i, 0))],
        out_shape=outs,
        compiler_params=_cparams(("parallel",)),
        name="rwkv_prep",
    )(p_rw, p_rw, p_rw, conv, lora_w, w0a0, k_k.reshape(1, w), k_a.reshape(1, w))


def _rwkv_scan_body(lwf_ref, kkaf_ref, kdf_ref, kkf_ref, vf_ref, rf_ref,
                    lwb_ref, kkab_ref, kdb_ref, kkb_ref, vb_ref, rb_ref, s0f_ref, s0b_ref,
                    of_ref, ob_ref, stf_ref, stb_ref, *, want_out):
    ci = pl.program_id(1)

    @pl.when(ci == 0)
    def _():
        stf_ref[...] = s0f_ref[...]
        stb_ref[...] = s0b_ref[...]

    n_l, width = lwf_ref.shape
    hd = RW_HEAD_DIM
    gw = RW_GROUP_HEADS * hd
    n_groups = width // gw
    row = lax.broadcasted_iota(jnp.int32, (n_l, n_l), 0)
    col = lax.broadcasted_iota(jnp.int32, (n_l, n_l), 1)
    trow = lax.broadcasted_iota(jnp.int32, (n_l, gw), 0)
    tcol = lax.broadcasted_iota(jnp.int32, (n_l, gw), 1) % hd
    eye = (tcol == trow).astype(F32)
    same_head = ((lax.broadcasted_iota(jnp.int32, (gw, gw), 0) // hd)
                 == (lax.broadcasted_iota(jnp.int32, (gw, gw), 1) // hd))
    reps = gw // n_l

    def bdiag(x):
        xb = x.astype(BF16)
        return jnp.where(same_head, jnp.concatenate([xb] * reps, axis=0), jnp.zeros((), BF16))

    units = []
    for reverse, refs in ((False, (lwf_ref, kkaf_ref, kdf_ref, kkf_ref, vf_ref, rf_ref, stf_ref, of_ref)),
                          (True, (lwb_ref, kkab_ref, kdb_ref, kkb_ref, vb_ref, rb_ref, stb_ref, ob_ref))):
        lw_ref, kka_ref, k_ref, kk_ref, v_ref, r_ref, st_ref, o_ref = refs
        lw = lw_ref[...]
        tri = ((col >= row) if reverse else (col <= row)).astype(BF16)
        lh, ll = _hilo(lw)
        c = jnp.dot(tri, lh, preferred_element_type=F32) + jnp.dot(tri, ll, preferred_element_type=F32)
        g_end = jnp.exp(c[0:1, :] if reverse else c[n_l - 1:n_l, :])
        e_inv = jnp.exp(-c)
        a_t = kk_ref[...] * jnp.exp(c - lw)
        r_t = r_ref[...] * jnp.exp(c)
        k_h = k_ref[...] * e_inv
        b_h = kka_ref[...] * e_inv
        v_all = v_ref[...]
        incl, strict = (tcol >= trow, tcol > trow) if reverse else (tcol <= trow, tcol < trow)
        for g in range(n_groups):
            s = slice(g * gw, (g + 1) * gw)
            units.append(dict(g=g, sl=s, incl=incl, strict=strict, st_ref=st_ref, o_ref=o_ref, state=st_ref[g],
                              ar=jnp.concatenate([a_t[:, s], r_t[:, s]], axis=0).astype(BF16),
                              k_h=k_h[:, s], b_h=b_h[:, s], v=v_all[:, s], g_end=g_end[:, s]))

    n_iter = max(1, (n_l - 1).bit_length()) - 1
    sc_k = [_dot_nt(x["ar"], bdiag(x["k_h"])) for x in units]
    sc_b = [_dot_nt(x["ar"], bdiag(x["b_h"])) for x in units]
    from_s = [_dot_nt(x["ar"], x["state"]) for x in units]
    p = [-jnp.where(x["strict"], sb[:n_l], 0.0) for x, sb in zip(units, sc_b)]
    t_inv = [eye + pi for pi in p]
    if n_iter:
        p = [_dot(pi, bdiag(pi)) for pi in p]
    for it in range(n_iter):
        if it + 1 < n_iter:
            y = [_dot(jnp.concatenate([ti, pi], axis=0), bdiag(pi)) for ti, pi in zip(t_inv, p)]
            t_inv = [ti + yi[:n_l] for ti, yi in zip(t_inv, y)]
            p = [yi[n_l:] for yi in y]
        else:
            t_inv = [ti + _dot(ti, bdiag(pi)) for ti, pi in zip(t_inv, p)]
    v_bd = [bdiag(x["v"]) for x in units]
    w = [fs[:n_l] + _dot(jnp.where(x["strict"], sk[:n_l], 0.0), vb)
         for x, fs, sk, vb in zip(units, from_s, sc_k, v_bd)]
    u = [_dot(ti, bdiag(wi)) for ti, wi in zip(t_inv, w)]
    if want_out:
        o_v = [_dot(jnp.where(x["incl"], sk[n_l:], 0.0), vb) for x, sk, vb in zip(units, sc_k, v_bd)]
        o_u = [_dot(jnp.where(x["incl"], sb[n_l:], 0.0), bdiag(ui)) for x, sb, ui in zip(units, sc_b, u)]
        for x, fs, ov, ou in zip(units, from_s, o_v, o_u):
            x["o_ref"][:, x["sl"]] = fs[n_l:] + ov - ou
    else:
        of_ref[...] = jnp.zeros_like(of_ref)
        ob_ref[...] = jnp.zeros_like(ob_ref)
    upd_k = [_dot_tn(x["v"], x["k_h"] * x["g_end"]) for x in units]
    upd_b = [_dot_tn(ui, x["b_h"] * x["g_end"]) for x, ui in zip(units, u)]
    for x, uk, ub in zip(units, upd_k, upd_b):
        x["st_ref"][x["g"]] = jnp.where(same_head, x["state"] * x["g_end"] + uk - ub, 0.0)


def _rwkv_scan(t, s0_f, s0_b, batch, *, want_out=True):
    n, w = t["kk"].shape
    seq = n // batch
    n_l = _tile(seq, RW_CHUNK)
    nch = seq // n_l
    gw = RW_GROUP_HEADS * RW_HEAD_DIM
    assert gw % n_l == 0 and w % gw == 0
    n_groups = w // gw
    fwd = pl.BlockSpec((n_l, w), lambda b, c: (b * nch + c, 0))
    rev = pl.BlockSpec((n_l, w), lambda b, c: (b * nch + nch - 1 - c, 0))
    st_spec = pl.BlockSpec((n_groups, gw, gw), lambda b, c: (b, 0, 0))
    seq_shape = jax.ShapeDtypeStruct((n, w), F32)
    st_shape = jax.ShapeDtypeStruct(s0_f.shape, F32)
    return pl.pallas_call(
        functools.partial(_rwkv_scan_body, want_out=want_out),
        grid=(batch, nch),
        in_specs=[fwd] * 6 + [rev] * 6 + [st_spec, st_spec],
        out_specs=[fwd, rev, st_spec, st_spec],
        out_shape=[seq_shape, seq_shape, st_shape, st_shape],
        compiler_params=_cparams(("parallel", "arbitrary")),
        name="rwkv_scan",
    )(t["lw_f"], t["kka_f"], t["kd_f"], t["kk"], t["v"], t["r"],
      t["lw_b"], t["kka_b"], t["kd_b"], t["kk"], t["v"], t["r"], s0_f, s0_b)


def _rwkv_readout_body(of_ref, ob_ref, r_ref, kdf_ref, kdb_ref, v_ref, gd_ref, gnw_ref, gnb_ref, rk_ref, g2_ref, y_ref):
    ones = _group_ones(RW_WIDTH, RW_HEAD_DIM)
    inv = 1.0 / RW_HEAD_DIM
    o = of_ref[...] + ob_ref[...]
    mu = _dot_hp_lhs(o, ones) * inv
    d = o - mu
    var = _dot_hp_lhs(d * d, ones) * inv
    on = d * lax.rsqrt(var + RW_GN_EPS) * gnw_ref[...] + gnb_ref[...]
    r = r_ref[...]
    rk = rk_ref[...]
    bonus = _dot_hp_lhs(r * kdf_ref[...] * rk, ones) + _dot_hp_lhs(r * kdb_ref[...] * rk, ones)
    y = on + bonus * v_ref[...]
    g = _dot(_sigmoid(gd_ref[...]), g2_ref[...])
    y_ref[...] = (y * g).astype(y_ref.dtype)


def _rwkv_readout(o_f, o_b, r, kd_f, kd_b, v, gd, gn_w, gn_b, r_k, g2):
    n, w = o_f.shape
    tm = _tile(n, 256)
    wide = pl.BlockSpec((tm, w), lambda i: (i, 0))
    vec = pl.BlockSpec((1, w), lambda i: (0, 0))
    return pl.pallas_call(
        _rwkv_readout_body,
        grid=(n // tm,),
        in_specs=[wide] * 6 + [pl.BlockSpec((tm, RW_GATE_RANK), lambda i: (i, 0)), vec, vec, vec,
                               pl.BlockSpec((RW_GATE_RANK, w), lambda i: (0, 0))],
        out_specs=wide,
        out_shape=jax.ShapeDtypeStruct((n, w), BF16),
        compiler_params=_cparams(("parallel",)),
        name="rwkv_readout",
    )(o_f, o_b, r, kd_f, kd_b, v, gd, gn_w.reshape(1, w), gn_b.reshape(1, w), r_k.reshape(1, w), g2)


def _s5_scan_body(u_ref, bre_ref, bim_ref, are_ref, aim_ref, cre_ref, cim_ref, h0_ref, y_ref, ht_ref,
                  dre, dim, *, reverse, want_out):
    ci = pl.program_id(0)

    @pl.when(ci == 0)
    def _():
        ht_ref[...] = h0_ref[...]

    tt, nb, wu = u_ref.shape
    nblk = wu // LANES
    sw = S5_LANES // nblk
    u2 = u_ref[...].reshape(tt * nb, wu).astype(BF16)
    for c in range(nblk):
        uc = u2[:, c * LANES:(c + 1) * LANES]
        dre[:, :, c * sw:(c + 1) * sw] = jnp.dot(uc, bre_ref[c], preferred_element_type=F32).reshape(tt, nb, sw)
        dim[:, :, c * sw:(c + 1) * sw] = jnp.dot(uc, bim_ref[c], preferred_element_type=F32).reshape(tt, nb, sw)
    lw = 1024
    for c in range(S5_LANES // lw):
        ls = slice(c * lw, (c + 1) * lw)
        ar = jnp.broadcast_to(are_ref[:, ls], (nb, lw))
        ai = jnp.broadcast_to(aim_ref[:, ls], (nb, lw))

        def step(s, carry, ls=ls, ar=ar, ai=ai):
            t = (tt - 1 - s) if reverse else s
            hr, hi = carry
            nr = ar * hr - ai * hi + dre[t, :, ls]
            ni = ar * hi + ai * hr + dim[t, :, ls]
            dre[t, :, ls] = nr
            dim[t, :, ls] = ni
            return nr, ni

        hr, hi = lax.fori_loop(0, tt, step, (ht_ref[0, :, ls], ht_ref[1, :, ls]), unroll=2)
        ht_ref[0, :, ls] = hr
        ht_ref[1, :, ls] = hi
    if want_out:
        xr = dre[...].reshape(tt * nb, S5_LANES).astype(BF16)
        xi = dim[...].reshape(tt * nb, S5_LANES).astype(BF16)
        for c in range(nblk):
            yc = (jnp.dot(xr[:, c * sw:(c + 1) * sw], cre_ref[c], preferred_element_type=F32)
                  - jnp.dot(xi[:, c * sw:(c + 1) * sw], cim_ref[c], preferred_element_type=F32))
            y_ref[:, :, c * LANES:(c + 1) * LANES] = yc.reshape(tt, nb, LANES)
    else:
        y_ref[...] = jnp.zeros_like(y_ref)


def _s5_scan(u_tm, h0, p, *, reverse, want_out=True):
    t_len, nb, wu = u_tm.shape
    tt = _tile(t_len, S5_CHUNK)
    nch = t_len // tt
    chunk = (lambda c: (nch - 1 - c, 0, 0)) if reverse else (lambda c: (c, 0, 0))
    const3 = lambda a: pl.BlockSpec(a.shape, lambda c: (0, 0, 0))
    const2 = lambda a: pl.BlockSpec(a.shape, lambda c: (0, 0))
    return pl.pallas_call(
        functools.partial(_s5_scan_body, reverse=reverse, want_out=want_out),
        grid=(nch,),
        in_specs=[pl.BlockSpec((tt, nb, wu), chunk), const3(p["b_re"]), const3(p["b_im"]),
                  const2(p["a_re"]), const2(p["a_im"]), const3(p["c_re"]), const3(p["c_im"]), const3(h0)],
        out_specs=[pl.BlockSpec((tt, nb, wu), chunk), const3(h0)],
        out_shape=[jax.ShapeDtypeStruct(u_tm.shape, F32), jax.ShapeDtypeStruct(h0.shape, F32)],
        scratch_shapes=[pltpu.VMEM((tt, nb, S5_LANES), F32), pltpu.VMEM((tt, nb, S5_LANES), F32)],
        compiler_params=_cparams(("arbitrary",)),
        name="s5_scan_rev" if reverse else "s5_scan_fwd",
    )(u_tm, p["b_re"], p["b_im"], p["a_re"], p["a_im"], p["c_re"], p["c_im"], h0)


def _s5_dir_params(lam_re, lam_im, log_dt, b_re, b_im, c_re, c_im):
    g, pdim = lam_re.shape
    dt = jnp.exp(log_dt.astype(F32))[:, None]
    mag = jnp.exp(lam_re * dt)
    abar_re = mag * jnp.cos(lam_im * dt)
    abar_im = mag * jnp.sin(lam_im * dt)
    den = lam_re * lam_re + lam_im * lam_im
    nr = abar_re - 1.0
    g_re = (nr * lam_re + abar_im * lam_im) / den
    g_im = (abar_im * lam_re - nr * lam_im) / den
    bb_re = g_re[:, :, None] * b_re - g_im[:, :, None] * b_im
    bb_im = g_re[:, :, None] * b_im + g_im[:, :, None] * b_re
    gpb = LANES // S5_GROUP
    nblk = g // gpb
    eye = jnp.eye(gpb, dtype=F32)

    def drive_mat(bb):
        x = bb.reshape(nblk, gpb, pdim, S5_GROUP)
        x = jnp.einsum("cgph,gk->cghkp", x, eye)
        return x.reshape(nblk, gpb * S5_GROUP, gpb * pdim).astype(BF16)

    def read_mat(cc):
        x = cc.reshape(nblk, gpb, S5_GROUP, pdim)
        x = jnp.einsum("cghp,gk->cgpkh", x, eye)
        return x.reshape(nblk, gpb * pdim, gpb * S5_GROUP).astype(BF16)

    return {"a_re": abar_re.reshape(1, g * pdim), "a_im": abar_im.reshape(1, g * pdim),
            "b_re": drive_mat(bb_re), "b_im": drive_mat(bb_im),
            "c_re": read_mat(c_re.astype(F32)), "c_im": read_mat(c_im.astype(F32))}


def _s5_glu_body(u_ref, yf_ref, yb_ref, d_ref, w_ref, b_ref, o_ref):
    y = u_ref[...] * d_ref[...] + yf_ref[...] + yb_ref[...]
    y = _gelu_tanh(y)
    z = _dot(y, w_ref[...]) + b_ref[...]
    o_ref[...] = (y * _sigmoid(z)).astype(o_ref.dtype)


def _s5_glu(u, y_f, y_b, d_skip, glu_w, glu_b):
    n, w = u.shape
    tm = _tile(n, 512)
    wide = pl.BlockSpec((tm, w), lambda i: (i, 0))
    vec = pl.BlockSpec((1, w), lambda i: (0, 0))
    return pl.pallas_call(
        _s5_glu_body,
        grid=(n // tm,),
        in_specs=[wide, wide, wide, vec, pl.BlockSpec((w, w), lambda i: (0, 0)), vec],
        out_specs=wide,
        out_shape=jax.ShapeDtypeStruct((n, w), BF16),
        compiler_params=_cparams(("parallel",)),
        name="s5_glu",
    )(u, y_f, y_b, d_skip.reshape(1, w), glu_w, glu_b.reshape(1, w))


def _merge_body(ya_ref, yr_ref, ys_ref, wa_ref, wr_ref, ws_ref, ga_ref, gr_ref, gs_ref, o_ref):
    m = (ga_ref[...].astype(F32) * jnp.dot(ya_ref[...], wa_ref[...], preferred_element_type=F32)
         + gr_ref[...].astype(F32) * jnp.dot(yr_ref[...], wr_ref[...], preferred_element_type=F32)
         + gs_ref[...].astype(F32) * jnp.dot(ys_ref[...], ws_ref[...], preferred_element_type=F32))
    o_ref[...] = m.astype(o_ref.dtype)


def _merge(y_a, y_r, y_s_tm, gates, w_a, w_r, w_s, seq_len, batch):
    n = y_a.shape[0]
    d = w_a.shape[1]
    tm = _tile(seq_len, 512)
    tn = _tile(d, 1024, LANES)
    nt = seq_len // tm
    nd = d // tn
    row = lambda i, j: (i, 0)
    return pl.pallas_call(
        _merge_body,
        grid=(n // tm, nd),
        in_specs=[pl.BlockSpec((tm, y_a.shape[1]), row),
                  pl.BlockSpec((tm, y_r.shape[1]), row),
                  pl.BlockSpec((tm, S5_WIDTH), lambda i, j: (i % nt, i // nt)),
                  pl.BlockSpec((w_a.shape[0], tn), lambda i, j: (0, j)),
                  pl.BlockSpec((w_r.shape[0], tn), lambda i, j: (0, j)),
                  pl.BlockSpec((w_s.shape[0], tn), lambda i, j: (0, j)),
                  pl.BlockSpec((tm, tn), lambda i, j: (i, j)),
                  pl.BlockSpec((tm, tn), lambda i, j: (i, nd + j)),
                  pl.BlockSpec((tm, tn), lambda i, j: (i, 2 * nd + j))],
        out_specs=pl.BlockSpec((tm, tn), lambda i, j: (i, j)),
        out_shape=jax.ShapeDtypeStruct((n, d), BF16),
        compiler_params=_cparams(("parallel", "arbitrary")),
        name="merge",
    )(y_a, y_r, y_s_tm, w_a, w_r, w_s, gates, gates, gates)


def _moe_plan_body(comb_ref, cnt_ref, pos_ref, w_ref, te_ref, carry_ref, *, tile_rows, dummy_row):
    i = pl.program_id(0)
    ne, tm = comb_ref.shape
    tiles = jnp.floor((cnt_ref[...] + (tile_rows - 1)) * (1.0 / tile_rows))
    er = lax.broadcasted_iota(jnp.int32, (ne, ne), 0)
    ec = lax.broadcasted_iota(jnp.int32, (ne, ne), 1)
    t_hi, t_lo = _hilo(tiles)
    lower = (ec < er).astype(BF16)
    off_tiles = (jnp.dot(lower, t_hi, preferred_element_type=F32)
                 + jnp.dot(lower, t_lo, preferred_element_type=F32))

    @pl.when(i == 0)
    def _():
        carry_ref[...] = jnp.zeros_like(carry_ref)
        ntp = te_ref.shape[1]
        end_tiles = (off_tiles + tiles)[:, 0:1]
        tile_idx = lax.broadcasted_iota(jnp.int32, (1, ntp), 1).astype(F32)
        expert = jnp.sum((end_tiles <= tile_idx).astype(F32), axis=0, keepdims=True)
        expert = jnp.minimum(expert, ne - 1.0)
        valid = (tile_idx < jnp.max(end_tiles, axis=0, keepdims=True)).astype(F32)
        r8 = lax.broadcasted_iota(jnp.int32, te_ref.shape, 0)
        te_ref[...] = jnp.where(r8 == 0, expert, jnp.where(r8 == 1, valid, 0.0)).astype(jnp.int32)

    comb = comb_ref[...]
    chosen = comb > 0.0
    chf = chosen.astype(BF16)
    tr = lax.broadcasted_iota(jnp.int32, (tm, tm), 0)
    tc = lax.broadcasted_iota(jnp.int32, (tm, tm), 1)
    rank = jnp.dot(chf, (tr < tc).astype(BF16), preferred_element_type=F32)
    pos = off_tiles[:, 0:1] * tile_rows + carry_ref[:, 0:1] + rank
    carry_ref[...] += jnp.sum(chosen.astype(F32), axis=1, keepdims=True)
    eidx = lax.broadcasted_iota(jnp.int32, (ne, tm), 0)
    tok = lax.broadcasted_iota(jnp.int32, (1, tm), 1).astype(F32)
    remaining = chosen
    pos_rows, w_rows = [], []
    for k in range(TOP_K):
        first = jnp.min(jnp.where(remaining, eidx, ne), axis=0, keepdims=True)
        pick = eidx == first
        pos_k = jnp.sum(jnp.where(pick, pos, 0.0), axis=0, keepdims=True)
        pos_rows.append(jnp.where(first < ne, pos_k, float(dummy_row + k * tm) + tok))
        w_rows.append(jnp.sum(jnp.where(pick, comb, 0.0), axis=0, keepdims=True))
        remaining = jnp.logical_and(remaining, jnp.logical_not(pick))
    pos_ref[...] = jnp.concatenate(pos_rows, axis=0).astype(jnp.int32)
    wmat = jnp.concatenate(w_rows + [jnp.zeros((LANES - TOP_K, tm), F32)], axis=0)
    w_ref[...] = wmat.T


def _moe_token_tile(n):
    return _tile(n, 256, LANES)


def _moe_plan(comb_t, counts, n_tiles, dummy_row):
    ne, n = comb_t.shape
    tm = _moe_token_tile(n)
    ntp = -(-n_tiles // LANES) * LANES
    return pl.pallas_call(
        functools.partial(_moe_plan_body, tile_rows=MOE_TILE, dummy_row=dummy_row),
        grid=(n // tm,),
        in_specs=[pl.BlockSpec((ne, tm), lambda i: (0, i)),
                  pl.BlockSpec((ne, LANES), lambda i: (0, 0))],
        out_specs=[pl.BlockSpec((TOP_K, tm), lambda i: (0, i)),
                   pl.BlockSpec((tm, LANES), lambda i: (i, 0)),
                   pl.BlockSpec((SUBLANES, ntp), lambda i: (0, 0))],
        out_shape=[jax.ShapeDtypeStruct((TOP_K, n), jnp.int32), jax.ShapeDtypeStruct((n, LANES), F32),
                   jax.ShapeDtypeStruct((SUBLANES, ntp), jnp.int32)],
        scratch_shapes=[pltpu.VMEM((ne, LANES), F32)],
        compiler_params=_cparams(("arbitrary",)),
        name="moe_plan",
    )(comb_t, counts)


def _start_row_copies(pos_ref, tm, make_copy):
    def start(t, c):
        for k in range(TOP_K):
            make_copy(t, k, pos_ref[k, t]).start(priority=k % 2)
        return c

    lax.fori_loop(0, tm, start, 0)


def _moe_dispatch_body(pos_ref, hp_ref, xg_in, xg_hbm, sem, *, tm):
    del xg_in
    _start_row_copies(pos_ref, tm, lambda t, k, p: pltpu.make_async_copy(
        hp_ref.at[pl.ds(t, 1)], xg_hbm.at[pl.ds(p, 1)], sem))
    for _ in range(TOP_K):
        pltpu.make_async_copy(hp_ref, hp_ref, sem).wait()


def _moe_dispatch(pos, hp, total_rows):
    n, half = hp.shape
    tm = _moe_token_tile(n)
    xg0 = jnp.zeros((total_rows, half), jnp.uint32)
    return pl.pallas_call(
        functools.partial(_moe_dispatch_body, tm=tm),
        grid=(n // tm,),
        in_specs=[pl.BlockSpec((TOP_K, tm), lambda i: (0, i), memory_space=pltpu.SMEM),
                  pl.BlockSpec((tm, half), lambda i: (i, 0)),
                  pl.BlockSpec(memory_space=pl.ANY)],
        out_specs=pl.BlockSpec(memory_space=pl.ANY),
        out_shape=jax.ShapeDtypeStruct((total_rows, half), jnp.uint32),
        scratch_shapes=[pltpu.SemaphoreType.DMA],
        input_output_aliases={2: 0},
        compiler_params=_cparams(("arbitrary",)),
        name="moe_dispatch",
    )(pos, hp, xg0)


def _moe_ffn_body(te_ref, tv_ref, xg_ref, wg_ref, wu_ref, wd_ref, ys_ref, wg_bf, wu_bf, wd_bf):
    j = pl.program_id(0)
    new_expert = jnp.logical_or(j == 0, te_ref[j] != te_ref[jnp.maximum(j - 1, 0)])

    @pl.when(new_expert)
    def _():
        wg_bf[...] = wg_ref[0, 0].astype(BF16)
        wu_bf[...] = wu_ref[0, 0].astype(BF16)
        wd_bf[...] = wd_ref[0, 0].astype(BF16)

    @pl.when(tv_ref[j] != 0)
    def _():
        lo, hi = _unpack_halves(xg_ref[...])
        half = lo.shape[1]
        hg = _dot(lo, wg_bf[:half, :]) + _dot(hi, wg_bf[half:, :])
        hu = _dot(lo, wu_bf[:half, :]) + _dot(hi, wu_bf[half:, :])
        act = hg * _sigmoid(hg) * hu
        ys_ref[...] = _pack_halves(_dot(act, wd_bf[...]))

    @pl.when(tv_ref[j] == 0)
    def _():
        ys_ref[...] = jnp.zeros_like(ys_ref)


def _moe_ffn(tile_expert, tile_valid, xg, w_gate, w_up, w_down, layer):
    rows, half = xg.shape
    _, ne, d, ff = w_gate.shape
    blk = pl.BlockSpec((MOE_TILE, half), lambda j, te, tv: (j, 0))
    return pl.pallas_call(
        _moe_ffn_body,
        grid_spec=pltpu.PrefetchScalarGridSpec(
            num_scalar_prefetch=2,
            grid=(rows // MOE_TILE,),
            in_specs=[blk,
                      pl.BlockSpec((1, 1, d, ff), lambda j, te, tv: (layer, te[j], 0, 0)),
                      pl.BlockSpec((1, 1, d, ff), lambda j, te, tv: (layer, te[j], 0, 0)),
                      pl.BlockSpec((1, 1, ff, d), lambda j, te, tv: (layer, te[j], 0, 0))],
            out_specs=blk,
            scratch_shapes=[pltpu.VMEM((d, ff), BF16), pltpu.VMEM((d, ff), BF16), pltpu.VMEM((ff, d), BF16)]),
        out_shape=jax.ShapeDtypeStruct((rows, half), jnp.uint32),
        compiler_params=_cparams(("arbitrary",)),
        name="moe_ffn",
    )(tile_expert, tile_valid, xg, w_gate, w_up, w_down)


def _moe_combine_body(pos_ref, w_ref, ys_hbm, o_ref, buf, sem, *, tm):
    _start_row_copies(pos_ref, tm, lambda t, k, p: pltpu.make_async_copy(
        ys_hbm.at[pl.ds(p, 1)], buf.at[k, pl.ds(t, 1)], sem))
    pltpu.make_async_copy(buf, buf, sem).wait()
    w = w_ref[...]
    half = buf.shape[2]
    acc_lo = jnp.zeros((tm, half), F32)
    acc_hi = jnp.zeros((tm, half), F32)
    for k in range(TOP_K):
        lo, hi = _unpack_halves(buf[k])
        wk = w[:, k:k + 1]
        acc_lo = acc_lo + wk * lo
        acc_hi = acc_hi + wk * hi
    o_ref[:, :half] = acc_lo
    o_ref[:, half:] = acc_hi


def _moe_combine(pos, w_tok, ys):
    n = w_tok.shape[0]
    half = ys.shape[1]
    tm = _tile(n, 128, LANES)
    return pl.pallas_call(
        functools.partial(_moe_combine_body, tm=tm),
        grid=(n // tm,),
        in_specs=[pl.BlockSpec((TOP_K, tm), lambda i: (0, i), memory_space=pltpu.SMEM),
                  pl.BlockSpec((tm, LANES), lambda i: (i, 0)),
                  pl.BlockSpec(memory_space=pl.ANY)],
        out_specs=pl.BlockSpec((tm, 2 * half), lambda i: (i, 0)),
        out_shape=jax.ShapeDtypeStruct((n, 2 * half), F32),
        scratch_shapes=[pltpu.VMEM((TOP_K, tm, half), jnp.uint32), pltpu.SemaphoreType.DMA],
        compiler_params=_cparams(("arbitrary",)),
        name="moe_combine",
    )(pos, w_tok, ys)


def _moe_routed(hp, comb_t, counts, w_gate, w_up, w_down, layer):
    n = hp.shape[0]
    ne = w_gate.shape[1]
    run_tiles = (n * TOP_K) // MOE_TILE + ne
    dummy_row = run_tiles * MOE_TILE
    spare_tiles = -(-(TOP_K * _moe_token_tile(n)) // MOE_TILE)
    n_tiles = run_tiles + spare_tiles
    pos, w_tok, te = _moe_plan(comb_t, counts, n_tiles, dummy_row)
    xg = _moe_dispatch(pos, hp, n_tiles * MOE_TILE)
    ys = _moe_ffn(te[0, :n_tiles], te[1, :n_tiles], xg, w_gate, w_up, w_down, layer)
    return _moe_combine(pos, w_tok, ys)


def _moe_shared_residual(h, w_gate, w_up, w_down, f_routed, x2, mod, gate_k, rows_per_batch):
    n, d = x2.shape
    g_act = _matmul(h, w_gate, F32, name="shared_gate")
    tm = _tile(n, 512)
    tn = _tile(w_up.shape[1], 1024, LANES)
    act = _matmul(h, w_up, BF16, tm=tm, tn=tn, epilogue=lambda acc, g: g * _sigmoid(g) * acc,
                  extras=[(g_act, pl.BlockSpec((tm, tn), lambda i, j: (i, j)))], name="shared_up")
    tm = _tile(n if rows_per_batch is None else rows_per_batch, 512)
    tn = _tile(d, 1024, LANES)
    nd = d // tn
    mrow = _mod_row_fn(rows_per_batch, tm)
    blk = pl.BlockSpec((tm, tn), lambda i, j: (i, j))
    return _matmul(act, w_down, F32, tm=tm, tn=tn,
                   epilogue=lambda acc, fr, xb, g: xb + g[0] * (acc + fr),
                   extras=[(f_routed, blk), (x2, blk),
                           (mod, pl.BlockSpec((1, 1, tn), lambda i, j: (mrow(i), 0, gate_k * nd + j)))],
                   name="shared_down")


def _token_mixer(h, hc, lw, lam_init, tabs, need_ctx, batch, t_len, c_len):
    cos, sin = tabs
    flat_rows = _tile(hc.shape[0], 512)
    flat_tabs = (jnp.ones((flat_rows, LANES), F32), jnp.zeros((flat_rows, LANES), F32))

    def project(hh, seq_len, full, positional):
        n = hh.shape[0]
        rope = (cos, sin, seq_len) if positional else flat_tabs + (None,)
        out = {}
        out["k"] = _proj_qk(hh, lw["w_k"], lw["da_k_norm"], rope[0], rope[1], 1.0, rope[2], "proj_k")
        out["v"] = _matmul(hh, lw["w_v"], BF16, name="proj_v")
        out["rw"] = _matmul(hh, lw["w_rw"], F32, name="proj_rw")
        tm = _tile(seq_len, 512)
        nt = seq_len // tm
        tn = S5_WIDTH
        out["s5"] = _matmul(
            hh, lw["w_s5"], F32, tn=tn, grid_m=n // tm,
            a_spec=pl.BlockSpec((tm, hh.shape[1]), lambda i, j: (i, 0)),
            out_spec=pl.BlockSpec((tm, tn), lambda i, j: (i % nt, i // nt)),
            out_shape=(seq_len, batch * S5_WIDTH), name="proj_s5").reshape(seq_len, batch, S5_WIDTH)
        if full:
            out["q"] = _proj_qk(hh, lw["w_q"], lw["da_q_norm"], rope[0], rope[1], DA_Q_SCALE, rope[2], "proj_q")
            out["gates"] = _matmul(hh, lw["w_gates"], BF16, epilogue=_sigmoid, name="proj_gates")
        return out

    pl_ = project(h, t_len, True, True)
    pc_ = project(hc, c_len, need_ctx, False)

    y_a = _diff_attention(pl_["q"], pl_["k"], pl_["v"], pc_["k"], pc_["v"], lw["da_lambda_q"], lw["da_lambda_k"],
                          lw["da_subln"], lam_init, batch)
    y_ac = None
    if need_ctx:
        y_ac = _diff_attention(pc_["q"], None, None, pc_["k"], pc_["v"], lw["da_lambda_q"], lw["da_lambda_k"],
                               lw["da_subln"], lam_init, batch)

    def prep(p_rw, seq_len):
        names = ("lw_f", "lw_b", "kk", "kka_f", "kka_b", "kd_f", "kd_b", "v", "r", "gd")
        vals = _rwkv_prep(p_rw, seq_len, lw["rw_conv"], lw["rw_lora"], lw["rw_w0a0"], lw["rw_k_k"], lw["rw_k_a"])
        return dict(zip(names, vals))

    tl, tc = prep(pl_["rw"], t_len), prep(pc_["rw"], c_len)
    gw = RW_GROUP_HEADS * RW_HEAD_DIM
    s_zero = jnp.zeros((batch * (RW_WIDTH // gw), gw, gw), F32)
    ocf, ocb, s_ctx_f, s_ctx_b = _rwkv_scan(tc, s_zero, s_zero, batch, want_out=need_ctx)
    olf, olb, _, _ = _rwkv_scan(tl, s_ctx_f, s_ctx_b, batch)
    o_lat, o_ctx = {"f": olf, "b": olb}, {"f": ocf, "b": ocb}
    ro = lambda o, t: _rwkv_readout(o["f"], o["b"], t["r"], t["kd_f"], t["kd_b"], t["v"], t["gd"],
                                    lw["rw_gn_w"], lw["rw_gn_b"], lw["rw_r_k"], lw["rw_g2"])
    y_r = ro(o_lat, tl)
    y_rc = ro(o_ctx, tc) if need_ctx else None

    h_zero = jnp.zeros((2, batch, S5_LANES), F32)
    ys_lat, ys_ctx = {}, {}
    for d, rev in (("f", False), ("b", True)):
        yc, h_ctx = _s5_scan(pc_["s5"], h_zero, lw["s5_" + d], reverse=rev, want_out=need_ctx)
        yl, _ = _s5_scan(pl_["s5"], h_ctx, lw["s5_" + d], reverse=rev)
        ys_lat[d], ys_ctx[d] = yl, yc
    flat = lambda a: a.reshape(a.shape[0] * batch, S5_WIDTH)
    glu = lambda p, ys, seq: _s5_glu(flat(p["s5"]), flat(ys["f"]), flat(ys["b"]), lw["s5_d"], lw["s5_glu_w"],
                                     lw["s5_glu_b"]).reshape(seq, batch * S5_WIDTH)
    y_s = glu(pl_, ys_lat, t_len)
    y_sc = glu(pc_, ys_ctx, c_len) if need_ctx else None

    m = _merge(y_a, y_r, y_s, pl_["gates"], lw["w_branch_a"], lw["w_branch_r"], lw["w_branch_s"], t_len, batch)
    m_c = None
    if need_ctx:
        m_c = _merge(y_ac, y_rc, y_sc, pc_["gates"], lw["w_branch_a"], lw["w_branch_r"], lw["w_branch_s"],
                     c_len, batch)
    return m, m_c


def _out_proj_residual(m, w_out, x2, mod, gate_k, rows_per_batch):
    n, d = x2.shape
    tm = _tile(n if rows_per_batch is None else rows_per_batch, 512)
    tn = _tile(d, 1024, LANES)
    nd = d // tn
    mrow = _mod_row_fn(rows_per_batch, tm)

    def epilogue(acc, xb, g):
        return xb + g[0] * acc

    return _matmul(m, w_out, F32, tm=tm, tn=tn, epilogue=epilogue,
                   extras=[(x2, pl.BlockSpec((tm, tn), lambda i, j: (i, j))),
                           (mod, pl.BlockSpec((1, 1, tn), lambda i, j: (mrow(i), 0, gate_k * nd + j)))],
                   name="out_proj")


def _prepare_layer(i, p):
    w_in = p["w_in"][i]
    c0 = DA_WIDTH
    c1 = 2 * DA_WIDTH
    c2 = c1 + RW_STATE_COLS
    c3 = c2 + S5_WIDTH
    c4 = c3 + DA_WIDTH
    c5 = c4 + RW_OUT_COLS
    bf = lambda a: a.astype(BF16)
    lw = {
        "w_k": bf(w_in[:, :c0]), "w_v": bf(w_in[:, c0:c1]),
        "w_rw": bf(jnp.concatenate([w_in[:, c1:c2], w_in[:, c4:c5]], axis=1)),
        "w_s5": bf(w_in[:, c2:c3]), "w_q": bf(w_in[:, c3:c4]), "w_gates": bf(w_in[:, c5:]),
    }
    for name in ("da_q_norm", "da_k_norm", "da_lambda_q", "da_lambda_k", "da_subln", "rw_conv", "rw_k_k", "rw_k_a",
                 "rw_gn_w", "rw_gn_b", "s5_d", "s5_glu_b", "router_bias"):
        lw[name] = p[name][i].astype(F32)
    w = RW_WIDTH
    lora = jnp.zeros((2 * RW_DECAY_RANK + 2 * RW_A_RANK, 4 * w), F32)
    r0 = 0
    for blk, src in enumerate((p["rw_w2"][i][0], p["rw_w2"][i][1], p["rw_a2"][i][0], p["rw_a2"][i][1])):
        lora = lora.at[r0:r0 + src.shape[0], blk * w:(blk + 1) * w].set(src.astype(F32))
        r0 += src.shape[0]
    lw["rw_lora"] = lora
    lw["rw_w0a0"] = jnp.concatenate([p["rw_w0"][i][0], p["rw_w0"][i][1], p["rw_a0"][i][0], p["rw_a0"][i][1]]
                                    ).astype(F32).reshape(1, 4 * w)
    lw["rw_r_k"] = p["rw_r_k"][i].astype(F32).reshape(w)
    lw["rw_g2"] = bf(p["rw_g2"][i])
    for d, name in enumerate(("s5_f", "s5_b")):
        lw[name] = _s5_dir_params(p["s5_lambda_re"][i][d], p["s5_lambda_im"][i][d], p["s5_log_dt"][i][d],
                                  p["s5_b_re"][i].astype(F32), p["s5_b_im"][i].astype(F32),
                                  p["s5_c_re"][i][d], p["s5_c_im"][i][d])
    lw["s5_glu_w"] = bf(p["s5_glu_w"][i])
    for name in ("w_branch_a", "w_branch_r", "w_branch_s", "w_out"):
        lw[name] = bf(p[name][i])
    for name in ("exp_w_gate", "exp_w_up", "exp_w_down"):
        lw[name] = p[name]
    lw["layer"] = i
    d_model = w_in.shape[0]
    rw_t = p["router_w"][i].astype(F32).T
    lw["router_w_t"] = jnp.concatenate([rw_t, jnp.zeros((LANES - N_EXPERTS, d_model), F32)], axis=0)
    for name in ("sh_w_gate", "sh_w_up", "sh_w_down"):
        lw[name] = bf(p[name][i])
    return lw


def kernel(x, c, ctx, c_ctx, ada_w, ada_b, w_in, da_q_norm, da_k_norm, da_lambda_q, da_lambda_k, da_subln, rw_conv, rw_w0, rw_w2, rw_a0, rw_a2, rw_g2, rw_k_k, rw_k_a, rw_r_k, rw_gn_w, rw_gn_b, s5_lambda_re, s5_lambda_im, s5_log_dt, s5_b_re, s5_b_im, s5_c_re, s5_c_im, s5_d, s5_glu_w, s5_glu_b, w_branch_a, w_branch_r, w_branch_s, w_out, router_w, router_bias, exp_w_gate, exp_w_up, exp_w_down, sh_w_gate, sh_w_up, sh_w_down):
    params = dict(w_in=w_in, da_q_norm=da_q_norm, da_k_norm=da_k_norm, da_lambda_q=da_lambda_q,
                  da_lambda_k=da_lambda_k, da_subln=da_subln, rw_conv=rw_conv, rw_w0=rw_w0, rw_w2=rw_w2,
                  rw_a0=rw_a0, rw_a2=rw_a2, rw_g2=rw_g2, rw_k_k=rw_k_k, rw_k_a=rw_k_a, rw_r_k=rw_r_k,
                  rw_gn_w=rw_gn_w, rw_gn_b=rw_gn_b, s5_lambda_re=s5_lambda_re, s5_lambda_im=s5_lambda_im,
                  s5_log_dt=s5_log_dt, s5_b_re=s5_b_re, s5_b_im=s5_b_im, s5_c_re=s5_c_re, s5_c_im=s5_c_im,
                  s5_d=s5_d, s5_glu_w=s5_glu_w, s5_glu_b=s5_glu_b, w_branch_a=w_branch_a, w_branch_r=w_branch_r,
                  w_branch_s=w_branch_s, w_out=w_out, router_w=router_w, router_bias=router_bias,
                  exp_w_gate=exp_w_gate, exp_w_up=exp_w_up, exp_w_down=exp_w_down, sh_w_gate=sh_w_gate,
                  sh_w_up=sh_w_up, sh_w_down=sh_w_down)
    batch, t_len, d_model = x.shape
    c_len = ctx.shape[1]
    depth = ada_w.shape[0]
    assert batch <= MOD_CTX_ROW
    tabs = _rope_tables(t_len)
    cvec = jnp.zeros((2 * SUBLANES, d_model), F32).at[:batch].set(c.astype(F32)).at[MOD_CTX_ROW].set(c_ctx.astype(F32))
    x2 = x.astype(F32).reshape(batch * t_len, d_model)
    ctx2 = ctx.astype(F32).reshape(batch * c_len, d_model)
    for i in range(depth):
        lw = _prepare_layer(i, params)
        need_ctx = i < depth - 1
        lam_init = 0.8 - 0.6 * math.exp(-0.3 * i)
        mod = _ada_table(cvec, ada_w[i], ada_b[i])
        h = _modulate(x2, mod, 0, 1, t_len)
        hc = _modulate(ctx2, mod, 0, 1, None)
        m, m_c = _token_mixer(h, hc, lw, lam_init, tabs, need_ctx, batch, t_len, c_len)
        x2 = _out_proj_residual(m, lw["w_out"], x2, mod, 2, t_len)
        streams = [(x2, t_len)]
        if need_ctx:
            ctx2 = _out_proj_residual(m_c, lw["w_out"], ctx2, mod, 2, None)
            streams.append((ctx2, None))
        outs = []
        for xs, rpb in streams:
            h2, h2p, comb_t, counts = _modulate_route(xs, mod, 3, 4, rpb, lw["router_w_t"], lw["router_bias"])
            f_r = _moe_routed(h2p, comb_t, counts, lw["exp_w_gate"], lw["exp_w_up"], lw["exp_w_down"], lw["layer"])
            outs.append(_moe_shared_residual(h2, lw["sh_w_gate"], lw["sh_w_up"], lw["sh_w_down"], f_r,
                                             xs, mod, 5, rpb))
        x2 = outs[0]
        if need_ctx:
            ctx2 = outs[1]
    return x2.reshape(batch, t_len, d_model).astype(x.dtype)
```

```python
import functools
import math

import jax
import jax.numpy as jnp
from jax import lax
from jax.experimental import pallas as pl
from jax.experimental.pallas import tpu as pltpu

F32 = jnp.float32
BF16 = jnp.bfloat16

GRID_W = 64
NORM_EPS = 1e-6
DA_HEADS = 16
DA_HEAD_DIM = 64
DA_V_DIM = 2 * DA_HEAD_DIM
DA_WIDTH = DA_HEADS * DA_V_DIM
DA_SCALE = DA_HEAD_DIM ** -0.5
DA_Q_SCALE = DA_SCALE * math.log2(math.e)
ROPE_BASE = 10000.0
RW_HEADS = 16
RW_HEAD_DIM = 64
RW_WIDTH = RW_HEADS * RW_HEAD_DIM
RW_DECAY_RANK = 64
RW_A_RANK = 64
RW_GATE_RANK = 128
RW_GN_EPS = 64e-5
RW_STATE_COLS = 2 * RW_WIDTH + 2 * RW_DECAY_RANK + 2 * RW_A_RANK
RW_OUT_COLS = RW_WIDTH + RW_GATE_RANK
S5_GROUP = 16
S5_GROUPS = 64
S5_WIDTH = S5_GROUPS * S5_GROUP
S5_STATE = 64
S5_LANES = S5_GROUPS * S5_STATE
N_EXPERTS = 64
TOP_K = 8
N_GROUPS = 8
TOPK_GROUPS = 4
EXPERT_FF = 256
ROUTED_SCALE = 2.5

LANES = 128
SUBLANES = 8
VMEM_LIMIT_BYTES = 56 * 1024 * 1024

RW_CHUNK = 64
RW_GROUP_HEADS = 4
S5_CHUNK = 64
MOD_CTX_ROW = 8
MOE_TILE = 256


def _cparams(sem):
    return pltpu.CompilerParams(dimension_semantics=sem, vmem_limit_bytes=VMEM_LIMIT_BYTES)


def _tile(n, pref, mult=SUBLANES):
    if n <= pref:
        return n
    t = (pref // mult) * mult
    while t > mult and n % t:
        t -= mult
    assert n % t == 0, (n, pref)
    return t


def _dot(a, b):
    return jnp.dot(a.astype(BF16), b.astype(BF16), preferred_element_type=F32)


def _dot_nt(a, b):
    return lax.dot_general(a.astype(BF16), b.astype(BF16), (((1,), (1,)), ((), ())),
                           preferred_element_type=F32)


def _dot_tn(a, b):
    return lax.dot_general(a.astype(BF16), b.astype(BF16), (((0,), (0,)), ((), ())),
                           preferred_element_type=F32)


def _hilo(x):
    hi = x.astype(BF16)
    lo = (x - hi.astype(F32)).astype(BF16)
    return hi, lo


def _dot_hp_lhs(a, b_exact):
    hi, lo = _hilo(a)
    return (jnp.dot(hi, b_exact, preferred_element_type=F32)
            + jnp.dot(lo, b_exact, preferred_element_type=F32))


def _dot_hp(a, b):
    ah, al = _hilo(a)
    bh, bl = _hilo(b)
    return (jnp.dot(ah, bh, preferred_element_type=F32) + jnp.dot(al, bh, preferred_element_type=F32)
            + jnp.dot(ah, bl, preferred_element_type=F32))


def _sigmoid(x):
    return 1.0 / (1.0 + jnp.exp(-x))


def _softplus(x):
    return jnp.maximum(x, 0.0) + jnp.log(1.0 + jnp.exp(-jnp.abs(x)))


def _gelu_tanh(x):
    c = math.sqrt(2.0 / math.pi)
    return 0.5 * x * (1.0 + jnp.tanh(c * (x + 0.044715 * (x * x * x))))


def _mm_body(a_ref, b_ref, *rest, n_extra, prologue, epilogue):
    extras = rest[:n_extra]
    o_ref = rest[n_extra]
    a = a_ref[...]
    if prologue is not None:
        a = prologue(a)
    acc = jnp.dot(a.astype(BF16), b_ref[...].astype(BF16), preferred_element_type=F32)
    if epilogue is not None:
        acc = epilogue(acc, *[e[...] for e in extras])
    o_ref[...] = acc.astype(o_ref.dtype)


def _matmul(a, b, out_dtype, *, tm=512, tn=1024, prologue=None, epilogue=None, extras=(),
            a_spec=None, out_spec=None, out_shape=None, grid_m=None, name="matmul"):
    k, n = b.shape
    tn = _tile(n, tn, LANES)
    if a_spec is None:
        m = a.shape[0]
        tm = _tile(m, tm)
        grid_m = m // tm
        a_spec = pl.BlockSpec((tm, k), lambda i, j: (i, 0))
    if out_spec is None:
        out_spec = pl.BlockSpec((tm, tn), lambda i, j: (i, j))
        out_shape = (a.shape[0], n)
    body = functools.partial(_mm_body, n_extra=len(extras), prologue=prologue, epilogue=epilogue)
    return pl.pallas_call(
        body,
        grid=(grid_m, n // tn),
        in_specs=[a_spec, pl.BlockSpec((k, tn), lambda i, j: (0, j))] + [s for _, s in extras],
        out_specs=out_spec,
        out_shape=jax.ShapeDtypeStruct(out_shape, out_dtype),
        compiler_params=_cparams(("parallel", "arbitrary")),
        name=name,
    )(a, b, *[x for x, _ in extras])


def _ada_table(cvec, ada_w_l, ada_b_l):
    d6 = ada_w_l.shape[1]

    def prologue(a):
        return a * _sigmoid(a)

    def epilogue(acc, bias):
        return acc + bias

    out = _matmul(cvec, ada_w_l, F32, tm=16, tn=512, prologue=prologue, epilogue=epilogue,
                  extras=[(ada_b_l.reshape(1, d6), pl.BlockSpec((1, 512), lambda i, j: (0, j)))],
                  name="ada_table")
    return out.reshape(cvec.shape[0], 1, d6)


def _mod_row_fn(rows_per_batch, tm):
    if rows_per_batch is None:
        return lambda i: MOD_CTX_ROW
    nb = rows_per_batch // tm
    return lambda i: i // nb


def _modulate_body(x_ref, sh_ref, sc_ref, o_ref):
    x = x_ref[...]
    ms = jnp.mean(x * x, axis=-1, keepdims=True)
    h = x * lax.rsqrt(ms + NORM_EPS) * (1.0 + sc_ref[0]) + sh_ref[0]
    o_ref[...] = h.astype(o_ref.dtype)


def _modulate(x2, mod, shift_k, scale_k, rows_per_batch):
    n, d = x2.shape
    tm = _tile(n if rows_per_batch is None else rows_per_batch, 256)
    row = _mod_row_fn(rows_per_batch, tm)
    return pl.pallas_call(
        _modulate_body,
        grid=(n // tm,),
        in_specs=[pl.BlockSpec((tm, d), lambda i: (i, 0)),
                  pl.BlockSpec((1, 1, d), lambda i: (row(i), 0, shift_k)),
                  pl.BlockSpec((1, 1, d), lambda i: (row(i), 0, scale_k))],
        out_specs=pl.BlockSpec((tm, d), lambda i: (i, 0)),
        out_shape=jax.ShapeDtypeStruct((n, d), BF16),
        compiler_params=_cparams(("parallel",)),
        name="modulate",
    )(x2, mod, mod)


def _pack_halves(x):
    w = x.shape[1] // 2
    lo = lax.bitcast_convert_type(x[:, :w].astype(BF16).astype(F32), jnp.uint32)
    hi = lax.bitcast_convert_type(x[:, w:].astype(BF16).astype(F32), jnp.uint32)
    return lax.shift_right_logical(lo, jnp.uint32(16)) | hi


def _unpack_halves(p):
    lo = lax.bitcast_convert_type(lax.shift_left(p, jnp.uint32(16)), F32)
    hi = lax.bitcast_convert_type(p & jnp.uint32(0xFFFF0000), F32)
    return lo, hi


def _route_body(x_ref, sh_ref, sc_ref, wr_ref, bias_ref, h_ref, hp_ref, comb_ref, cnt_ref):
    x = x_ref[...]
    ms = jnp.mean(x * x, axis=-1, keepdims=True)
    h = x * lax.rsqrt(ms + NORM_EPS) * (1.0 + sc_ref[0]) + sh_ref[0]
    h_ref[...] = h.astype(h_ref.dtype)
    hp_ref[...] = _pack_halves(h)
    tm = x.shape[0]
    wr = wr_ref[...]
    hh, hl = _hilo(h)
    wh, wl = _hilo(wr)
    logits = _dot_nt(wh, hh) + _dot_nt(wl, hh) + _dot_nt(wh, hl)
    scores = _sigmoid(logits[:N_EXPERTS])
    per_group = N_EXPERTS // N_GROUPS
    sc3 = scores.reshape(N_GROUPS, per_group, tm)
    sel = sc3 + bias_ref[...]
    midx = lax.broadcasted_iota(jnp.int32, sel.shape, 1)
    neg = jnp.float32(-jnp.inf)
    m1 = jnp.max(sel, axis=1, keepdims=True)
    first = jnp.min(jnp.where(sel == m1, midx, per_group), axis=1, keepdims=True)
    m2 = jnp.max(jnp.where(midx == first, neg, sel), axis=1, keepdims=True)
    gs = (m1 + m2).reshape(N_GROUPS, tm)
    gidx = lax.broadcasted_iota(jnp.int32, gs.shape, 0)
    gmask = jnp.zeros(gs.shape, jnp.bool_)
    for _ in range(TOPK_GROUPS):
        m = jnp.max(gs, axis=0, keepdims=True)
        f = jnp.min(jnp.where(gs == m, gidx, N_GROUPS), axis=0, keepdims=True)
        pick = gidx == f
        gmask = jnp.logical_or(gmask, pick)
        gs = jnp.where(pick, neg, gs)
    val = jnp.where(gmask.reshape(N_GROUPS, 1, tm), sel, neg)
    eidx = lax.broadcasted_iota(jnp.int32, sel.shape, 0) * per_group + midx
    chosen = jnp.zeros(sel.shape, jnp.bool_)
    for _ in range(TOP_K):
        m = jnp.max(jnp.max(val, axis=1, keepdims=True), axis=0, keepdims=True)
        f = jnp.min(jnp.min(jnp.where(val == m, eidx, N_EXPERTS), axis=1, keepdims=True), axis=0, keepdims=True)
        pick = eidx == f
        chosen = jnp.logical_or(chosen, pick)
        val = jnp.where(pick, neg, val)
    w = jnp.where(chosen, sc3, 0.0)
    wsum = jnp.sum(jnp.sum(w, axis=1, keepdims=True), axis=0, keepdims=True)
    comb = (w / wsum * ROUTED_SCALE).reshape(N_EXPERTS, tm)
    comb_ref[...] = comb

    @pl.when(pl.program_id(0) == 0)
    def _():
        cnt_ref[...] = jnp.zeros_like(cnt_ref)

    cnt_ref[...] += jnp.sum((comb > 0.0).astype(F32), axis=1, keepdims=True)


def _modulate_route(x2, mod, shift_k, scale_k, rows_per_batch, router_w_t, router_bias):
    n, d = x2.shape
    tm = _tile(n if rows_per_batch is None else rows_per_batch, 256, LANES)
    row = _mod_row_fn(rows_per_batch, tm)
    return pl.pallas_call(
        _route_body,
        grid=(n // tm,),
        in_specs=[pl.BlockSpec((tm, d), lambda i: (i, 0)),
                  pl.BlockSpec((1, 1, d), lambda i: (row(i), 0, shift_k)),
                  pl.BlockSpec((1, 1, d), lambda i: (row(i), 0, scale_k)),
                  pl.BlockSpec((LANES, d), lambda i: (0, 0)),
                  pl.BlockSpec((N_GROUPS, N_EXPERTS // N_GROUPS, 1), lambda i: (0, 0, 0))],
        out_specs=[pl.BlockSpec((tm, d), lambda i: (i, 0)),
                   pl.BlockSpec((tm, d // 2), lambda i: (i, 0)),
                   pl.BlockSpec((N_EXPERTS, tm), lambda i: (0, i)),
                   pl.BlockSpec((N_EXPERTS, LANES), lambda i: (0, 0))],
        out_shape=[jax.ShapeDtypeStruct((n, d), BF16), jax.ShapeDtypeStruct((n, d // 2), jnp.uint32),
                   jax.ShapeDtypeStruct((N_EXPERTS, n), F32), jax.ShapeDtypeStruct((N_EXPERTS, LANES), F32)],
        compiler_params=_cparams(("arbitrary",)),
        name="modulate_route",
    )(x2, mod, mod, router_w_t, router_bias.reshape(N_GROUPS, N_EXPERTS // N_GROUPS, 1))


def _group_ones(width, group):
    r = lax.broadcasted_iota(jnp.int32, (width, width), 0) // group
    c = lax.broadcasted_iota(jnp.int32, (width, width), 1) // group
    return (r == c).astype(BF16)


def _qk_norm_rope(x, gain, cos, sin, scale):
    ss = _dot_hp_lhs(x * x, _group_ones(LANES, DA_HEAD_DIM))
    xn = x * lax.rsqrt(ss * (1.0 / DA_HEAD_DIM) + NORM_EPS) * gain
    lane = lax.broadcasted_iota(jnp.int32, x.shape, 1)
    quarter = DA_HEAD_DIM // 4
    partner = jnp.where((lane % (2 * quarter)) < quarter,
                        pltpu.roll(xn, LANES - quarter, axis=1),
                        pltpu.roll(xn, quarter, axis=1))
    return (xn * cos + partner * sin) * scale


def _proj_qk(h, w, gain, cos, sin, scale, t_len, name):
    n = h.shape[0]
    tm = _tile(n if t_len is None else t_len, 512)
    nb = 1 if t_len is None else t_len // tm
    tab = pl.BlockSpec((tm, LANES), lambda i, j: (i % nb, 0))
    gain2 = jnp.tile(gain.reshape(1, DA_HEAD_DIM), (1, 2))

    def epilogue(acc, g, c, s):
        heads = [_qk_norm_rope(acc[:, k * LANES:(k + 1) * LANES], g, c, s, scale)
                 for k in range(acc.shape[1] // LANES)]
        return jnp.concatenate(heads, axis=1)

    return _matmul(h, w, BF16, tm=tm, epilogue=epilogue,
                   extras=[(gain2, pl.BlockSpec((1, LANES), lambda i, j: (0, 0))), (cos, tab), (sin, tab)],
                   name=name)


def _rope_tables(t_len):
    rows = t_len // GRID_W
    row = jnp.repeat(jnp.arange(rows, dtype=F32), GRID_W)
    col = jnp.tile(jnp.arange(GRID_W, dtype=F32), rows)
    half = DA_HEAD_DIM // 2
    inv_freq = 1.0 / (ROPE_BASE ** (jnp.arange(0, half, 2, dtype=F32) / half))
    ang_r = row[:, None] * inv_freq
    ang_c = col[:, None] * inv_freq
    cos64 = jnp.concatenate([jnp.cos(ang_r), jnp.cos(ang_r), jnp.cos(ang_c), jnp.cos(ang_c)], axis=1)
    sin64 = jnp.concatenate([-jnp.sin(ang_r), jnp.sin(ang_r), -jnp.sin(ang_c), jnp.sin(ang_c)], axis=1)
    return jnp.tile(cos64, (1, 2)), jnp.tile(sin64, (1, 2))


def _attn_body(*refs, has_lat, lam_init):
    if has_lat:
        q_ref, kl_ref, vl_ref, kc_ref, vc_ref, lq_ref, lk_ref, sub_ref, o_ref = refs
    else:
        q_ref, kc_ref, vc_ref, lq_ref, lk_ref, sub_ref, o_ref = refs
    lqk = lq_ref[...] * lk_ref[...]
    lsum = jnp.sum(lqk, axis=1, keepdims=True)
    e = jnp.exp(lsum)
    lam = e[0:1, :] - e[1:2, :] + lam_init
    q = q_ref[...]
    lane = lax.broadcasted_iota(jnp.int32, q.shape, 1)
    zero = jnp.zeros_like(q)
    ext = lambda v: jnp.concatenate([v, jnp.ones_like(v)], axis=1)
    vc_ext = ext(vc_ref[...])
    vl_ext = ext(vl_ref[...]) if has_lat else None
    mixed = []
    for m in range(2):
        in_map = (lane // DA_HEAD_DIM) == m
        qm = jnp.where(in_map, q, zero)
        s_c = _dot_nt(qm, kc_ref[...]).astype(BF16)
        mx = jnp.max(s_c, axis=-1, keepdims=True)
        if has_lat:
            s_l = _dot_nt(qm, kl_ref[...]).astype(BF16)
            mx = jnp.maximum(mx, jnp.max(s_l, axis=-1, keepdims=True))
        acc = jnp.dot(jnp.exp2(s_c - mx), vc_ext, preferred_element_type=F32)
        if has_lat:
            acc = acc + jnp.dot(jnp.exp2(s_l - mx), vl_ext, preferred_element_type=F32)
        mixed.append(acc[:, :DA_V_DIM] / acc[:, DA_V_DIM:])
    o = mixed[0] - lam * mixed[1]
    ms = jnp.mean(o * o, axis=-1, keepdims=True)
    o = o * lax.rsqrt(ms + NORM_EPS) * sub_ref[...] * (1.0 - lam_init)
    o_ref[...] = o.astype(o_ref.dtype)


def _diff_attention(q, k_lat, v_lat, k_ctx, v_ctx, lq, lk, subln, lam_init, batch):
    n, w = q.shape
    tq_len = n // batch
    c_len = k_ctx.shape[0] // batch
    tq = _tile(tq_len, 1024)
    nq = tq_len // tq
    has_lat = k_lat is not None
    blk = lambda rows: pl.BlockSpec((rows, LANES), lambda b, h, i: (b, h))
    in_specs = [pl.BlockSpec((tq, LANES), lambda b, h, i: (b * nq + i, h))]
    args = [q]
    if has_lat:
        t_len = k_lat.shape[0] // batch
        in_specs += [blk(t_len), blk(t_len)]
        args += [k_lat, v_lat]
    in_specs += [blk(c_len), blk(c_len),
                 pl.BlockSpec((2, DA_HEAD_DIM), lambda b, h, i: (0, 0)),
                 pl.BlockSpec((2, DA_HEAD_DIM), lambda b, h, i: (0, 0)),
                 pl.BlockSpec((1, LANES), lambda b, h, i: (0, 0))]
    args += [k_ctx, v_ctx, lq, lk, subln.reshape(1, DA_V_DIM)]
    return pl.pallas_call(
        functools.partial(_attn_body, has_lat=has_lat, lam_init=lam_init),
        grid=(batch, w // LANES, nq),
        in_specs=in_specs,
        out_specs=pl.BlockSpec((tq, LANES), lambda b, h, i: (b * nq + i, h)),
        out_shape=jax.ShapeDtypeStruct((n, w), BF16),
        compiler_params=_cparams(("parallel", "parallel", "arbitrary")),
        name="diff_attention",
    )(*args)


def _rwkv_prep_body(x_ref, xp_ref, xn_ref, conv_ref, lora_ref, w0a0_ref, kk_w_ref, ka_w_ref, ones_ref,
                    lwf_ref, lwb_ref, kk_ref, kkaf_ref, kkab_ref, kdf_ref, kdb_ref, v_ref, r_ref, gd_ref,
                    *, blocks_per_seq):
    i = pl.program_id(0)
    x = x_ref[...]
    tm = x.shape[0]
    first = (i % blocks_per_seq) == 0
    last = (i % blocks_per_seq) == blocks_per_seq - 1
    xp = jnp.where(first, 0.0, xp_ref[SUBLANES - 1:SUBLANES, :])
    xn = jnp.where(last, 0.0, xn_ref[0:1, :])
    row = lax.broadcasted_iota(jnp.int32, (tm, 1), 0)
    up = jnp.where(row == 0, xp, pltpu.roll(x, 1, axis=0))
    dn = jnp.where(row == tm - 1, xn, pltpu.roll(x, tm - 1, axis=0))
    cw = conv_ref[...]
    cv = up * cw[0:1, :] + x * cw[1:2, :] + dn * cw[2:3, :]
    w = RW_WIDTH
    k = cv[:, :w]
    v = cv[:, w:2 * w]
    lora_in = cv[:, 2 * w:RW_STATE_COLS]
    r = cv[:, RW_STATE_COLS:RW_STATE_COLS + w]
    gd = cv[:, RW_STATE_COLS + w:]
    lane = lax.broadcasted_iota(jnp.int32, lora_in.shape, 1)
    li = jnp.where(lane < 2 * RW_DECAY_RANK, jnp.tanh(lora_in), lora_in)
    li_hi, li_lo = _hilo(li)
    pre = (jnp.dot(li_hi, lora_ref[0], preferred_element_type=F32)
           + jnp.dot(li_lo, lora_ref[0], preferred_element_type=F32)
           + jnp.dot(li_hi, lora_ref[1], preferred_element_type=F32)) + w0a0_ref[...]
    kkr = k * kk_w_ref[...]
    ss = _dot_hp_lhs(kkr * kkr, ones_ref[...])
    kk = kkr * lax.rsqrt(ss + 1e-12)
    kk_ref[...] = kk
    v_ref[...] = v
    r_ref[...] = r
    gd_ref[...] = gd
    ka = ka_w_ref[...]
    for d, (lw_ref, kka_ref, kd_ref) in enumerate(((lwf_ref, kkaf_ref, kdf_ref), (lwb_ref, kkab_ref, kdb_ref))):
        lw_ref[...] = -math.exp(-0.5) * _sigmoid(pre[:, d * w:(d + 1) * w])
        a = _sigmoid(pre[:, (2 + d) * w:(3 + d) * w])
        kka_ref[...] = kk * a
        kd_ref[...] = k * (1.0 + (a - 1.0) * ka)


def _rwkv_prep(p_rw, seq_len, conv, lora_w, w0a0, k_k, k_a):
    n, c = p_rw.shape
    tm = _tile(seq_len, 128)
    bps = seq_len // tm
    sub = tm // SUBLANES
    nsub = n // SUBLANES
    w = RW_WIDTH
    wide = lambda: pl.BlockSpec((tm, w), lambda i: (i, 0))
    outs = [jax.ShapeDtypeStruct((n, w), F32)] * 9 + [jax.ShapeDtypeStruct((n, RW_GATE_RANK), F32)]
    return pl.pallas_call(
        functools.partial(_rwkv_prep_body, blocks_per_seq=bps),
        grid=(n // tm,),
        in_specs=[pl.BlockSpec((tm, c), lambda i: (i, 0)),
                  pl.BlockSpec((SUBLANES, c), lambda i: (jnp.maximum(i * sub - 1, 0), 0)),
                  pl.BlockSpec((SUBLANES, c), lambda i: (jnp.minimum((i + 1) * sub, nsub - 1), 0)),
                  pl.BlockSpec((3, c), lambda i: (0, 0)),
                  pl.BlockSpec(lora_w.shape, lambda i: (0, 0, 0)),
                  pl.BlockSpec((1, 4 * w), lambda i: (0, 0)),
                  pl.BlockSpec((1, w), lambda i: (0, 0)),
                  pl.BlockSpec((1, w), lambda i: (0, 0)),
                  pl.BlockSpec((w, w), lambda i: (0, 0))],
        out_specs=[wide() for _ in range(9)] + [pl.BlockSpec((tm, RW_GATE_RANK), lambda i: (i, 0))],
        out_shape=outs,
        compiler_params=_cparams(("parallel",)),
        name="rwkv_prep",
    )(p_rw, p_rw, p_rw, conv, lora_w, w0a0, k_k.reshape(1, w), k_a.reshape(1, w), _group_ones(w, RW_HEAD_DIM))


def _rwkv_scan_body(lwf_ref, kkaf_ref, kdf_ref, kkf_ref, vf_ref, rf_ref,
                    lwb_ref, kkab_ref, kdb_ref, kkb_ref, vb_ref, rb_ref, s0f_ref, s0b_ref,
                    of_ref, ob_ref, stf_ref, stb_ref, *, want_out):
    ci = pl.program_id(1)

    @pl.when(ci == 0)
    def _():
        stf_ref[...] = s0f_ref[...]
        stb_ref[...] = s0b_ref[...]

    n_l, width = lwf_ref.shape
    hd = RW_HEAD_DIM
    gw = RW_GROUP_HEADS * hd
    n_groups = width // gw
    row = lax.broadcasted_iota(jnp.int32, (n_l, n_l), 0)
    col = lax.broadcasted_iota(jnp.int32, (n_l, n_l), 1)
    trow = lax.broadcasted_iota(jnp.int32, (n_l, gw), 0)
    tcol = lax.broadcasted_iota(jnp.int32, (n_l, gw), 1) % hd
    eye = (tcol == trow).astype(F32)
    same_head = ((lax.broadcasted_iota(jnp.int32, (gw, gw), 0) // hd)
                 == (lax.broadcasted_iota(jnp.int32, (gw, gw), 1) // hd))
    reps = gw // n_l

    def bdiag(x):
        xb = x.astype(BF16)
        return jnp.where(same_head, jnp.concatenate([xb] * reps, axis=0), jnp.zeros((), BF16))

    units = []
    for reverse, refs in ((False, (lwf_ref, kkaf_ref, kdf_ref, kkf_ref, vf_ref, rf_ref, stf_ref, of_ref)),
                          (True, (lwb_ref, kkab_ref, kdb_ref, kkb_ref, vb_ref, rb_ref, stb_ref, ob_ref))):
        lw_ref, kka_ref, k_ref, kk_ref, v_ref, r_ref, st_ref, o_ref = refs
        lw = lw_ref[...]
        tri = ((col >= row) if reverse else (col <= row)).astype(BF16)
        lh, ll = _hilo(lw)
        c = jnp.dot(tri, lh, preferred_element_type=F32) + jnp.dot(tri, ll, preferred_element_type=F32)
        g_end = jnp.exp(c[0:1, :] if reverse else c[n_l - 1:n_l, :])
        e_inv = jnp.exp(-c)
        a_t = kk_ref[...] * jnp.exp(c - lw)
        r_t = r_ref[...] * jnp.exp(c)
        k_h = k_ref[...] * e_inv
        b_h = kka_ref[...] * e_inv
        v_all = v_ref[...]
        incl, strict = (tcol >= trow, tcol > trow) if reverse else (tcol <= trow, tcol < trow)
        for g in range(n_groups):
            s = slice(g * gw, (g + 1) * gw)
            units.append(dict(g=g, sl=s, incl=incl, strict=strict, st_ref=st_ref, o_ref=o_ref, state=st_ref[g],
                              ar=jnp.concatenate([a_t[:, s], r_t[:, s]], axis=0).astype(BF16),
                              k_h=k_h[:, s], b_h=b_h[:, s], v=v_all[:, s], g_end=g_end[:, s]))

    n_iter = max(1, (n_l - 1).bit_length()) - 1
    sc_k = [_dot_nt(x["ar"], bdiag(x["k_h"])) for x in units]
    sc_b = [_dot_nt(x["ar"], bdiag(x["b_h"])) for x in units]
    from_s = [_dot_nt(x["ar"], x["state"]) for x in units]
    p = [-jnp.where(x["strict"], sb[:n_l], 0.0) for x, sb in zip(units, sc_b)]
    t_inv = [eye + pi for pi in p]
    if n_iter:
        p = [_dot(pi, bdiag(pi)) for pi in p]
    for it in range(n_iter):
        if it + 1 < n_iter:
            y = [_dot(jnp.concatenate([ti, pi], axis=0), bdiag(pi)) for ti, pi in zip(t_inv, p)]
            t_inv = [ti + yi[:n_l] for ti, yi in zip(t_inv, y)]
            p = [yi[n_l:] for yi in y]
        else:
            t_inv = [ti + _dot(ti, bdiag(pi)) for ti, pi in zip(t_inv, p)]
    v_bd = [bdiag(x["v"]) for x in units]
    w = [fs[:n_l] + _dot(jnp.where(x["strict"], sk[:n_l], 0.0), vb)
         for x, fs, sk, vb in zip(units, from_s, sc_k, v_bd)]
    u = [_dot(ti, bdiag(wi)) for ti, wi in zip(t_inv, w)]
    if want_out:
        o_v = [_dot(jnp.where(x["incl"], sk[n_l:], 0.0), vb) for x, sk, vb in zip(units, sc_k, v_bd)]
        o_u = [_dot(jnp.where(x["incl"], sb[n_l:], 0.0), bdiag(ui)) for x, sb, ui in zip(units, sc_b, u)]
        for x, fs, ov, ou in zip(units, from_s, o_v, o_u):
            x["o_ref"][:, x["sl"]] = fs[n_l:] + ov - ou
    else:
        of_ref[...] = jnp.zeros_like(of_ref)
        ob_ref[...] = jnp.zeros_like(ob_ref)
    upd_k = [_dot_tn(x["v"], x["k_h"] * x["g_end"]) for x in units]
    upd_b = [_dot_tn(ui, x["b_h"] * x["g_end"]) for x, ui in zip(units, u)]
    for x, uk, ub in zip(units, upd_k, upd_b):
        x["st_ref"][x["g"]] = jnp.where(same_head, x["state"] * x["g_end"] + uk - ub, 0.0)


def _rwkv_scan(t, s0_f, s0_b, batch, *, want_out=True):
    n, w = t["kk"].shape
    seq = n // batch
    n_l = _tile(seq, RW_CHUNK)
    nch = seq // n_l
    gw = RW_GROUP_HEADS * RW_HEAD_DIM
    assert gw % n_l == 0 and w % gw == 0
    n_groups = w // gw
    fwd = pl.BlockSpec((n_l, w), lambda b, c: (b * nch + c, 0))
    rev = pl.BlockSpec((n_l, w), lambda b, c: (b * nch + nch - 1 - c, 0))
    st_spec = pl.BlockSpec((n_groups, gw, gw), lambda b, c: (b, 0, 0))
    seq_shape = jax.ShapeDtypeStruct((n, w), F32)
    st_shape = jax.ShapeDtypeStruct(s0_f.shape, F32)
    return pl.pallas_call(
        functools.partial(_rwkv_scan_body, want_out=want_out),
        grid=(batch, nch),
        in_specs=[fwd] * 6 + [rev] * 6 + [st_spec, st_spec],
        out_specs=[fwd, rev, st_spec, st_spec],
        out_shape=[seq_shape, seq_shape, st_shape, st_shape],
        compiler_params=_cparams(("parallel", "arbitrary")),
        name="rwkv_scan",
    )(t["lw_f"], t["kka_f"], t["kd_f"], t["kk"], t["v"], t["r"],
      t["lw_b"], t["kka_b"], t["kd_b"], t["kk"], t["v"], t["r"], s0_f, s0_b)


def _rwkv_readout_body(of_ref, ob_ref, r_ref, kdf_ref, kdb_ref, v_ref, gd_ref, gnw_ref, gnb_ref, rk_ref, g2_ref,
                       ones_ref, y_ref):
    ones = ones_ref[...]
    inv = 1.0 / RW_HEAD_DIM
    o = of_ref[...] + ob_ref[...]
    mu = _dot_hp_lhs(o, ones) * inv
    d = o - mu
    var = _dot_hp_lhs(d * d, ones) * inv
    on = d * lax.rsqrt(var + RW_GN_EPS) * gnw_ref[...] + gnb_ref[...]
    r = r_ref[...]
    rk = rk_ref[...]
    bonus = _dot_hp_lhs(r * kdf_ref[...] * rk, ones) + _dot_hp_lhs(r * kdb_ref[...] * rk, ones)
    y = on + bonus * v_ref[...]
    g = _dot(_sigmoid(gd_ref[...]), g2_ref[...])
    y_ref[...] = (y * g).astype(y_ref.dtype)


def _rwkv_readout(o_f, o_b, r, kd_f, kd_b, v, gd, gn_w, gn_b, r_k, g2):
    n, w = o_f.shape
    tm = _tile(n, 256)
    wide = pl.BlockSpec((tm, w), lambda i: (i, 0))
    vec = pl.BlockSpec((1, w), lambda i: (0, 0))
    return pl.pallas_call(
        _rwkv_readout_body,
        grid=(n // tm,),
        in_specs=[wide] * 6 + [pl.BlockSpec((tm, RW_GATE_RANK), lambda i: (i, 0)), vec, vec, vec,
                               pl.BlockSpec((RW_GATE_RANK, w), lambda i: (0, 0)),
                               pl.BlockSpec((w, w), lambda i: (0, 0))],
        out_specs=wide,
        out_shape=jax.ShapeDtypeStruct((n, w), BF16),
        compiler_params=_cparams(("parallel",)),
        name="rwkv_readout",
    )(o_f, o_b, r, kd_f, kd_b, v, gd, gn_w.reshape(1, w), gn_b.reshape(1, w), r_k.reshape(1, w), g2,
      _group_ones(w, RW_HEAD_DIM))


def _s5_scan_body(u_ref, bre_ref, bim_ref, are_ref, aim_ref, cre_ref, cim_ref, h0_ref, y_ref, ht_ref,
                  dre, dim, *, reverse, want_out):
    ci = pl.program_id(0)

    @pl.when(ci == 0)
    def _():
        ht_ref[...] = h0_ref[...]

    tt, nb, wu = u_ref.shape
    nblk = wu // LANES
    sw = S5_LANES // nblk
    u2 = u_ref[...].reshape(tt * nb, wu).astype(BF16)
    for c in range(nblk):
        uc = u2[:, c * LANES:(c + 1) * LANES]
        dre[:, :, c * sw:(c + 1) * sw] = jnp.dot(uc, bre_ref[c], preferred_element_type=F32).reshape(tt, nb, sw)
        dim[:, :, c * sw:(c + 1) * sw] = jnp.dot(uc, bim_ref[c], preferred_element_type=F32).reshape(tt, nb, sw)
    lw = 1024
    for c in range(S5_LANES // lw):
        ls = slice(c * lw, (c + 1) * lw)
        ar = jnp.broadcast_to(are_ref[:, ls], (nb, lw))
        ai = jnp.broadcast_to(aim_ref[:, ls], (nb, lw))

        def step(s, carry, ls=ls, ar=ar, ai=ai):
            t = (tt - 1 - s) if reverse else s
            hr, hi = carry
            nr = ar * hr - ai * hi + dre[t, :, ls]
            ni = ar * hi + ai * hr + dim[t, :, ls]
            dre[t, :, ls] = nr
            dim[t, :, ls] = ni
            return nr, ni

        hr, hi = lax.fori_loop(0, tt, step, (ht_ref[0, :, ls], ht_ref[1, :, ls]), unroll=2)
        ht_ref[0, :, ls] = hr
        ht_ref[1, :, ls] = hi
    if want_out:
        xr = dre[...].reshape(tt * nb, S5_LANES).astype(BF16)
        xi = dim[...].reshape(tt * nb, S5_LANES).astype(BF16)
        for c in range(nblk):
            yc = (jnp.dot(xr[:, c * sw:(c + 1) * sw], cre_ref[c], preferred_element_type=F32)
                  - jnp.dot(xi[:, c * sw:(c + 1) * sw], cim_ref[c], preferred_element_type=F32))
            y_ref[:, :, c * LANES:(c + 1) * LANES] = yc.reshape(tt, nb, LANES)
    else:
        y_ref[...] = jnp.zeros_like(y_ref)


def _s5_scan(u_tm, h0, p, *, reverse, want_out=True):
    t_len, nb, wu = u_tm.shape
    tt = _tile(t_len, S5_CHUNK)
    nch = t_len // tt
    chunk = (lambda c: (nch - 1 - c, 0, 0)) if reverse else (lambda c: (c, 0, 0))
    const3 = lambda a: pl.BlockSpec(a.shape, lambda c: (0, 0, 0))
    const2 = lambda a: pl.BlockSpec(a.shape, lambda c: (0, 0))
    return pl.pallas_call(
        functools.partial(_s5_scan_body, reverse=reverse, want_out=want_out),
        grid=(nch,),
        in_specs=[pl.BlockSpec((tt, nb, wu), chunk), const3(p["b_re"]), const3(p["b_im"]),
                  const2(p["a_re"]), const2(p["a_im"]), const3(p["c_re"]), const3(p["c_im"]), const3(h0)],
        out_specs=[pl.BlockSpec((tt, nb, wu), chunk), const3(h0)],
        out_shape=[jax.ShapeDtypeStruct(u_tm.shape, F32), jax.ShapeDtypeStruct(h0.shape, F32)],
        scratch_shapes=[pltpu.VMEM((tt, nb, S5_LANES), F32), pltpu.VMEM((tt, nb, S5_LANES), F32)],
        compiler_params=_cparams(("arbitrary",)),
        name="s5_scan_rev" if reverse else "s5_scan_fwd",
    )(u_tm, p["b_re"], p["b_im"], p["a_re"], p["a_im"], p["c_re"], p["c_im"], h0)


def _s5_dir_params(lam_re, lam_im, log_dt, b_re, b_im, c_re, c_im):
    g, pdim = lam_re.shape
    dt = jnp.exp(log_dt.astype(F32))[:, None]
    mag = jnp.exp(lam_re * dt)
    abar_re = mag * jnp.cos(lam_im * dt)
    abar_im = mag * jnp.sin(lam_im * dt)
    den = lam_re * lam_re + lam_im * lam_im
    nr = abar_re - 1.0
    g_re = (nr * lam_re + abar_im * lam_im) / den
    g_im = (abar_im * lam_re - nr * lam_im) / den
    bb_re = g_re[:, :, None] * b_re - g_im[:, :, None] * b_im
    bb_im = g_re[:, :, None] * b_im + g_im[:, :, None] * b_re
    gpb = LANES // S5_GROUP
    nblk = g // gpb
    eye = jnp.eye(gpb, dtype=F32)

    def drive_mat(bb):
        x = bb.reshape(nblk, gpb, pdim, S5_GROUP)
        x = jnp.einsum("cgph,gk->cghkp", x, eye)
        return x.reshape(nblk, gpb * S5_GROUP, gpb * pdim).astype(BF16)

    def read_mat(cc):
        x = cc.reshape(nblk, gpb, S5_GROUP, pdim)
        x = jnp.einsum("cghp,gk->cgpkh", x, eye)
        return x.reshape(nblk, gpb * pdim, gpb * S5_GROUP).astype(BF16)

    return {"a_re": abar_re.reshape(1, g * pdim), "a_im": abar_im.reshape(1, g * pdim),
            "b_re": drive_mat(bb_re), "b_im": drive_mat(bb_im),
            "c_re": read_mat(c_re.astype(F32)), "c_im": read_mat(c_im.astype(F32))}


def _s5_glu_body(u_ref, yf_ref, yb_ref, d_ref, w_ref, b_ref, o_ref):
    y = u_ref[...] * d_ref[...] + yf_ref[...] + yb_ref[...]
    y = _gelu_tanh(y)
    z = _dot(y, w_ref[...]) + b_ref[...]
    o_ref[...] = (y * _sigmoid(z)).astype(o_ref.dtype)


def _s5_glu(u, y_f, y_b, d_skip, glu_w, glu_b):
    n, w = u.shape
    tm = _tile(n, 512)
    wide = pl.BlockSpec((tm, w), lambda i: (i, 0))
    vec = pl.BlockSpec((1, w), lambda i: (0, 0))
    return pl.pallas_call(
        _s5_glu_body,
        grid=(n // tm,),
        in_specs=[wide, wide, wide, vec, pl.BlockSpec((w, w), lambda i: (0, 0)), vec],
        out_specs=wide,
        out_shape=jax.ShapeDtypeStruct((n, w), BF16),
        compiler_params=_cparams(("parallel",)),
        name="s5_glu",
    )(u, y_f, y_b, d_skip.reshape(1, w), glu_w, glu_b.reshape(1, w))


def _merge_body(ya_ref, yr_ref, ys_ref, wa_ref, wr_ref, ws_ref, ga_ref, gr_ref, gs_ref, o_ref):
    m = (ga_ref[...].astype(F32) * jnp.dot(ya_ref[...], wa_ref[...], preferred_element_type=F32)
         + gr_ref[...].astype(F32) * jnp.dot(yr_ref[...], wr_ref[...], preferred_element_type=F32)
         + gs_ref[...].astype(F32) * jnp.dot(ys_ref[...], ws_ref[...], preferred_element_type=F32))
    o_ref[...] = m.astype(o_ref.dtype)


def _merge(y_a, y_r, y_s_tm, gates, w_a, w_r, w_s, seq_len, batch):
    n = y_a.shape[0]
    d = w_a.shape[1]
    tm = _tile(seq_len, 512)
    tn = _tile(d, 1024, LANES)
    nt = seq_len // tm
    nd = d // tn
    row = lambda i, j: (i, 0)
    return pl.pallas_call(
        _merge_body,
        grid=(n // tm, nd),
        in_specs=[pl.BlockSpec((tm, y_a.shape[1]), row),
                  pl.BlockSpec((tm, y_r.shape[1]), row),
                  pl.BlockSpec((tm, S5_WIDTH), lambda i, j: (i % nt, i // nt)),
                  pl.BlockSpec((w_a.shape[0], tn), lambda i, j: (0, j)),
                  pl.BlockSpec((w_r.shape[0], tn), lambda i, j: (0, j)),
                  pl.BlockSpec((w_s.shape[0], tn), lambda i, j: (0, j)),
                  pl.BlockSpec((tm, tn), lambda i, j: (i, j)),
                  pl.BlockSpec((tm, tn), lambda i, j: (i, nd + j)),
                  pl.BlockSpec((tm, tn), lambda i, j: (i, 2 * nd + j))],
        out_specs=pl.BlockSpec((tm, tn), lambda i, j: (i, j)),
        out_shape=jax.ShapeDtypeStruct((n, d), BF16),
        compiler_params=_cparams(("parallel", "arbitrary")),
        name="merge",
    )(y_a, y_r, y_s_tm, w_a, w_r, w_s, gates, gates, gates)


def _moe_plan_body(comb_ref, cnt_ref, pos_ref, w_ref, te_ref, carry_ref, *, tile_rows, dummy_row):
    i = pl.program_id(0)
    ne, tm = comb_ref.shape
    tiles = jnp.floor((cnt_ref[...] + (tile_rows - 1)) * (1.0 / tile_rows))
    er = lax.broadcasted_iota(jnp.int32, (ne, ne), 0)
    ec = lax.broadcasted_iota(jnp.int32, (ne, ne), 1)
    t_hi, t_lo = _hilo(tiles)
    lower = (ec < er).astype(BF16)
    off_tiles = (jnp.dot(lower, t_hi, preferred_element_type=F32)
                 + jnp.dot(lower, t_lo, preferred_element_type=F32))

    @pl.when(i == 0)
    def _():
        carry_ref[...] = jnp.zeros_like(carry_ref)
        ntp = te_ref.shape[1]
        end_tiles = (off_tiles + tiles)[:, 0:1]
        tile_idx = lax.broadcasted_iota(jnp.int32, (1, ntp), 1).astype(F32)
        expert = jnp.sum((end_tiles <= tile_idx).astype(F32), axis=0, keepdims=True)
        expert = jnp.minimum(expert, ne - 1.0)
        valid = (tile_idx < jnp.max(end_tiles, axis=0, keepdims=True)).astype(F32)
        r8 = lax.broadcasted_iota(jnp.int32, te_ref.shape, 0)
        te_ref[...] = jnp.where(r8 == 0, expert, jnp.where(r8 == 1, valid, 0.0)).astype(jnp.int32)

    comb = comb_ref[...]
    chosen = comb > 0.0
    chf = chosen.astype(BF16)
    tr = lax.broadcasted_iota(jnp.int32, (tm, tm), 0)
    tc = lax.broadcasted_iota(jnp.int32, (tm, tm), 1)
    rank = jnp.dot(chf, (tr < tc).astype(BF16), preferred_element_type=F32)
    pos = off_tiles[:, 0:1] * tile_rows + carry_ref[:, 0:1] + rank
    carry_ref[...] += jnp.sum(chosen.astype(F32), axis=1, keepdims=True)
    eidx = lax.broadcasted_iota(jnp.int32, (ne, tm), 0)
    tok = lax.broadcasted_iota(jnp.int32, (1, tm), 1).astype(F32)
    remaining = chosen
    pos_rows, w_rows = [], []
    for k in range(TOP_K):
        first = jnp.min(jnp.where(remaining, eidx, ne), axis=0, keepdims=True)
        pick = eidx == first
        pos_k = jnp.sum(jnp.where(pick, pos, 0.0), axis=0, keepdims=True)
        pos_rows.append(jnp.where(first < ne, pos_k, float(dummy_row + k * tm) + tok))
        w_rows.append(jnp.sum(jnp.where(pick, comb, 0.0), axis=0, keepdims=True))
        remaining = jnp.logical_and(remaining, jnp.logical_not(pick))
    pos_ref[...] = jnp.concatenate(pos_rows, axis=0).astype(jnp.int32)
    wmat = jnp.concatenate(w_rows + [jnp.zeros((LANES - TOP_K, tm), F32)], axis=0)
    w_ref[...] = wmat.T


def _moe_token_tile(n):
    return _tile(n, 256, LANES)


def _moe_plan(comb_t, counts, n_tiles, dummy_row):
    ne, n = comb_t.shape
    tm = _moe_token_tile(n)
    ntp = -(-n_tiles // LANES) * LANES
    return pl.pallas_call(
        functools.partial(_moe_plan_body, tile_rows=MOE_TILE, dummy_row=dummy_row),
        grid=(n // tm,),
        in_specs=[pl.BlockSpec((ne, tm), lambda i: (0, i)),
                  pl.BlockSpec((ne, LANES), lambda i: (0, 0))],
        out_specs=[pl.BlockSpec((TOP_K, tm), lambda i: (0, i)),
                   pl.BlockSpec((tm, LANES), lambda i: (i, 0)),
                   pl.BlockSpec((SUBLANES, ntp), lambda i: (0, 0))],
        out_shape=[jax.ShapeDtypeStruct((TOP_K, n), jnp.int32), jax.ShapeDtypeStruct((n, LANES), F32),
                   jax.ShapeDtypeStruct((SUBLANES, ntp), jnp.int32)],
        scratch_shapes=[pltpu.VMEM((ne, LANES), F32)],
        compiler_params=_cparams(("arbitrary",)),
        name="moe_plan",
    )(comb_t, counts)


def _start_row_copies(pos_ref, tm, make_copy):
    def start(t, c):
        for k in range(TOP_K):
            make_copy(t, k, pos_ref[k, t]).start(priority=k % 2)
        return c

    lax.fori_loop(0, tm, start, 0)


def _moe_dispatch_body(pos_ref, hp_ref, xg_in, xg_hbm, sem, *, tm):
    del xg_in
    _start_row_copies(pos_ref, tm, lambda t, k, p: pltpu.make_async_copy(
        hp_ref.at[pl.ds(t, 1)], xg_hbm.at[pl.ds(p, 1)], sem))
    for _ in range(TOP_K):
        pltpu.make_async_copy(hp_ref, hp_ref, sem).wait()


def _moe_dispatch(pos, hp, total_rows):
    n, half = hp.shape
    tm = _moe_token_tile(n)
    xg0 = jnp.zeros((total_rows, half), jnp.uint32)
    return pl.pallas_call(
        functools.partial(_moe_dispatch_body, tm=tm),
        grid=(n // tm,),
        in_specs=[pl.BlockSpec((TOP_K, tm), lambda i: (0, i), memory_space=pltpu.SMEM),
                  pl.BlockSpec((tm, half), lambda i: (i, 0)),
                  pl.BlockSpec(memory_space=pl.ANY)],
        out_specs=pl.BlockSpec(memory_space=pl.ANY),
        out_shape=jax.ShapeDtypeStruct((total_rows, half), jnp.uint32),
        scratch_shapes=[pltpu.SemaphoreType.DMA],
        input_output_aliases={2: 0},
        compiler_params=_cparams(("arbitrary",)),
        name="moe_dispatch",
    )(pos, hp, xg0)


def _moe_ffn_body(te_ref, tv_ref, xg_ref, wg_ref, wu_ref, wd_ref, ys_ref, wg_bf, wu_bf, wd_bf):
    j = pl.program_id(0)
    new_expert = jnp.logical_or(j == 0, te_ref[j] != te_ref[jnp.maximum(j - 1, 0)])

    @pl.when(new_expert)
    def _():
        wg_bf[...] = wg_ref[0, 0].astype(BF16)
        wu_bf[...] = wu_ref[0, 0].astype(BF16)
        wd_bf[...] = wd_ref[0, 0].astype(BF16)

    @pl.when(tv_ref[j] != 0)
    def _():
        lo, hi = _unpack_halves(xg_ref[...])
        half = lo.shape[1]
        hg = _dot(lo, wg_bf[:half, :]) + _dot(hi, wg_bf[half:, :])
        hu = _dot(lo, wu_bf[:half, :]) + _dot(hi, wu_bf[half:, :])
        act = hg * _sigmoid(hg) * hu
        ys_ref[...] = _pack_halves(_dot(act, wd_bf[...]))

    @pl.when(tv_ref[j] == 0)
    def _():
        ys_ref[...] = jnp.zeros_like(ys_ref)


def _moe_ffn(tile_expert, tile_valid, xg, w_gate, w_up, w_down, layer):
    rows, half = xg.shape
    _, ne, d, ff = w_gate.shape
    blk = pl.BlockSpec((MOE_TILE, half), lambda j, te, tv: (j, 0))
    return pl.pallas_call(
        _moe_ffn_body,
        grid_spec=pltpu.PrefetchScalarGridSpec(
            num_scalar_prefetch=2,
            grid=(rows // MOE_TILE,),
            in_specs=[blk,
                      pl.BlockSpec((1, 1, d, ff), lambda j, te, tv: (layer, te[j], 0, 0)),
                      pl.BlockSpec((1, 1, d, ff), lambda j, te, tv: (layer, te[j], 0, 0)),
                      pl.BlockSpec((1, 1, ff, d), lambda j, te, tv: (layer, te[j], 0, 0))],
            out_specs=blk,
            scratch_shapes=[pltpu.VMEM((d, ff), BF16), pltpu.VMEM((d, ff), BF16), pltpu.VMEM((ff, d), BF16)]),
        out_shape=jax.ShapeDtypeStruct((rows, half), jnp.uint32),
        compiler_params=_cparams(("arbitrary",)),
        name="moe_ffn",
    )(tile_expert, tile_valid, xg, w_gate, w_up, w_down)


def _moe_combine_body(pos_ref, w_ref, ys_hbm, o_ref, buf, sem, *, tm):
    _start_row_copies(pos_ref, tm, lambda t, k, p: pltpu.make_async_copy(
        ys_hbm.at[pl.ds(p, 1)], buf.at[k, pl.ds(t, 1)], sem))
    pltpu.make_async_copy(buf, buf, sem).wait()
    w = w_ref[...]
    half = buf.shape[2]
    acc_lo = jnp.zeros((tm, half), F32)
    acc_hi = jnp.zeros((tm, half), F32)
    for k in range(TOP_K):
        lo, hi = _unpack_halves(buf[k])
        wk = w[:, k:k + 1]
        acc_lo = acc_lo + wk * lo
        acc_hi = acc_hi + wk * hi
    o_ref[:, :half] = acc_lo
    o_ref[:, half:] = acc_hi


def _moe_combine(pos, w_tok, ys):
    n = w_tok.shape[0]
    half = ys.shape[1]
    tm = _tile(n, 128, LANES)
    return pl.pallas_call(
        functools.partial(_moe_combine_body, tm=tm),
        grid=(n // tm,),
        in_specs=[pl.BlockSpec((TOP_K, tm), lambda i: (0, i), memory_space=pltpu.SMEM),
                  pl.BlockSpec((tm, LANES), lambda i: (i, 0)),
                  pl.BlockSpec(memory_space=pl.ANY)],
        out_specs=pl.BlockSpec((tm, 2 * half), lambda i: (i, 0)),
        out_shape=jax.ShapeDtypeStruct((n, 2 * half), F32),
        scratch_shapes=[pltpu.VMEM((TOP_K, tm, half), jnp.uint32), pltpu.SemaphoreType.DMA],
        compiler_params=_cparams(("arbitrary",)),
        name="moe_combine",
    )(pos, w_tok, ys)


def _moe_routed(hp, comb_t, counts, w_gate, w_up, w_down, layer):
    n = hp.shape[0]
    ne = w_gate.shape[1]
    run_tiles = (n * TOP_K) // MOE_TILE + ne
    dummy_row = run_tiles * MOE_TILE
    spare_tiles = -(-(TOP_K * _moe_token_tile(n)) // MOE_TILE)
    n_tiles = run_tiles + spare_tiles
    pos, w_tok, te = _moe_plan(comb_t, counts, n_tiles, dummy_row)
    xg = _moe_dispatch(pos, hp, n_tiles * MOE_TILE)
    ys = _moe_ffn(te[0, :n_tiles], te[1, :n_tiles], xg, w_gate, w_up, w_down, layer)
    return _moe_combine(pos, w_tok, ys)


def _moe_shared_residual(h, w_gate, w_up, w_down, f_routed, f_row0, x2, mod, gate_k, rows_per_batch):
    n, d = x2.shape
    g_act = _matmul(h, w_gate, F32, name="shared_gate")
    tm = _tile(n, 512)
    tn = _tile(w_up.shape[1], 1024, LANES)
    act = _matmul(h, w_up, BF16, tm=tm, tn=tn, epilogue=lambda acc, g: g * _sigmoid(g) * acc,
                  extras=[(g_act, pl.BlockSpec((tm, tn), lambda i, j: (i, j)))], name="shared_up")
    tm = _tile(n if rows_per_batch is None else rows_per_batch, 512)
    tn = _tile(d, 1024, LANES)
    nd = d // tn
    mrow = _mod_row_fn(rows_per_batch, tm)
    blk = pl.BlockSpec((tm, tn), lambda i, j: (i, j))
    assert f_row0 % tm == 0
    f_blk = pl.BlockSpec((tm, tn), lambda i, j: (f_row0 // tm + i, j))
    return _matmul(act, w_down, F32, tm=tm, tn=tn,
                   epilogue=lambda acc, fr, xb, g: xb + g[0] * (acc + fr),
                   extras=[(f_routed, f_blk), (x2, blk),
                           (mod, pl.BlockSpec((1, 1, tn), lambda i, j: (mrow(i), 0, gate_k * nd + j)))],
                   name="shared_down")


def _token_mixer(h, hc, lw, lam_init, tabs, need_ctx, batch, t_len, c_len):
    cos, sin = tabs
    flat_rows = _tile(hc.shape[0], 512)
    flat_tabs = (jnp.ones((flat_rows, LANES), F32), jnp.zeros((flat_rows, LANES), F32))

    def project(hh, seq_len, full, positional):
        n = hh.shape[0]
        rope = (cos, sin, seq_len) if positional else flat_tabs + (None,)
        out = {}
        out["k"] = _proj_qk(hh, lw["w_k"], lw["da_k_norm"], rope[0], rope[1], 1.0, rope[2], "proj_k")
        out["v"] = _matmul(hh, lw["w_v"], BF16, name="proj_v")
        out["rw"] = _matmul(hh, lw["w_rw"], F32, name="proj_rw")
        tm = _tile(seq_len, 512)
        nt = seq_len // tm
        tn = S5_WIDTH
        out["s5"] = _matmul(
            hh, lw["w_s5"], F32, tn=tn, grid_m=n // tm,
            a_spec=pl.BlockSpec((tm, hh.shape[1]), lambda i, j: (i, 0)),
            out_spec=pl.BlockSpec((tm, tn), lambda i, j: (i % nt, i // nt)),
            out_shape=(seq_len, batch * S5_WIDTH), name="proj_s5").reshape(seq_len, batch, S5_WIDTH)
        if full:
            out["q"] = _proj_qk(hh, lw["w_q"], lw["da_q_norm"], rope[0], rope[1], DA_Q_SCALE, rope[2], "proj_q")
            out["gates"] = _matmul(hh, lw["w_gates"], BF16, epilogue=_sigmoid, name="proj_gates")
        return out

    pl_ = project(h, t_len, True, True)
    pc_ = project(hc, c_len, need_ctx, False)

    y_a = _diff_attention(pl_["q"], pl_["k"], pl_["v"], pc_["k"], pc_["v"], lw["da_lambda_q"], lw["da_lambda_k"],
                          lw["da_subln"], lam_init, batch)
    y_ac = None
    if need_ctx:
        y_ac = _diff_attention(pc_["q"], None, None, pc_["k"], pc_["v"], lw["da_lambda_q"], lw["da_lambda_k"],
                               lw["da_subln"], lam_init, batch)

    def prep(p_rw, seq_len):
        names = ("lw_f", "lw_b", "kk", "kka_f", "kka_b", "kd_f", "kd_b", "v", "r", "gd")
        vals = _rwkv_prep(p_rw, seq_len, lw["rw_conv"], lw["rw_lora"], lw["rw_w0a0"], lw["rw_k_k"], lw["rw_k_a"])
        return dict(zip(names, vals))

    tl, tc = prep(pl_["rw"], t_len), prep(pc_["rw"], c_len)
    gw = RW_GROUP_HEADS * RW_HEAD_DIM
    s_zero = jnp.zeros((batch * (RW_WIDTH // gw), gw, gw), F32)
    ocf, ocb, s_ctx_f, s_ctx_b = _rwkv_scan(tc, s_zero, s_zero, batch, want_out=need_ctx)
    olf, olb, _, _ = _rwkv_scan(tl, s_ctx_f, s_ctx_b, batch)
    o_lat, o_ctx = {"f": olf, "b": olb}, {"f": ocf, "b": ocb}
    ro = lambda o, t: _rwkv_readout(o["f"], o["b"], t["r"], t["kd_f"], t["kd_b"], t["v"], t["gd"],
                                    lw["rw_gn_w"], lw["rw_gn_b"], lw["rw_r_k"], lw["rw_g2"])
    y_r = ro(o_lat, tl)
    y_rc = ro(o_ctx, tc) if need_ctx else None

    h_zero = jnp.zeros((2, batch, S5_LANES), F32)
    ys_lat, ys_ctx = {}, {}
    for d, rev in (("f", False), ("b", True)):
        yc, h_ctx = _s5_scan(pc_["s5"], h_zero, lw["s5_" + d], reverse=rev, want_out=need_ctx)
        yl, _ = _s5_scan(pl_["s5"], h_ctx, lw["s5_" + d], reverse=rev)
        ys_lat[d], ys_ctx[d] = yl, yc
    flat = lambda a: a.reshape(a.shape[0] * batch, S5_WIDTH)
    glu = lambda p, ys, seq: _s5_glu(flat(p["s5"]), flat(ys["f"]), flat(ys["b"]), lw["s5_d"], lw["s5_glu_w"],
                                     lw["s5_glu_b"]).reshape(seq, batch * S5_WIDTH)
    y_s = glu(pl_, ys_lat, t_len)
    y_sc = glu(pc_, ys_ctx, c_len) if need_ctx else None

    m = _merge(y_a, y_r, y_s, pl_["gates"], lw["w_branch_a"], lw["w_branch_r"], lw["w_branch_s"], t_len, batch)
    m_c = None
    if need_ctx:
        m_c = _merge(y_ac, y_rc, y_sc, pc_["gates"], lw["w_branch_a"], lw["w_branch_r"], lw["w_branch_s"],
                     c_len, batch)
    return m, m_c


def _out_proj_residual(m, w_out, x2, mod, gate_k, rows_per_batch):
    n, d = x2.shape
    tm = _tile(n if rows_per_batch is None else rows_per_batch, 512)
    tn = _tile(d, 1024, LANES)
    nd = d // tn
    mrow = _mod_row_fn(rows_per_batch, tm)

    def epilogue(acc, xb, g):
        return xb + g[0] * acc

    return _matmul(m, w_out, F32, tm=tm, tn=tn, epilogue=epilogue,
                   extras=[(x2, pl.BlockSpec((tm, tn), lambda i, j: (i, j))),
                           (mod, pl.BlockSpec((1, 1, tn), lambda i, j: (mrow(i), 0, gate_k * nd + j)))],
                   name="out_proj")


def _prepare_layer(i, p):
    w_in = p["w_in"][i]
    c0 = DA_WIDTH
    c1 = 2 * DA_WIDTH
    c2 = c1 + RW_STATE_COLS
    c3 = c2 + S5_WIDTH
    c4 = c3 + DA_WIDTH
    c5 = c4 + RW_OUT_COLS
    bf = lambda a: a.astype(BF16)
    lw = {
        "w_k": bf(w_in[:, :c0]), "w_v": bf(w_in[:, c0:c1]),
        "w_rw": bf(jnp.concatenate([w_in[:, c1:c2], w_in[:, c4:c5]], axis=1)),
        "w_s5": bf(w_in[:, c2:c3]), "w_q": bf(w_in[:, c3:c4]), "w_gates": bf(w_in[:, c5:]),
    }
    for name in ("da_q_norm", "da_k_norm", "da_lambda_q", "da_lambda_k", "da_subln", "rw_conv", "rw_k_k", "rw_k_a",
                 "rw_gn_w", "rw_gn_b", "s5_d", "s5_glu_b", "router_bias"):
        lw[name] = p[name][i].astype(F32)
    w = RW_WIDTH
    lora = jnp.zeros((2 * RW_DECAY_RANK + 2 * RW_A_RANK, 4 * w), F32)
    r0 = 0
    for blk, src in enumerate((p["rw_w2"][i][0], p["rw_w2"][i][1], p["rw_a2"][i][0], p["rw_a2"][i][1])):
        lora = lora.at[r0:r0 + src.shape[0], blk * w:(blk + 1) * w].set(src.astype(F32))
        r0 += src.shape[0]
    lw["rw_lora"] = jnp.stack(_hilo(lora))
    lw["rw_w0a0"] = jnp.concatenate([p["rw_w0"][i][0], p["rw_w0"][i][1], p["rw_a0"][i][0], p["rw_a0"][i][1]]
                                    ).astype(F32).reshape(1, 4 * w)
    lw["rw_r_k"] = p["rw_r_k"][i].astype(F32).reshape(w)
    lw["rw_g2"] = bf(p["rw_g2"][i])
    for d, name in enumerate(("s5_f", "s5_b")):
        lw[name] = _s5_dir_params(p["s5_lambda_re"][i][d], p["s5_lambda_im"][i][d], p["s5_log_dt"][i][d],
                                  p["s5_b_re"][i].astype(F32), p["s5_b_im"][i].astype(F32),
                                  p["s5_c_re"][i][d], p["s5_c_im"][i][d])
    lw["s5_glu_w"] = bf(p["s5_glu_w"][i])
    for name in ("w_branch_a", "w_branch_r", "w_branch_s", "w_out"):
        lw[name] = bf(p[name][i])
    for name in ("exp_w_gate", "exp_w_up", "exp_w_down"):
        lw[name] = p[name]
    lw["layer"] = i
    d_model = w_in.shape[0]
    rw_t = p["router_w"][i].astype(F32).T
    lw["router_w_t"] = jnp.concatenate([rw_t, jnp.zeros((LANES - N_EXPERTS, d_model), F32)], axis=0)
    for name in ("sh_w_gate", "sh_w_up", "sh_w_down"):
        lw[name] = bf(p[name][i])
    return lw


def kernel(x, c, ctx, c_ctx, ada_w, ada_b, w_in, da_q_norm, da_k_norm, da_lambda_q, da_lambda_k, da_subln, rw_conv, rw_w0, rw_w2, rw_a0, rw_a2, rw_g2, rw_k_k, rw_k_a, rw_r_k, rw_gn_w, rw_gn_b, s5_lambda_re, s5_lambda_im, s5_log_dt, s5_b_re, s5_b_im, s5_c_re, s5_c_im, s5_d, s5_glu_w, s5_glu_b, w_branch_a, w_branch_r, w_branch_s, w_out, router_w, router_bias, exp_w_gate, exp_w_up, exp_w_down, sh_w_gate, sh_w_up, sh_w_down):
    params = dict(w_in=w_in, da_q_norm=da_q_norm, da_k_norm=da_k_norm, da_lambda_q=da_lambda_q,
                  da_lambda_k=da_lambda_k, da_subln=da_subln, rw_conv=rw_conv, rw_w0=rw_w0, rw_w2=rw_w2,
                  rw_a0=rw_a0, rw_a2=rw_a2, rw_g2=rw_g2, rw_k_k=rw_k_k, rw_k_a=rw_k_a, rw_r_k=rw_r_k,
                  rw_gn_w=rw_gn_w, rw_gn_b=rw_gn_b, s5_lambda_re=s5_lambda_re, s5_lambda_im=s5_lambda_im,
                  s5_log_dt=s5_log_dt, s5_b_re=s5_b_re, s5_b_im=s5_b_im, s5_c_re=s5_c_re, s5_c_im=s5_c_im,
                  s5_d=s5_d, s5_glu_w=s5_glu_w, s5_glu_b=s5_glu_b, w_branch_a=w_branch_a, w_branch_r=w_branch_r,
                  w_branch_s=w_branch_s, w_out=w_out, router_w=router_w, router_bias=router_bias,
                  exp_w_gate=exp_w_gate, exp_w_up=exp_w_up, exp_w_down=exp_w_down, sh_w_gate=sh_w_gate,
                  sh_w_up=sh_w_up, sh_w_down=sh_w_down)
    batch, t_len, d_model = x.shape
    c_len = ctx.shape[1]
    depth = ada_w.shape[0]
    assert batch <= MOD_CTX_ROW
    tabs = _rope_tables(t_len)
    cvec = jnp.zeros((2 * SUBLANES, d_model), F32).at[:batch].set(c.astype(F32)).at[MOD_CTX_ROW].set(c_ctx.astype(F32))
    x2 = x.astype(F32).reshape(batch * t_len, d_model)
    ctx2 = ctx.astype(F32).reshape(batch * c_len, d_model)
    for i in range(depth):
        lw = _prepare_layer(i, params)
        need_ctx = i < depth - 1
        lam_init = 0.8 - 0.6 * math.exp(-0.3 * i)
        mod = _ada_table(cvec, ada_w[i], ada_b[i])
        h = _modulate(x2, mod, 0, 1, t_len)
        hc = _modulate(ctx2, mod, 0, 1, None)
        m, m_c = _token_mixer(h, hc, lw, lam_init, tabs, need_ctx, batch, t_len, c_len)
        x2 = _out_proj_residual(m, lw["w_out"], x2, mod, 2, t_len)
        streams = [(x2, t_len)]
        if need_ctx:
            ctx2 = _out_proj_residual(m_c, lw["w_out"], ctx2, mod, 2, None)
            streams.append((ctx2, None))
        routed = [_modulate_route(xs, mod, 3, 4, rpb, lw["router_w_t"], lw["router_bias"]) for xs, rpb in streams]
        f_r = _moe_routed(jnp.concatenate([r[1] for r in routed], axis=0),
                          jnp.concatenate([r[2] for r in routed], axis=1),
                          sum(r[3] for r in routed),
                          lw["exp_w_gate"], lw["exp_w_up"], lw["exp_w_down"], lw["layer"])
        outs, row0 = [], 0
        for (xs, rpb), r in zip(streams, routed):
            outs.append(_moe_shared_residual(r[0], lw["sh_w_gate"], lw["sh_w_up"], lw["sh_w_down"], f_r, row0,
                                             xs, mod, 5, rpb))
            row0 += xs.shape[0]
        x2 = outs[0]
        if need_ctx:
            ctx2 = outs[1]
    return x2.reshape(batch, t_len, d_model).astype(x.dtype)
```

```python
import functools
import math

import jax
import jax.numpy as jnp
from jax import lax
from jax.experimental import pallas as pl
from jax.experimental.pallas import tpu as pltpu

F32 = jnp.float32
BF16 = jnp.bfloat16

GRID_W = 64
NORM_EPS = 1e-6
DA_HEADS = 16
DA_HEAD_DIM = 64
DA_V_DIM = 2 * DA_HEAD_DIM
DA_WIDTH = DA_HEADS * DA_V_DIM
DA_SCALE = DA_HEAD_DIM ** -0.5
DA_Q_SCALE = DA_SCALE * math.log2(math.e)
ROPE_BASE = 10000.0
RW_HEADS = 16
RW_HEAD_DIM = 64
RW_WIDTH = RW_HEADS * RW_HEAD_DIM
RW_DECAY_RANK = 64
RW_A_RANK = 64
RW_GATE_RANK = 128
RW_GN_EPS = 64e-5
RW_STATE_COLS = 2 * RW_WIDTH + 2 * RW_DECAY_RANK + 2 * RW_A_RANK
RW_OUT_COLS = RW_WIDTH + RW_GATE_RANK
S5_GROUP = 16
S5_GROUPS = 64
S5_WIDTH = S5_GROUPS * S5_GROUP
S5_STATE = 64
S5_LANES = S5_GROUPS * S5_STATE
N_EXPERTS = 64
TOP_K = 8
N_GROUPS = 8
TOPK_GROUPS = 4
EXPERT_FF = 256
ROUTED_SCALE = 2.5

LANES = 128
SUBLANES = 8
VMEM_LIMIT_BYTES = 56 * 1024 * 1024

RW_CHUNK = 64
RW_GROUP_HEADS = 4
S5_CHUNK = 64
MOD_CTX_ROW = 8
MOE_TILE = 256


def _cparams(sem):
    return pltpu.CompilerParams(dimension_semantics=sem, vmem_limit_bytes=VMEM_LIMIT_BYTES)


def _tile(n, pref, mult=SUBLANES):
    if n <= pref:
        return n
    t = (pref // mult) * mult
    while t > mult and n % t:
        t -= mult
    assert n % t == 0, (n, pref)
    return t


def _dot(a, b):
    return jnp.dot(a.astype(BF16), b.astype(BF16), preferred_element_type=F32)


def _dot_nt(a, b):
    return lax.dot_general(a.astype(BF16), b.astype(BF16), (((1,), (1,)), ((), ())),
                           preferred_element_type=F32)


def _dot_tn(a, b):
    return lax.dot_general(a.astype(BF16), b.astype(BF16), (((0,), (0,)), ((), ())),
                           preferred_element_type=F32)


def _hilo(x):
    hi = x.astype(BF16)
    lo = (x - hi.astype(F32)).astype(BF16)
    return hi, lo


def _dot_hp_lhs(a, b_exact):
    hi, lo = _hilo(a)
    return (jnp.dot(hi, b_exact, preferred_element_type=F32)
            + jnp.dot(lo, b_exact, preferred_element_type=F32))


def _dot_hp(a, b):
    ah, al = _hilo(a)
    bh, bl = _hilo(b)
    return (jnp.dot(ah, bh, preferred_element_type=F32) + jnp.dot(al, bh, preferred_element_type=F32)
            + jnp.dot(ah, bl, preferred_element_type=F32))


def _sigmoid(x):
    return 1.0 / (1.0 + jnp.exp(-x))


def _softplus(x):
    return jnp.maximum(x, 0.0) + jnp.log(1.0 + jnp.exp(-jnp.abs(x)))


def _gelu_tanh(x):
    c = math.sqrt(2.0 / math.pi)
    return 0.5 * x * (1.0 + jnp.tanh(c * (x + 0.044715 * (x * x * x))))


def _mm_body(a_ref, b_ref, *rest, n_extra, prologue, epilogue):
    extras = rest[:n_extra]
    o_ref = rest[n_extra]
    a = a_ref[...]
    if prologue is not None:
        a = prologue(a)
    acc = jnp.dot(a.astype(BF16), b_ref[...].astype(BF16), preferred_element_type=F32)
    if epilogue is not None:
        acc = epilogue(acc, *[e[...] for e in extras])
    o_ref[...] = acc.astype(o_ref.dtype)


def _matmul(a, b, out_dtype, *, tm=512, tn=1024, prologue=None, epilogue=None, extras=(),
            a_spec=None, out_spec=None, out_shape=None, grid_m=None, name="matmul"):
    k, n = b.shape
    tn = _tile(n, tn, LANES)
    if a_spec is None:
        m = a.shape[0]
        tm = _tile(m, tm)
        grid_m = m // tm
        a_spec = pl.BlockSpec((tm, k), lambda i, j: (i, 0))
    if out_spec is None:
        out_spec = pl.BlockSpec((tm, tn), lambda i, j: (i, j))
        out_shape = (a.shape[0], n)
    body = functools.partial(_mm_body, n_extra=len(extras), prologue=prologue, epilogue=epilogue)
    return pl.pallas_call(
        body,
        grid=(grid_m, n // tn),
        in_specs=[a_spec, pl.BlockSpec((k, tn), lambda i, j: (0, j))] + [s for _, s in extras],
        out_specs=out_spec,
        out_shape=jax.ShapeDtypeStruct(out_shape, out_dtype),
        compiler_params=_cparams(("parallel", "arbitrary")),
        name=name,
    )(a, b, *[x for x, _ in extras])


def _ada_table(cvec, ada_w_l, ada_b_l):
    d6 = ada_w_l.shape[1]

    def prologue(a):
        return a * _sigmoid(a)

    def epilogue(acc, bias):
        return acc + bias

    out = _matmul(cvec, ada_w_l, F32, tm=16, tn=512, prologue=prologue, epilogue=epilogue,
                  extras=[(ada_b_l.reshape(1, d6), pl.BlockSpec((1, 512), lambda i, j: (0, j)))],
                  name="ada_table")
    return out.reshape(cvec.shape[0], 1, d6)


def _mod_row_fn(rows_per_batch, tm):
    if rows_per_batch is None:
        return lambda i: MOD_CTX_ROW
    nb = rows_per_batch // tm
    return lambda i: i // nb


def _modulate_body(x_ref, sh_ref, sc_ref, o_ref):
    x = x_ref[...]
    ms = jnp.mean(x * x, axis=-1, keepdims=True)
    h = x * lax.rsqrt(ms + NORM_EPS) * (1.0 + sc_ref[0]) + sh_ref[0]
    o_ref[...] = h.astype(o_ref.dtype)


def _modulate(x2, mod, shift_k, scale_k, rows_per_batch):
    n, d = x2.shape
    tm = _tile(n if rows_per_batch is None else rows_per_batch, 256)
    row = _mod_row_fn(rows_per_batch, tm)
    return pl.pallas_call(
        _modulate_body,
        grid=(n // tm,),
        in_specs=[pl.BlockSpec((tm, d), lambda i: (i, 0)),
                  pl.BlockSpec((1, 1, d), lambda i: (row(i), 0, shift_k)),
                  pl.BlockSpec((1, 1, d), lambda i: (row(i), 0, scale_k))],
        out_specs=pl.BlockSpec((tm, d), lambda i: (i, 0)),
        out_shape=jax.ShapeDtypeStruct((n, d), BF16),
        compiler_params=_cparams(("parallel",)),
        name="modulate",
    )(x2, mod, mod)


def _pack_halves(x):
    w = x.shape[1] // 2
    lo = lax.bitcast_convert_type(x[:, :w].astype(BF16).astype(F32), jnp.uint32)
    hi = lax.bitcast_convert_type(x[:, w:].astype(BF16).astype(F32), jnp.uint32)
    return lax.shift_right_logical(lo, jnp.uint32(16)) | hi


def _unpack_halves(p):
    lo = lax.bitcast_convert_type(lax.shift_left(p, jnp.uint32(16)), F32)
    hi = lax.bitcast_convert_type(p & jnp.uint32(0xFFFF0000), F32)
    return lo, hi


def _route_body(x_ref, sh_ref, sc_ref, wr_ref, bias_ref, h_ref, hp_ref, comb_ref, cnt_ref):
    x = x_ref[...]
    ms = jnp.mean(x * x, axis=-1, keepdims=True)
    h = x * lax.rsqrt(ms + NORM_EPS) * (1.0 + sc_ref[0]) + sh_ref[0]
    h_ref[...] = h.astype(h_ref.dtype)
    hp_ref[...] = _pack_halves(h)
    tm = x.shape[0]
    wr = wr_ref[...]
    hh, hl = _hilo(h)
    wh, wl = _hilo(wr)
    logits = _dot_nt(wh, hh) + _dot_nt(wl, hh) + _dot_nt(wh, hl)
    scores = _sigmoid(logits[:N_EXPERTS])
    per_group = N_EXPERTS // N_GROUPS
    sc3 = scores.reshape(N_GROUPS, per_group, tm)
    sel = sc3 + bias_ref[...]
    midx = lax.broadcasted_iota(jnp.int32, sel.shape, 1)
    neg = jnp.float32(-jnp.inf)
    m1 = jnp.max(sel, axis=1, keepdims=True)
    first = jnp.min(jnp.where(sel == m1, midx, per_group), axis=1, keepdims=True)
    m2 = jnp.max(jnp.where(midx == first, neg, sel), axis=1, keepdims=True)
    gs = (m1 + m2).reshape(N_GROUPS, tm)
    gidx = lax.broadcasted_iota(jnp.int32, gs.shape, 0)
    gmask = jnp.zeros(gs.shape, jnp.bool_)
    for _ in range(TOPK_GROUPS):
        m = jnp.max(gs, axis=0, keepdims=True)
        f = jnp.min(jnp.where(gs == m, gidx, N_GROUPS), axis=0, keepdims=True)
        pick = gidx == f
        gmask = jnp.logical_or(gmask, pick)
        gs = jnp.where(pick, neg, gs)
    val = jnp.where(gmask.reshape(N_GROUPS, 1, tm), sel, neg)
    eidx = lax.broadcasted_iota(jnp.int32, sel.shape, 0) * per_group + midx
    chosen = jnp.zeros(sel.shape, jnp.bool_)
    for _ in range(TOP_K):
        m = jnp.max(jnp.max(val, axis=1, keepdims=True), axis=0, keepdims=True)
        f = jnp.min(jnp.min(jnp.where(val == m, eidx, N_EXPERTS), axis=1, keepdims=True), axis=0, keepdims=True)
        pick = eidx == f
        chosen = jnp.logical_or(chosen, pick)
        val = jnp.where(pick, neg, val)
    w = jnp.where(chosen, sc3, 0.0)
    wsum = jnp.sum(jnp.sum(w, axis=1, keepdims=True), axis=0, keepdims=True)
    comb = (w / wsum * ROUTED_SCALE).reshape(N_EXPERTS, tm)
    comb_ref[...] = comb

    @pl.when(pl.program_id(0) == 0)
    def _():
        cnt_ref[...] = jnp.zeros_like(cnt_ref)

    cnt_ref[...] += jnp.sum((comb > 0.0).astype(F32), axis=1, keepdims=True)


def _modulate_route(x2, mod, shift_k, scale_k, rows_per_batch, router_w_t, router_bias):
    n, d = x2.shape
    tm = _tile(n if rows_per_batch is None else rows_per_batch, 256, LANES)
    row = _mod_row_fn(rows_per_batch, tm)
    return pl.pallas_call(
        _route_body,
        grid=(n // tm,),
        in_specs=[pl.BlockSpec((tm, d), lambda i: (i, 0)),
                  pl.BlockSpec((1, 1, d), lambda i: (row(i), 0, shift_k)),
                  pl.BlockSpec((1, 1, d), lambda i: (row(i), 0, scale_k)),
                  pl.BlockSpec((LANES, d), lambda i: (0, 0)),
                  pl.BlockSpec((N_GROUPS, N_EXPERTS // N_GROUPS, 1), lambda i: (0, 0, 0))],
        out_specs=[pl.BlockSpec((tm, d), lambda i: (i, 0)),
                   pl.BlockSpec((tm, d // 2), lambda i: (i, 0)),
                   pl.BlockSpec((N_EXPERTS, tm), lambda i: (0, i)),
                   pl.BlockSpec((N_EXPERTS, LANES), lambda i: (0, 0))],
        out_shape=[jax.ShapeDtypeStruct((n, d), BF16), jax.ShapeDtypeStruct((n, d // 2), jnp.uint32),
                   jax.ShapeDtypeStruct((N_EXPERTS, n), F32), jax.ShapeDtypeStruct((N_EXPERTS, LANES), F32)],
        compiler_params=_cparams(("arbitrary",)),
        name="modulate_route",
    )(x2, mod, mod, router_w_t, router_bias.reshape(N_GROUPS, N_EXPERTS // N_GROUPS, 1))


def _group_ones(width, group):
    r = lax.broadcasted_iota(jnp.int32, (width, width), 0) // group
    c = lax.broadcasted_iota(jnp.int32, (width, width), 1) // group
    return (r == c).astype(BF16)


def _qk_norm_rope(x, gain, cos, sin, scale):
    ss = _dot_hp_lhs(x * x, _group_ones(LANES, DA_HEAD_DIM))
    xn = x * lax.rsqrt(ss * (1.0 / DA_HEAD_DIM) + NORM_EPS) * gain
    lane = lax.broadcasted_iota(jnp.int32, x.shape, 1)
    quarter = DA_HEAD_DIM // 4
    partner = jnp.where((lane % (2 * quarter)) < quarter,
                        pltpu.roll(xn, LANES - quarter, axis=1),
                        pltpu.roll(xn, quarter, axis=1))
    return (xn * cos + partner * sin) * scale


def _proj_qk(h, w, gain, cos, sin, scale, t_len, name):
    n = h.shape[0]
    tm = _tile(n if t_len is None else t_len, 512)
    nb = 1 if t_len is None else t_len // tm
    tab = pl.BlockSpec((tm, LANES), lambda i, j: (i % nb, 0))
    gain2 = jnp.tile(gain.reshape(1, DA_HEAD_DIM), (1, 2))

    def epilogue(acc, g, c, s):
        heads = [_qk_norm_rope(acc[:, k * LANES:(k + 1) * LANES], g, c, s, scale)
                 for k in range(acc.shape[1] // LANES)]
        return jnp.concatenate(heads, axis=1)

    return _matmul(h, w, BF16, tm=tm, epilogue=epilogue,
                   extras=[(gain2, pl.BlockSpec((1, LANES), lambda i, j: (0, 0))), (cos, tab), (sin, tab)],
                   name=name)


def _rope_tables(t_len):
    rows = t_len // GRID_W
    row = jnp.repeat(jnp.arange(rows, dtype=F32), GRID_W)
    col = jnp.tile(jnp.arange(GRID_W, dtype=F32), rows)
    half = DA_HEAD_DIM // 2
    inv_freq = 1.0 / (ROPE_BASE ** (jnp.arange(0, half, 2, dtype=F32) / half))
    ang_r = row[:, None] * inv_freq
    ang_c = col[:, None] * inv_freq
    cos64 = jnp.concatenate([jnp.cos(ang_r), jnp.cos(ang_r), jnp.cos(ang_c), jnp.cos(ang_c)], axis=1)
    sin64 = jnp.concatenate([-jnp.sin(ang_r), jnp.sin(ang_r), -jnp.sin(ang_c), jnp.sin(ang_c)], axis=1)
    return jnp.tile(cos64, (1, 2)), jnp.tile(sin64, (1, 2))


def _attn_body(*refs, has_lat, lam_init):
    if has_lat:
        q_ref, kl_ref, vl_ref, kc_ref, vc_ref, lq_ref, lk_ref, sub_ref, o_ref = refs
    else:
        q_ref, kc_ref, vc_ref, lq_ref, lk_ref, sub_ref, o_ref = refs
    lqk = lq_ref[...] * lk_ref[...]
    lsum = jnp.sum(lqk, axis=1, keepdims=True)
    e = jnp.exp(lsum)
    lam = e[0:1, :] - e[1:2, :] + lam_init
    q = q_ref[...]
    lane = lax.broadcasted_iota(jnp.int32, q.shape, 1)
    zero = jnp.zeros_like(q)
    ext = lambda v: jnp.concatenate([v, jnp.ones_like(v)], axis=1)
    vc_ext = ext(vc_ref[...])
    vl_ext = ext(vl_ref[...]) if has_lat else None
    mixed = []
    for m in range(2):
        in_map = (lane // DA_HEAD_DIM) == m
        qm = jnp.where(in_map, q, zero)
        s_c = _dot_nt(qm, kc_ref[...]).astype(BF16)
        mx = jnp.max(s_c, axis=-1, keepdims=True)
        if has_lat:
            s_l = _dot_nt(qm, kl_ref[...]).astype(BF16)
            mx = jnp.maximum(mx, jnp.max(s_l, axis=-1, keepdims=True))
        acc = jnp.dot(jnp.exp2(s_c - mx), vc_ext, preferred_element_type=F32)
        if has_lat:
            acc = acc + jnp.dot(jnp.exp2(s_l - mx), vl_ext, preferred_element_type=F32)
        mixed.append(acc[:, :DA_V_DIM] / acc[:, DA_V_DIM:])
    o = mixed[0] - lam * mixed[1]
    ms = jnp.mean(o * o, axis=-1, keepdims=True)
    o = o * lax.rsqrt(ms + NORM_EPS) * sub_ref[...] * (1.0 - lam_init)
    o_ref[...] = o.astype(o_ref.dtype)


def _diff_attention(q, k_lat, v_lat, k_ctx, v_ctx, lq, lk, subln, lam_init, batch):
    n, w = q.shape
    tq_len = n // batch
    c_len = k_ctx.shape[0] // batch
    tq = _tile(tq_len, 1024)
    nq = tq_len // tq
    has_lat = k_lat is not None
    blk = lambda rows: pl.BlockSpec((rows, LANES), lambda b, h, i: (b, h))
    in_specs = [pl.BlockSpec((tq, LANES), lambda b, h, i: (b * nq + i, h))]
    args = [q]
    if has_lat:
        t_len = k_lat.shape[0] // batch
        in_specs += [blk(t_len), blk(t_len)]
        args += [k_lat, v_lat]
    in_specs += [blk(c_len), blk(c_len),
                 pl.BlockSpec((2, DA_HEAD_DIM), lambda b, h, i: (0, 0)),
                 pl.BlockSpec((2, DA_HEAD_DIM), lambda b, h, i: (0, 0)),
                 pl.BlockSpec((1, LANES), lambda b, h, i: (0, 0))]
    args += [k_ctx, v_ctx, lq, lk, subln.reshape(1, DA_V_DIM)]
    return pl.pallas_call(
        functools.partial(_attn_body, has_lat=has_lat, lam_init=lam_init),
        grid=(batch, w // LANES, nq),
        in_specs=in_specs,
        out_specs=pl.BlockSpec((tq, LANES), lambda b, h, i: (b * nq + i, h)),
        out_shape=jax.ShapeDtypeStruct((n, w), BF16),
        compiler_params=_cparams(("parallel", "parallel", "arbitrary")),
        name="diff_attention",
    )(*args)


def _rwkv_prep_body(x_ref, xp_ref, xn_ref, conv_ref, lora_ref, w0a0_ref, kk_w_ref, ka_w_ref, ones_ref,
                    lwf_ref, lwb_ref, kk_ref, kkaf_ref, kkab_ref, kdf_ref, kdb_ref, v_ref, r_ref, gd_ref,
                    *, blocks_per_seq):
    i = pl.program_id(0)
    x = x_ref[...]
    tm = x.shape[0]
    first = (i % blocks_per_seq) == 0
    last = (i % blocks_per_seq) == blocks_per_seq - 1
    xp = jnp.where(first, 0.0, xp_ref[SUBLANES - 1:SUBLANES, :])
    xn = jnp.where(last, 0.0, xn_ref[0:1, :])
    row = lax.broadcasted_iota(jnp.int32, (tm, 1), 0)
    up = jnp.where(row == 0, xp, pltpu.roll(x, 1, axis=0))
    dn = jnp.where(row == tm - 1, xn, pltpu.roll(x, tm - 1, axis=0))
    cw = conv_ref[...]
    cv = up * cw[0:1, :] + x * cw[1:2, :] + dn * cw[2:3, :]
    w = RW_WIDTH
    k = cv[:, :w]
    v = cv[:, w:2 * w]
    lora_in = cv[:, 2 * w:RW_STATE_COLS]
    r = cv[:, RW_STATE_COLS:RW_STATE_COLS + w]
    gd = cv[:, RW_STATE_COLS + w:]
    lane = lax.broadcasted_iota(jnp.int32, lora_in.shape, 1)
    li = jnp.where(lane < 2 * RW_DECAY_RANK, jnp.tanh(lora_in), lora_in)
    li_hi, li_lo = _hilo(li)
    pre = (jnp.dot(li_hi, lora_ref[0], preferred_element_type=F32)
           + jnp.dot(li_lo, lora_ref[0], preferred_element_type=F32)
           + jnp.dot(li_hi, lora_ref[1], preferred_element_type=F32)) + w0a0_ref[...]
    kkr = k * kk_w_ref[...]
    ss = _dot_hp_lhs(kkr * kkr, ones_ref[...])
    kk = kkr * lax.rsqrt(ss + 1e-12)
    kk_ref[...] = kk
    v_ref[...] = v
    r_ref[...] = r
    gd_ref[...] = gd
    ka = ka_w_ref[...]
    for d, (lw_ref, kka_ref, kd_ref) in enumerate(((lwf_ref, kkaf_ref, kdf_ref), (lwb_ref, kkab_ref, kdb_ref))):
        lw_ref[...] = -math.exp(-0.5) * _sigmoid(pre[:, d * w:(d + 1) * w])
        a = _sigmoid(pre[:, (2 + d) * w:(3 + d) * w])
        kka_ref[...] = kk * a
        kd_ref[...] = k * (1.0 + (a - 1.0) * ka)


def _rwkv_prep(p_rw, seq_len, conv, lora_w, w0a0, k_k, k_a):
    n, c = p_rw.shape
    tm = _tile(seq_len, 128)
    bps = seq_len // tm
    sub = tm // SUBLANES
    nsub = n // SUBLANES
    w = RW_WIDTH
    wide = lambda: pl.BlockSpec((tm, w), lambda i: (i, 0))
    outs = [jax.ShapeDtypeStruct((n, w), F32)] * 9 + [jax.ShapeDtypeStruct((n, RW_GATE_RANK), F32)]
    return pl.pallas_call(
        functools.partial(_rwkv_prep_body, blocks_per_seq=bps),
        grid=(n // tm,),
        in_specs=[pl.BlockSpec((tm, c), lambda i: (i, 0)),
                  pl.BlockSpec((SUBLANES, c), lambda i: (jnp.maximum(i * sub - 1, 0), 0)),
                  pl.BlockSpec((SUBLANES, c), lambda i: (jnp.minimum((i + 1) * sub, nsub - 1), 0)),
                  pl.BlockSpec((3, c), lambda i: (0, 0)),
                  pl.BlockSpec(lora_w.shape, lambda i: (0, 0, 0)),
                  pl.BlockSpec((1, 4 * w), lambda i: (0, 0)),
                  pl.BlockSpec((1, w), lambda i: (0, 0)),
                  pl.BlockSpec((1, w), lambda i: (0, 0)),
                  pl.BlockSpec((w, w), lambda i: (0, 0))],
        out_specs=[wide() for _ in range(9)] + [pl.BlockSpec((tm, RW_GATE_RANK), lambda i: (i, 0))],
        out_shape=outs,
        compiler_params=_cparams(("parallel",)),
        name="rwkv_prep",
    )(p_rw, p_rw, p_rw, conv, lora_w, w0a0, k_k.reshape(1, w), k_a.reshape(1, w), _group_ones(w, RW_HEAD_DIM))


def _rwkv_scan_body(lwf_ref, kkaf_ref, kdf_ref, kkf_ref, vf_ref, rf_ref,
                    lwb_ref, kkab_ref, kdb_ref, kkb_ref, vb_ref, rb_ref, s0f_ref, s0b_ref,
                    of_ref, ob_ref, stf_ref, stb_ref, *, want_out):
    ci = pl.program_id(1)

    @pl.when(ci == 0)
    def _():
        stf_ref[...] = s0f_ref[...]
        stb_ref[...] = s0b_ref[...]

    n_l, width = lwf_ref.shape
    hd = RW_HEAD_DIM
    gw = RW_GROUP_HEADS * hd
    n_groups = width // gw
    row = lax.broadcasted_iota(jnp.int32, (n_l, n_l), 0)
    col = lax.broadcasted_iota(jnp.int32, (n_l, n_l), 1)
    trow = lax.broadcasted_iota(jnp.int32, (n_l, gw), 0)
    tcol = lax.broadcasted_iota(jnp.int32, (n_l, gw), 1) % hd
    eye = (tcol == trow).astype(F32)
    same_head = ((lax.broadcasted_iota(jnp.int32, (gw, gw), 0) // hd)
                 == (lax.broadcasted_iota(jnp.int32, (gw, gw), 1) // hd))
    reps = gw // n_l

    def bdiag(x):
        xb = x.astype(BF16)
        return jnp.where(same_head, jnp.concatenate([xb] * reps, axis=0), jnp.zeros((), BF16))

    units = []
    for reverse, refs in ((False, (lwf_ref, kkaf_ref, kdf_ref, kkf_ref, vf_ref, rf_ref, stf_ref, of_ref)),
                          (True, (lwb_ref, kkab_ref, kdb_ref, kkb_ref, vb_ref, rb_ref, stb_ref, ob_ref))):
        lw_ref, kka_ref, k_ref, kk_ref, v_ref, r_ref, st_ref, o_ref = refs
        lw = lw_ref[...]
        tri = ((col >= row) if reverse else (col <= row)).astype(BF16)
        lh, ll = _hilo(lw)
        c = jnp.dot(tri, lh, preferred_element_type=F32) + jnp.dot(tri, ll, preferred_element_type=F32)
        g_end = jnp.exp(c[0:1, :] if reverse else c[n_l - 1:n_l, :])
        e_inv = jnp.exp(-c)
        a_t = kk_ref[...] * jnp.exp(c - lw)
        r_t = r_ref[...] * jnp.exp(c)
        k_h = k_ref[...] * e_inv
        b_h = kka_ref[...] * e_inv
        v_all = v_ref[...]
        incl, strict = (tcol >= trow, tcol > trow) if reverse else (tcol <= trow, tcol < trow)
        for g in range(n_groups):
            s = slice(g * gw, (g + 1) * gw)
            units.append(dict(g=g, sl=s, incl=incl, strict=strict, st_ref=st_ref, o_ref=o_ref, state=st_ref[g],
                              ar=jnp.concatenate([a_t[:, s], r_t[:, s]], axis=0).astype(BF16),
                              k_h=k_h[:, s], b_h=b_h[:, s], v=v_all[:, s], g_end=g_end[:, s]))

    n_iter = max(1, (n_l - 1).bit_length()) - 1
    sc_k = [_dot_nt(x["ar"], bdiag(x["k_h"])) for x in units]
    sc_b = [_dot_nt(x["ar"], bdiag(x["b_h"])) for x in units]
    from_s = [_dot_nt(x["ar"], x["state"]) for x in units]
    p = [-jnp.where(x["strict"], sb[:n_l], 0.0) for x, sb in zip(units, sc_b)]
    t_inv = [eye + pi for pi in p]
    if n_iter:
        p = [_dot(pi, bdiag(pi)) for pi in p]
    for it in range(n_iter):
        if it + 1 < n_iter:
            y = [_dot(jnp.concatenate([ti, pi], axis=0), bdiag(pi)) for ti, pi in zip(t_inv, p)]
            t_inv = [ti + yi[:n_l] for ti, yi in zip(t_inv, y)]
            p = [yi[n_l:] for yi in y]
        else:
            t_inv = [ti + _dot(ti, bdiag(pi)) for ti, pi in zip(t_inv, p)]
    v_bd = [bdiag(x["v"]) for x in units]
    w = [fs[:n_l] + _dot(jnp.where(x["strict"], sk[:n_l], 0.0), vb)
         for x, fs, sk, vb in zip(units, from_s, sc_k, v_bd)]
    u = [_dot(ti, bdiag(wi)) for ti, wi in zip(t_inv, w)]
    if want_out:
        o_v = [_dot(jnp.where(x["incl"], sk[n_l:], 0.0), vb) for x, sk, vb in zip(units, sc_k, v_bd)]
        o_u = [_dot(jnp.where(x["incl"], sb[n_l:], 0.0), bdiag(ui)) for x, sb, ui in zip(units, sc_b, u)]
        for x, fs, ov, ou in zip(units, from_s, o_v, o_u):
            x["o_ref"][:, x["sl"]] = fs[n_l:] + ov - ou
    else:
        of_ref[...] = jnp.zeros_like(of_ref)
        ob_ref[...] = jnp.zeros_like(ob_ref)
    upd_k = [_dot_tn(x["v"], x["k_h"] * x["g_end"]) for x in units]
    upd_b = [_dot_tn(ui, x["b_h"] * x["g_end"]) for x, ui in zip(units, u)]
    for x, uk, ub in zip(units, upd_k, upd_b):
        x["st_ref"][x["g"]] = jnp.where(same_head, x["state"] * x["g_end"] + uk - ub, 0.0)


def _rwkv_scan(t, s0_f, s0_b, batch, *, want_out=True):
    n, w = t["kk"].shape
    seq = n // batch
    n_l = _tile(seq, RW_CHUNK)
    nch = seq // n_l
    gw = RW_GROUP_HEADS * RW_HEAD_DIM
    assert gw % n_l == 0 and w % gw == 0
    n_groups = w // gw
    fwd = pl.BlockSpec((n_l, w), lambda b, c: (b * nch + c, 0))
    rev = pl.BlockSpec((n_l, w), lambda b, c: (b * nch + nch - 1 - c, 0))
    st_spec = pl.BlockSpec((n_groups, gw, gw), lambda b, c: (b, 0, 0))
    seq_shape = jax.ShapeDtypeStruct((n, w), F32)
    st_shape = jax.ShapeDtypeStruct(s0_f.shape, F32)
    return pl.pallas_call(
        functools.partial(_rwkv_scan_body, want_out=want_out),
        grid=(batch, nch),
        in_specs=[fwd] * 6 + [rev] * 6 + [st_spec, st_spec],
        out_specs=[fwd, rev, st_spec, st_spec],
        out_shape=[seq_shape, seq_shape, st_shape, st_shape],
        compiler_params=_cparams(("parallel", "arbitrary")),
        name="rwkv_scan",
    )(t["lw_f"], t["kka_f"], t["kd_f"], t["kk"], t["v"], t["r"],
      t["lw_b"], t["kka_b"], t["kd_b"], t["kk"], t["v"], t["r"], s0_f, s0_b)


def _rwkv_readout_body(of_ref, ob_ref, r_ref, kdf_ref, kdb_ref, v_ref, gd_ref, gnw_ref, gnb_ref, rk_ref, g2_ref,
                       ones_ref, y_ref):
    ones = ones_ref[...]
    inv = 1.0 / RW_HEAD_DIM
    o = of_ref[...] + ob_ref[...]
    mu = _dot_hp_lhs(o, ones) * inv
    d = o - mu
    var = _dot_hp_lhs(d * d, ones) * inv
    on = d * lax.rsqrt(var + RW_GN_EPS) * gnw_ref[...] + gnb_ref[...]
    r = r_ref[...]
    rk = rk_ref[...]
    bonus = _dot_hp_lhs(r * kdf_ref[...] * rk, ones) + _dot_hp_lhs(r * kdb_ref[...] * rk, ones)
    y = on + bonus * v_ref[...]
    g = _dot(_sigmoid(gd_ref[...]), g2_ref[...])
    y_ref[...] = (y * g).astype(y_ref.dtype)


def _rwkv_readout(o_f, o_b, r, kd_f, kd_b, v, gd, gn_w, gn_b, r_k, g2):
    n, w = o_f.shape
    tm = _tile(n, 256)
    wide = pl.BlockSpec((tm, w), lambda i: (i, 0))
    vec = pl.BlockSpec((1, w), lambda i: (0, 0))
    return pl.pallas_call(
        _rwkv_readout_body,
        grid=(n // tm,),
        in_specs=[wide] * 6 + [pl.BlockSpec((tm, RW_GATE_RANK), lambda i: (i, 0)), vec, vec, vec,
                               pl.BlockSpec((RW_GATE_RANK, w), lambda i: (0, 0)),
                               pl.BlockSpec((w, w), lambda i: (0, 0))],
        out_specs=wide,
        out_shape=jax.ShapeDtypeStruct((n, w), BF16),
        compiler_params=_cparams(("parallel",)),
        name="rwkv_readout",
    )(o_f, o_b, r, kd_f, kd_b, v, gd, gn_w.reshape(1, w), gn_b.reshape(1, w), r_k.reshape(1, w), g2,
      _group_ones(w, RW_HEAD_DIM))


def _s5_scan_body(u_ref, bre_ref, bim_ref, are_ref, aim_ref, cre_ref, cim_ref, h0_ref, y_ref, ht_ref,
                  dre, dim, *, reverse, want_out):
    ci = pl.program_id(0)

    @pl.when(ci == 0)
    def _():
        ht_ref[...] = h0_ref[...]

    tt, nb, wu = u_ref.shape
    nblk = wu // LANES
    sw = S5_LANES // nblk
    u2 = u_ref[...].reshape(tt * nb, wu).astype(BF16)
    for c in range(nblk):
        uc = u2[:, c * LANES:(c + 1) * LANES]
        dre[:, :, c * sw:(c + 1) * sw] = jnp.dot(uc, bre_ref[c], preferred_element_type=F32).reshape(tt, nb, sw)
        dim[:, :, c * sw:(c + 1) * sw] = jnp.dot(uc, bim_ref[c], preferred_element_type=F32).reshape(tt, nb, sw)
    lw = 1024
    for c in range(S5_LANES // lw):
        ls = slice(c * lw, (c + 1) * lw)
        ar = jnp.broadcast_to(are_ref[:, ls], (nb, lw))
        ai = jnp.broadcast_to(aim_ref[:, ls], (nb, lw))

        def step(s, carry, ls=ls, ar=ar, ai=ai):
            t = (tt - 1 - s) if reverse else s
            hr, hi = carry
            nr = ar * hr - ai * hi + dre[t, :, ls]
            ni = ar * hi + ai * hr + dim[t, :, ls]
            dre[t, :, ls] = nr
            dim[t, :, ls] = ni
            return nr, ni

        hr, hi = lax.fori_loop(0, tt, step, (ht_ref[0, :, ls], ht_ref[1, :, ls]), unroll=2)
        ht_ref[0, :, ls] = hr
        ht_ref[1, :, ls] = hi
    if want_out:
        xr = dre[...].reshape(tt * nb, S5_LANES).astype(BF16)
        xi = dim[...].reshape(tt * nb, S5_LANES).astype(BF16)
        for c in range(nblk):
            yc = (jnp.dot(xr[:, c * sw:(c + 1) * sw], cre_ref[c], preferred_element_type=F32)
                  - jnp.dot(xi[:, c * sw:(c + 1) * sw], cim_ref[c], preferred_element_type=F32))
            y_ref[:, :, c * LANES:(c + 1) * LANES] = yc.reshape(tt, nb, LANES)
    else:
        y_ref[...] = jnp.zeros_like(y_ref)


def _s5_scan(u_tm, h0, p, *, reverse, want_out=True):
    t_len, nb, wu = u_tm.shape
    tt = _tile(t_len, S5_CHUNK)
    nch = t_len // tt
    chunk = (lambda c: (nch - 1 - c, 0, 0)) if reverse else (lambda c: (c, 0, 0))
    const3 = lambda a: pl.BlockSpec(a.shape, lambda c: (0, 0, 0))
    const2 = lambda a: pl.BlockSpec(a.shape, lambda c: (0, 0))
    return pl.pallas_call(
        functools.partial(_s5_scan_body, reverse=reverse, want_out=want_out),
        grid=(nch,),
        in_specs=[pl.BlockSpec((tt, nb, wu), chunk), const3(p["b_re"]), const3(p["b_im"]),
                  const2(p["a_re"]), const2(p["a_im"]), const3(p["c_re"]), const3(p["c_im"]), const3(h0)],
        out_specs=[pl.BlockSpec((tt, nb, wu), chunk), const3(h0)],
        out_shape=[jax.ShapeDtypeStruct(u_tm.shape, F32), jax.ShapeDtypeStruct(h0.shape, F32)],
        scratch_shapes=[pltpu.VMEM((tt, nb, S5_LANES), F32), pltpu.VMEM((tt, nb, S5_LANES), F32)],
        compiler_params=_cparams(("arbitrary",)),
        name="s5_scan_rev" if reverse else "s5_scan_fwd",
    )(u_tm, p["b_re"], p["b_im"], p["a_re"], p["a_im"], p["c_re"], p["c_im"], h0)


def _s5_dir_params(lam_re, lam_im, log_dt, b_re, b_im, c_re, c_im):
    g, pdim = lam_re.shape
    dt = jnp.exp(log_dt.astype(F32))[:, None]
    mag = jnp.exp(lam_re * dt)
    abar_re = mag * jnp.cos(lam_im * dt)
    abar_im = mag * jnp.sin(lam_im * dt)
    den = lam_re * lam_re + lam_im * lam_im
    nr = abar_re - 1.0
    g_re = (nr * lam_re + abar_im * lam_im) / den
    g_im = (abar_im * lam_re - nr * lam_im) / den
    bb_re = g_re[:, :, None] * b_re - g_im[:, :, None] * b_im
    bb_im = g_re[:, :, None] * b_im + g_im[:, :, None] * b_re
    gpb = LANES // S5_GROUP
    nblk = g // gpb
    eye = jnp.eye(gpb, dtype=F32)

    def drive_mat(bb):
        x = bb.reshape(nblk, gpb, pdim, S5_GROUP)
        x = jnp.einsum("cgph,gk->cghkp", x, eye)
        return x.reshape(nblk, gpb * S5_GROUP, gpb * pdim).astype(BF16)

    def read_mat(cc):
        x = cc.reshape(nblk, gpb, S5_GROUP, pdim)
        x = jnp.einsum("cghp,gk->cgpkh", x, eye)
        return x.reshape(nblk, gpb * pdim, gpb * S5_GROUP).astype(BF16)

    return {"a_re": abar_re.reshape(1, g * pdim), "a_im": abar_im.reshape(1, g * pdim),
            "b_re": drive_mat(bb_re), "b_im": drive_mat(bb_im),
            "c_re": read_mat(c_re.astype(F32)), "c_im": read_mat(c_im.astype(F32))}


def _s5_glu_body(u_ref, yf_ref, yb_ref, d_ref, w_ref, b_ref, o_ref):
    y = u_ref[...] * d_ref[...] + yf_ref[...] + yb_ref[...]
    y = _gelu_tanh(y)
    z = _dot(y, w_ref[...]) + b_ref[...]
    o_ref[...] = (y * _sigmoid(z)).astype(o_ref.dtype)


def _s5_glu(u, y_f, y_b, d_skip, glu_w, glu_b):
    n, w = u.shape
    tm = _tile(n, 512)
    wide = pl.BlockSpec((tm, w), lambda i: (i, 0))
    vec = pl.BlockSpec((1, w), lambda i: (0, 0))
    return pl.pallas_call(
        _s5_glu_body,
        grid=(n // tm,),
        in_specs=[wide, wide, wide, vec, pl.BlockSpec((w, w), lambda i: (0, 0)), vec],
        out_specs=wide,
        out_shape=jax.ShapeDtypeStruct((n, w), BF16),
        compiler_params=_cparams(("parallel",)),
        name="s5_glu",
    )(u, y_f, y_b, d_skip.reshape(1, w), glu_w, glu_b.reshape(1, w))


def _merge_body(ya_ref, yr_ref, ys_ref, wa_ref, wr_ref, ws_ref, ga_ref, gr_ref, gs_ref, o_ref):
    m = (ga_ref[...].astype(F32) * jnp.dot(ya_ref[...], wa_ref[...], preferred_element_type=F32)
         + gr_ref[...].astype(F32) * jnp.dot(yr_ref[...], wr_ref[...], preferred_element_type=F32)
         + gs_ref[...].astype(F32) * jnp.dot(ys_ref[...], ws_ref[...], preferred_element_type=F32))
    o_ref[...] = m.astype(o_ref.dtype)


def _merge(y_a, y_r, y_s_tm, gates, w_a, w_r, w_s, seq_len, batch):
    n = y_a.shape[0]
    d = w_a.shape[1]
    tm = _tile(seq_len, 512)
    tn = _tile(d, 1024, LANES)
    nt = seq_len // tm
    nd = d // tn
    row = lambda i, j: (i, 0)
    return pl.pallas_call(
        _merge_body,
        grid=(n // tm, nd),
        in_specs=[pl.BlockSpec((tm, y_a.shape[1]), row),
                  pl.BlockSpec((tm, y_r.shape[1]), row),
                  pl.BlockSpec((tm, S5_WIDTH), lambda i, j: (i % nt, i // nt)),
                  pl.BlockSpec((w_a.shape[0], tn), lambda i, j: (0, j)),
                  pl.BlockSpec((w_r.shape[0], tn), lambda i, j: (0, j)),
                  pl.BlockSpec((w_s.shape[0], tn), lambda i, j: (0, j)),
                  pl.BlockSpec((tm, tn), lambda i, j: (i, j)),
                  pl.BlockSpec((tm, tn), lambda i, j: (i, nd + j)),
                  pl.BlockSpec((tm, tn), lambda i, j: (i, 2 * nd + j))],
        out_specs=pl.BlockSpec((tm, tn), lambda i, j: (i, j)),
        out_shape=jax.ShapeDtypeStruct((n, d), BF16),
        compiler_params=_cparams(("parallel", "arbitrary")),
        name="merge",
    )(y_a, y_r, y_s_tm, w_a, w_r, w_s, gates, gates, gates)


def _moe_plan_body(comb_ref, cnt_ref, pos_ref, w_ref, te_ref, carry_ref, *, tile_rows, dummy_row):
    i = pl.program_id(0)
    ne, tm = comb_ref.shape
    tiles = jnp.floor((cnt_ref[...] + (tile_rows - 1)) * (1.0 / tile_rows))
    er = lax.broadcasted_iota(jnp.int32, (ne, ne), 0)
    ec = lax.broadcasted_iota(jnp.int32, (ne, ne), 1)
    t_hi, t_lo = _hilo(tiles)
    lower = (ec < er).astype(BF16)
    off_tiles = (jnp.dot(lower, t_hi, preferred_element_type=F32)
                 + jnp.dot(lower, t_lo, preferred_element_type=F32))

    @pl.when(i == 0)
    def _():
        carry_ref[...] = jnp.zeros_like(carry_ref)
        ntp = te_ref.shape[1]
        end_tiles = (off_tiles + tiles)[:, 0:1]
        tile_idx = lax.broadcasted_iota(jnp.int32, (1, ntp), 1).astype(F32)
        expert = jnp.sum((end_tiles <= tile_idx).astype(F32), axis=0, keepdims=True)
        expert = jnp.minimum(expert, ne - 1.0)
        valid = (tile_idx < jnp.max(end_tiles, axis=0, keepdims=True)).astype(F32)
        cnt = cnt_ref[:, 0:1]
        on_lane = lax.broadcasted_iota(jnp.int32, (ne, ntp), 0) == lax.broadcasted_iota(jnp.int32, (ne, ntp), 1)
        to_lanes = lambda col: jnp.sum(jnp.where(on_lane, col, 0.0), axis=0, keepdims=True)
        pad_start = to_lanes(off_tiles[:, 0:1] * tile_rows + cnt)
        pad_len = to_lanes(tiles[:, 0:1] * tile_rows - cnt)
        r8 = lax.broadcasted_iota(jnp.int32, te_ref.shape, 0)
        rows = jnp.where(r8 == 0, expert, jnp.where(r8 == 1, valid, jnp.where(r8 == 2, pad_start,
                                                                               jnp.where(r8 == 3, pad_len, 0.0))))
        te_ref[...] = rows.astype(jnp.int32)

    comb = comb_ref[...]
    chosen = comb > 0.0
    chf = chosen.astype(BF16)
    tr = lax.broadcasted_iota(jnp.int32, (tm, tm), 0)
    tc = lax.broadcasted_iota(jnp.int32, (tm, tm), 1)
    rank = jnp.dot(chf, (tr < tc).astype(BF16), preferred_element_type=F32)
    pos = off_tiles[:, 0:1] * tile_rows + carry_ref[:, 0:1] + rank
    carry_ref[...] += jnp.sum(chosen.astype(F32), axis=1, keepdims=True)
    eidx = lax.broadcasted_iota(jnp.int32, (ne, tm), 0)
    tok = lax.broadcasted_iota(jnp.int32, (1, tm), 1).astype(F32)
    remaining = chosen
    pos_rows, w_rows = [], []
    for k in range(TOP_K):
        first = jnp.min(jnp.where(remaining, eidx, ne), axis=0, keepdims=True)
        pick = eidx == first
        pos_k = jnp.sum(jnp.where(pick, pos, 0.0), axis=0, keepdims=True)
        pos_rows.append(jnp.where(first < ne, pos_k, float(dummy_row + k * tm) + tok))
        w_rows.append(jnp.sum(jnp.where(pick, comb, 0.0), axis=0, keepdims=True))
        remaining = jnp.logical_and(remaining, jnp.logical_not(pick))
    pos_ref[...] = jnp.concatenate(pos_rows, axis=0).astype(jnp.int32)
    wmat = jnp.concatenate(w_rows + [jnp.zeros((LANES - TOP_K, tm), F32)], axis=0)
    w_ref[...] = wmat.T


def _moe_token_tile(n):
    return _tile(n, 256, LANES)


def _moe_plan(comb_t, counts, n_tiles, dummy_row):
    ne, n = comb_t.shape
    tm = _moe_token_tile(n)
    ntp = -(-n_tiles // LANES) * LANES
    return pl.pallas_call(
        functools.partial(_moe_plan_body, tile_rows=MOE_TILE, dummy_row=dummy_row),
        grid=(n // tm,),
        in_specs=[pl.BlockSpec((ne, tm), lambda i: (0, i)),
                  pl.BlockSpec((ne, LANES), lambda i: (0, 0))],
        out_specs=[pl.BlockSpec((TOP_K, tm), lambda i: (0, i)),
                   pl.BlockSpec((tm, LANES), lambda i: (i, 0)),
                   pl.BlockSpec((SUBLANES, ntp), lambda i: (0, 0))],
        out_shape=[jax.ShapeDtypeStruct((TOP_K, n), jnp.int32), jax.ShapeDtypeStruct((n, LANES), F32),
                   jax.ShapeDtypeStruct((SUBLANES, ntp), jnp.int32)],
        scratch_shapes=[pltpu.VMEM((ne, LANES), F32)],
        compiler_params=_cparams(("arbitrary",)),
        name="moe_plan",
    )(comb_t, counts)


def _start_row_copies(pos_ref, tm, make_copy):
    def start(t, c):
        for k in range(TOP_K):
            make_copy(t, k, pos_ref[k, t]).start(priority=k % 2)
        return c

    lax.fori_loop(0, tm, start, 0)


def _moe_dispatch_body(pos_ref, plan_ref, hp_ref, xg_hbm, zrow, sem, zsem, *, tm, experts_per_step, n_experts):
    _start_row_copies(pos_ref, tm, lambda t, k, p: pltpu.make_async_copy(
        hp_ref.at[pl.ds(t, 1)], xg_hbm.at[pl.ds(p, 1)], sem))
    zrow[...] = jnp.zeros_like(zrow)
    zero_copy = lambda row: pltpu.make_async_copy(zrow.at[pl.ds(0, 1)], xg_hbm.at[pl.ds(row, 1)], zsem)
    for j in range(experts_per_step):
        e = pl.program_id(0) * experts_per_step + j

        @pl.when(e < n_experts)
        def _():
            start = plan_ref[2, e]
            count = plan_ref[3, e]
            lax.fori_loop(0, count, lambda r, c: (zero_copy(start + r).start(), c)[1], 0)
            lax.fori_loop(0, count, lambda r, c: (zero_copy(start + r).wait(), c)[1], 0)

    for _ in range(TOP_K):
        pltpu.make_async_copy(hp_ref, hp_ref, sem).wait()


def _moe_dispatch(pos, plan, hp, total_rows, n_experts):
    n, half = hp.shape
    tm = _moe_token_tile(n)
    steps = n // tm
    return pl.pallas_call(
        functools.partial(_moe_dispatch_body, tm=tm, experts_per_step=-(-n_experts // steps), n_experts=n_experts),
        grid=(steps,),
        in_specs=[pl.BlockSpec((TOP_K, tm), lambda i: (0, i), memory_space=pltpu.SMEM),
                  pl.BlockSpec(memory_space=pltpu.SMEM),
                  pl.BlockSpec((tm, half), lambda i: (i, 0))],
        out_specs=pl.BlockSpec(memory_space=pl.ANY),
        out_shape=jax.ShapeDtypeStruct((total_rows, half), jnp.uint32),
        scratch_shapes=[pltpu.VMEM((SUBLANES, half), jnp.uint32), pltpu.SemaphoreType.DMA, pltpu.SemaphoreType.DMA],
        compiler_params=_cparams(("arbitrary",)),
        name="moe_dispatch",
    )(pos, plan, hp)


def _moe_ffn_body(te_ref, tv_ref, xg_ref, wg_ref, wu_ref, wd_ref, ys_ref, wg_bf, wu_bf, wd_bf):
    j = pl.program_id(0)
    new_expert = jnp.logical_or(j == 0, te_ref[j] != te_ref[jnp.maximum(j - 1, 0)])

    @pl.when(new_expert)
    def _():
        wg_bf[...] = wg_ref[0, 0].astype(BF16)
        wu_bf[...] = wu_ref[0, 0].astype(BF16)
        wd_bf[...] = wd_ref[0, 0].astype(BF16)

    @pl.when(tv_ref[j] != 0)
    def _():
        lo, hi = _unpack_halves(xg_ref[...])
        half = lo.shape[1]
        hg = _dot(lo, wg_bf[:half, :]) + _dot(hi, wg_bf[half:, :])
        hu = _dot(lo, wu_bf[:half, :]) + _dot(hi, wu_bf[half:, :])
        act = hg * _sigmoid(hg) * hu
        ys_ref[...] = _pack_halves(_dot(act, wd_bf[...]))

    @pl.when(tv_ref[j] == 0)
    def _():
        ys_ref[...] = jnp.zeros_like(ys_ref)


def _moe_ffn(tile_expert, tile_valid, xg, w_gate, w_up, w_down, layer):
    rows, half = xg.shape
    _, ne, d, ff = w_gate.shape
    blk = pl.BlockSpec((MOE_TILE, half), lambda j, te, tv: (j, 0))
    return pl.pallas_call(
        _moe_ffn_body,
        grid_spec=pltpu.PrefetchScalarGridSpec(
            num_scalar_prefetch=2,
            grid=(rows // MOE_TILE,),
            in_specs=[blk,
                      pl.BlockSpec((1, 1, d, ff), lambda j, te, tv: (layer, te[j], 0, 0)),
                      pl.BlockSpec((1, 1, d, ff), lambda j, te, tv: (layer, te[j], 0, 0)),
                      pl.BlockSpec((1, 1, ff, d), lambda j, te, tv: (layer, te[j], 0, 0))],
            out_specs=blk,
            scratch_shapes=[pltpu.VMEM((d, ff), BF16), pltpu.VMEM((d, ff), BF16), pltpu.VMEM((ff, d), BF16)]),
        out_shape=jax.ShapeDtypeStruct((rows, half), jnp.uint32),
        compiler_params=_cparams(("arbitrary",)),
        name="moe_ffn",
    )(tile_expert, tile_valid, xg, w_gate, w_up, w_down)


def _moe_combine_body(pos_ref, w_ref, ys_hbm, o_ref, buf, sem, *, tm):
    _start_row_copies(pos_ref, tm, lambda t, k, p: pltpu.make_async_copy(
        ys_hbm.at[pl.ds(p, 1)], buf.at[k, pl.ds(t, 1)], sem))
    pltpu.make_async_copy(buf, buf, sem).wait()
    w = w_ref[...]
    half = buf.shape[2]
    acc_lo = jnp.zeros((tm, half), F32)
    acc_hi = jnp.zeros((tm, half), F32)
    for k in range(TOP_K):
        lo, hi = _unpack_halves(buf[k])
        wk = w[:, k:k + 1]
        acc_lo = acc_lo + wk * lo
        acc_hi = acc_hi + wk * hi
    o_ref[:, :half] = acc_lo
    o_ref[:, half:] = acc_hi


def _moe_combine(pos, w_tok, ys):
    n = w_tok.shape[0]
    half = ys.shape[1]
    tm = _tile(n, 128, LANES)
    return pl.pallas_call(
        functools.partial(_moe_combine_body, tm=tm),
        grid=(n // tm,),
        in_specs=[pl.BlockSpec((TOP_K, tm), lambda i: (0, i), memory_space=pltpu.SMEM),
                  pl.BlockSpec((tm, LANES), lambda i: (i, 0)),
                  pl.BlockSpec(memory_space=pl.ANY)],
        out_specs=pl.BlockSpec((tm, 2 * half), lambda i: (i, 0)),
        out_shape=jax.ShapeDtypeStruct((n, 2 * half), F32),
        scratch_shapes=[pltpu.VMEM((TOP_K, tm, half), jnp.uint32), pltpu.SemaphoreType.DMA],
        compiler_params=_cparams(("arbitrary",)),
        name="moe_combine",
    )(pos, w_tok, ys)


def _moe_routed(hp, comb_t, counts, w_gate, w_up, w_down, layer):
    n = hp.shape[0]
    ne = w_gate.shape[1]
    run_tiles = (n * TOP_K) // MOE_TILE + ne
    dummy_row = run_tiles * MOE_TILE
    spare_tiles = -(-(TOP_K * _moe_token_tile(n)) // MOE_TILE)
    n_tiles = run_tiles + spare_tiles
    pos, w_tok, te = _moe_plan(comb_t, counts, n_tiles, dummy_row)
    xg = _moe_dispatch(pos, te, hp, n_tiles * MOE_TILE, ne)
    ys = _moe_ffn(te[0, :n_tiles], te[1, :n_tiles], xg, w_gate, w_up, w_down, layer)
    return _moe_combine(pos, w_tok, ys)


def _moe_shared_residual(h, w_gate, w_up, w_down, f_routed, f_row0, x2, mod, gate_k, rows_per_batch):
    n, d = x2.shape
    g_act = _matmul(h, w_gate, F32, name="shared_gate")
    tm = _tile(n, 512)
    tn = _tile(w_up.shape[1], 1024, LANES)
    act = _matmul(h, w_up, BF16, tm=tm, tn=tn, epilogue=lambda acc, g: g * _sigmoid(g) * acc,
                  extras=[(g_act, pl.BlockSpec((tm, tn), lambda i, j: (i, j)))], name="shared_up")
    tm = _tile(n if rows_per_batch is None else rows_per_batch, 512)
    tn = _tile(d, 1024, LANES)
    nd = d // tn
    mrow = _mod_row_fn(rows_per_batch, tm)
    blk = pl.BlockSpec((tm, tn), lambda i, j: (i, j))
    assert f_row0 % tm == 0
    f_blk = pl.BlockSpec((tm, tn), lambda i, j: (f_row0 // tm + i, j))
    return _matmul(act, w_down, F32, tm=tm, tn=tn,
                   epilogue=lambda acc, fr, xb, g: xb + g[0] * (acc + fr),
                   extras=[(f_routed, f_blk), (x2, blk),
                           (mod, pl.BlockSpec((1, 1, tn), lambda i, j: (mrow(i), 0, gate_k * nd + j)))],
                   name="shared_down")


def _token_mixer(h, hc, lw, lam_init, tabs, need_ctx, batch, t_len, c_len):
    cos, sin = tabs
    flat_rows = _tile(hc.shape[0], 512)
    flat_tabs = (jnp.ones((flat_rows, LANES), F32), jnp.zeros((flat_rows, LANES), F32))

    def project(hh, seq_len, full, positional):
        n = hh.shape[0]
        rope = (cos, sin, seq_len) if positional else flat_tabs + (None,)
        out = {}
        out["k"] = _proj_qk(hh, lw["w_k"], lw["da_k_norm"], rope[0], rope[1], 1.0, rope[2], "proj_k")
        out["v"] = _matmul(hh, lw["w_v"], BF16, name="proj_v")
        out["rw"] = _matmul(hh, lw["w_rw"], F32, name="proj_rw")
        tm = _tile(seq_len, 512)
        nt = seq_len // tm
        tn = S5_WIDTH
        out["s5"] = _matmul(
            hh, lw["w_s5"], F32, tn=tn, grid_m=n // tm,
            a_spec=pl.BlockSpec((tm, hh.shape[1]), lambda i, j: (i, 0)),
            out_spec=pl.BlockSpec((tm, tn), lambda i, j: (i % nt, i // nt)),
            out_shape=(seq_len, batch * S5_WIDTH), name="proj_s5").reshape(seq_len, batch, S5_WIDTH)
        if full:
            out["q"] = _proj_qk(hh, lw["w_q"], lw["da_q_norm"], rope[0], rope[1], DA_Q_SCALE, rope[2], "proj_q")
            out["gates"] = _matmul(hh, lw["w_gates"], BF16, epilogue=_sigmoid, name="proj_gates")
        return out

    pl_ = project(h, t_len, True, True)
    pc_ = project(hc, c_len, need_ctx, False)

    y_a = _diff_attention(pl_["q"], pl_["k"], pl_["v"], pc_["k"], pc_["v"], lw["da_lambda_q"], lw["da_lambda_k"],
                          lw["da_subln"], lam_init, batch)
    y_ac = None
    if need_ctx:
        y_ac = _diff_attention(pc_["q"], None, None, pc_["k"], pc_["v"], lw["da_lambda_q"], lw["da_lambda_k"],
                               lw["da_subln"], lam_init, batch)

    def prep(p_rw, seq_len):
        names = ("lw_f", "lw_b", "kk", "kka_f", "kka_b", "kd_f", "kd_b", "v", "r", "gd")
        vals = _rwkv_prep(p_rw, seq_len, lw["rw_conv"], lw["rw_lora"], lw["rw_w0a0"], lw["rw_k_k"], lw["rw_k_a"])
        return dict(zip(names, vals))

    tl, tc = prep(pl_["rw"], t_len), prep(pc_["rw"], c_len)
    gw = RW_GROUP_HEADS * RW_HEAD_DIM
    s_zero = jnp.zeros((batch * (RW_WIDTH // gw), gw, gw), F32)
    ocf, ocb, s_ctx_f, s_ctx_b = _rwkv_scan(tc, s_zero, s_zero, batch, want_out=need_ctx)
    olf, olb, _, _ = _rwkv_scan(tl, s_ctx_f, s_ctx_b, batch)
    o_lat, o_ctx = {"f": olf, "b": olb}, {"f": ocf, "b": ocb}
    ro = lambda o, t: _rwkv_readout(o["f"], o["b"], t["r"], t["kd_f"], t["kd_b"], t["v"], t["gd"],
                                    lw["rw_gn_w"], lw["rw_gn_b"], lw["rw_r_k"], lw["rw_g2"])
    y_r = ro(o_lat, tl)
    y_rc = ro(o_ctx, tc) if need_ctx else None

    h_zero = jnp.zeros((2, batch, S5_LANES), F32)
    ys_lat, ys_ctx = {}, {}
    for d, rev in (("f", False), ("b", True)):
        yc, h_ctx = _s5_scan(pc_["s5"], h_zero, lw["s5_" + d], reverse=rev, want_out=need_ctx)
        yl, _ = _s5_scan(pl_["s5"], h_ctx, lw["s5_" + d], reverse=rev)
        ys_lat[d], ys_ctx[d] = yl, yc
    flat = lambda a: a.reshape(a.shape[0] * batch, S5_WIDTH)
    glu = lambda p, ys, seq: _s5_glu(flat(p["s5"]), flat(ys["f"]), flat(ys["b"]), lw["s5_d"], lw["s5_glu_w"],
                                     lw["s5_glu_b"]).reshape(seq, batch * S5_WIDTH)
    y_s = glu(pl_, ys_lat, t_len)
    y_sc = glu(pc_, ys_ctx, c_len) if need_ctx else None

    m = _merge(y_a, y_r, y_s, pl_["gates"], lw["w_branch_a"], lw["w_branch_r"], lw["w_branch_s"], t_len, batch)
    m_c = None
    if need_ctx:
        m_c = _merge(y_ac, y_rc, y_sc, pc_["gates"], lw["w_branch_a"], lw["w_branch_r"], lw["w_branch_s"],
                     c_len, batch)
    return m, m_c


def _out_proj_residual(m, w_out, x2, mod, gate_k, rows_per_batch):
    n, d = x2.shape
    tm = _tile(n if rows_per_batch is None else rows_per_batch, 512)
    tn = _tile(d, 1024, LANES)
    nd = d // tn
    mrow = _mod_row_fn(rows_per_batch, tm)

    def epilogue(acc, xb, g):
        return xb + g[0] * acc

    return _matmul(m, w_out, F32, tm=tm, tn=tn, epilogue=epilogue,
                   extras=[(x2, pl.BlockSpec((tm, tn), lambda i, j: (i, j))),
                           (mod, pl.BlockSpec((1, 1, tn), lambda i, j: (mrow(i), 0, gate_k * nd + j)))],
                   name="out_proj")


def _prepare_layer(i, p):
    w_in = p["w_in"][i]
    c0 = DA_WIDTH
    c1 = 2 * DA_WIDTH
    c2 = c1 + RW_STATE_COLS
    c3 = c2 + S5_WIDTH
    c4 = c3 + DA_WIDTH
    c5 = c4 + RW_OUT_COLS
    bf = lambda a: a.astype(BF16)
    lw = {
        "w_k": bf(w_in[:, :c0]), "w_v": bf(w_in[:, c0:c1]),
        "w_rw": bf(jnp.concatenate([w_in[:, c1:c2], w_in[:, c4:c5]], axis=1)),
        "w_s5": bf(w_in[:, c2:c3]), "w_q": bf(w_in[:, c3:c4]), "w_gates": bf(w_in[:, c5:]),
    }
    for name in ("da_q_norm", "da_k_norm", "da_lambda_q", "da_lambda_k", "da_subln", "rw_conv", "rw_k_k", "rw_k_a",
                 "rw_gn_w", "rw_gn_b", "s5_d", "s5_glu_b", "router_bias"):
        lw[name] = p[name][i].astype(F32)
    w = RW_WIDTH
    lora = jnp.zeros((2 * RW_DECAY_RANK + 2 * RW_A_RANK, 4 * w), F32)
    r0 = 0
    for blk, src in enumerate((p["rw_w2"][i][0], p["rw_w2"][i][1], p["rw_a2"][i][0], p["rw_a2"][i][1])):
        lora = lora.at[r0:r0 + src.shape[0], blk * w:(blk + 1) * w].set(src.astype(F32))
        r0 += src.shape[0]
    lw["rw_lora"] = jnp.stack(_hilo(lora))
    lw["rw_w0a0"] = jnp.concatenate([p["rw_w0"][i][0], p["rw_w0"][i][1], p["rw_a0"][i][0], p["rw_a0"][i][1]]
                                    ).astype(F32).reshape(1, 4 * w)
    lw["rw_r_k"] = p["rw_r_k"][i].astype(F32).reshape(w)
    lw["rw_g2"] = bf(p["rw_g2"][i])
    for d, name in enumerate(("s5_f", "s5_b")):
        lw[name] = _s5_dir_params(p["s5_lambda_re"][i][d], p["s5_lambda_im"][i][d], p["s5_log_dt"][i][d],
                                  p["s5_b_re"][i].astype(F32), p["s5_b_im"][i].astype(F32),
                                  p["s5_c_re"][i][d], p["s5_c_im"][i][d])
    lw["s5_glu_w"] = bf(p["s5_glu_w"][i])
    for name in ("w_branch_a", "w_branch_r", "w_branch_s", "w_out"):
        lw[name] = bf(p[name][i])
    for name in ("exp_w_gate", "exp_w_up", "exp_w_down"):
        lw[name] = p[name]
    lw["layer"] = i
    d_model = w_in.shape[0]
    rw_t = p["router_w"][i].astype(F32).T
    lw["router_w_t"] = jnp.concatenate([rw_t, jnp.zeros((LANES - N_EXPERTS, d_model), F32)], axis=0)
    for name in ("sh_w_gate", "sh_w_up", "sh_w_down"):
        lw[name] = bf(p[name][i])
    return lw


def kernel(x, c, ctx, c_ctx, ada_w, ada_b, w_in, da_q_norm, da_k_norm, da_lambda_q, da_lambda_k, da_subln, rw_conv, rw_w0, rw_w2, rw_a0, rw_a2, rw_g2, rw_k_k, rw_k_a, rw_r_k, rw_gn_w, rw_gn_b, s5_lambda_re, s5_lambda_im, s5_log_dt, s5_b_re, s5_b_im, s5_c_re, s5_c_im, s5_d, s5_glu_w, s5_glu_b, w_branch_a, w_branch_r, w_branch_s, w_out, router_w, router_bias, exp_w_gate, exp_w_up, exp_w_down, sh_w_gate, sh_w_up, sh_w_down):
    params = dict(w_in=w_in, da_q_norm=da_q_norm, da_k_norm=da_k_norm, da_lambda_q=da_lambda_q,
                  da_lambda_k=da_lambda_k, da_subln=da_subln, rw_conv=rw_conv, rw_w0=rw_w0, rw_w2=rw_w2,
                  rw_a0=rw_a0, rw_a2=rw_a2, rw_g2=rw_g2, rw_k_k=rw_k_k, rw_k_a=rw_k_a, rw_r_k=rw_r_k,
                  rw_gn_w=rw_gn_w, rw_gn_b=rw_gn_b, s5_lambda_re=s5_lambda_re, s5_lambda_im=s5_lambda_im,
                  s5_log_dt=s5_log_dt, s5_b_re=s5_b_re, s5_b_im=s5_b_im, s5_c_re=s5_c_re, s5_c_im=s5_c_im,
                  s5_d=s5_d, s5_glu_w=s5_glu_w, s5_glu_b=s5_glu_b, w_branch_a=w_branch_a, w_branch_r=w_branch_r,
                  w_branch_s=w_branch_s, w_out=w_out, router_w=router_w, router_bias=router_bias,
                  exp_w_gate=exp_w_gate, exp_w_up=exp_w_up, exp_w_down=exp_w_down, sh_w_gate=sh_w_gate,
                  sh_w_up=sh_w_up, sh_w_down=sh_w_down)
    batch, t_len, d_model = x.shape
    c_len = ctx.shape[1]
    depth = ada_w.shape[0]
    assert batch <= MOD_CTX_ROW
    tabs = _rope_tables(t_len)
    cvec = jnp.zeros((2 * SUBLANES, d_model), F32).at[:batch].set(c.astype(F32)).at[MOD_CTX_ROW].set(c_ctx.astype(F32))
    x2 = x.astype(F32).reshape(batch * t_len, d_model)
    ctx2 = ctx.astype(F32).reshape(batch * c_len, d_model)
    for i in range(depth):
        lw = _prepare_layer(i, params)
        need_ctx = i < depth - 1
        lam_init = 0.8 - 0.6 * math.exp(-0.3 * i)
        mod = _ada_table(cvec, ada_w[i], ada_b[i])
        h = _modulate(x2, mod, 0, 1, t_len)
        hc = _modulate(ctx2, mod, 0, 1, None)
        m, m_c = _token_mixer(h, hc, lw, lam_init, tabs, need_ctx, batch, t_len, c_len)
        x2 = _out_proj_residual(m, lw["w_out"], x2, mod, 2, t_len)
        streams = [(x2, t_len)]
        if need_ctx:
            ctx2 = _out_proj_residual(m_c, lw["w_out"], ctx2, mod, 2, None)
            streams.append((ctx2, None))
        routed = [_modulate_route(xs, mod, 3, 4, rpb, lw["router_w_t"], lw["router_bias"]) for xs, rpb in streams]
        f_r = _moe_routed(jnp.concatenate([r[1] for r in routed], axis=0),
                          jnp.concatenate([r[2] for r in routed], axis=1),
                          sum(r[3] for r in routed),
                          lw["exp_w_gate"], lw["exp_w_up"], lw["exp_w_down"], lw["layer"])
        outs, row0 = [], 0
        for (xs, rpb), r in zip(streams, routed):
            outs.append(_moe_shared_residual(r[0], lw["sh_w_gate"], lw["sh_w_up"], lw["sh_w_down"], f_r, row0,
                                             xs, mod, 5, rpb))
            row0 += xs.shape[0]
        x2 = outs[0]
        if need_ctx:
            ctx2 = outs[1]
    return x2.reshape(batch, t_len, d_model).astype(x.dtype)
```

```python
import functools
import math

import jax
import jax.numpy as jnp
from jax import lax
from jax.experimental import pallas as pl
from jax.experimental.pallas import tpu as pltpu

F32 = jnp.float32
BF16 = jnp.bfloat16

GRID_W = 64
NORM_EPS = 1e-6
DA_HEADS = 16
DA_HEAD_DIM = 64
DA_V_DIM = 2 * DA_HEAD_DIM
DA_WIDTH = DA_HEADS * DA_V_DIM
DA_SCALE = DA_HEAD_DIM ** -0.5
DA_Q_SCALE = DA_SCALE * math.log2(math.e)
ROPE_BASE = 10000.0
RW_HEADS = 16
RW_HEAD_DIM = 64
RW_WIDTH = RW_HEADS * RW_HEAD_DIM
RW_DECAY_RANK = 64
RW_A_RANK = 64
RW_GATE_RANK = 128
RW_GN_EPS = 64e-5
RW_STATE_COLS = 2 * RW_WIDTH + 2 * RW_DECAY_RANK + 2 * RW_A_RANK
RW_OUT_COLS = RW_WIDTH + RW_GATE_RANK
S5_GROUP = 16
S5_GROUPS = 64
S5_WIDTH = S5_GROUPS * S5_GROUP
S5_STATE = 64
S5_LANES = S5_GROUPS * S5_STATE
N_EXPERTS = 64
TOP_K = 8
N_GROUPS = 8
TOPK_GROUPS = 4
EXPERT_FF = 256
ROUTED_SCALE = 2.5

LANES = 128
SUBLANES = 8
VMEM_LIMIT_BYTES = 56 * 1024 * 1024

RW_CHUNK = 64
RW_GROUP_HEADS = 4
S5_CHUNK = 64
MOD_CTX_ROW = 8
MOE_TILE = 256


def _cparams(sem):
    return pltpu.CompilerParams(dimension_semantics=sem, vmem_limit_bytes=VMEM_LIMIT_BYTES)


def _tile(n, pref, mult=SUBLANES):
    if n <= pref:
        return n
    t = (pref // mult) * mult
    while t > mult and n % t:
        t -= mult
    assert n % t == 0, (n, pref)
    return t


def _dot(a, b):
    return jnp.dot(a.astype(BF16), b.astype(BF16), preferred_element_type=F32)


def _dot_nt(a, b):
    return lax.dot_general(a.astype(BF16), b.astype(BF16), (((1,), (1,)), ((), ())),
                           preferred_element_type=F32)


def _dot_tn(a, b):
    return lax.dot_general(a.astype(BF16), b.astype(BF16), (((0,), (0,)), ((), ())),
                           preferred_element_type=F32)


def _hilo(x):
    hi = x.astype(BF16)
    lo = (x - hi.astype(F32)).astype(BF16)
    return hi, lo


def _dot_hp_lhs(a, b_exact):
    hi, lo = _hilo(a)
    return (jnp.dot(hi, b_exact, preferred_element_type=F32)
            + jnp.dot(lo, b_exact, preferred_element_type=F32))


def _dot_hp(a, b):
    ah, al = _hilo(a)
    bh, bl = _hilo(b)
    return (jnp.dot(ah, bh, preferred_element_type=F32) + jnp.dot(al, bh, preferred_element_type=F32)
            + jnp.dot(ah, bl, preferred_element_type=F32))


def _sigmoid(x):
    return 1.0 / (1.0 + jnp.exp(-x))


def _softplus(x):
    return jnp.maximum(x, 0.0) + jnp.log(1.0 + jnp.exp(-jnp.abs(x)))


def _gelu_tanh(x):
    c = math.sqrt(2.0 / math.pi)
    return 0.5 * x * (1.0 + jnp.tanh(c * (x + 0.044715 * (x * x * x))))


def _mm_body(a_ref, b_ref, *rest, n_extra, prologue, epilogue):
    extras = rest[:n_extra]
    o_ref = rest[n_extra]
    a = a_ref[...]
    if prologue is not None:
        a = prologue(a)
    b = b_ref[0] if len(b_ref.shape) == 3 else b_ref[...]
    acc = jnp.dot(a.astype(BF16), b.astype(BF16), preferred_element_type=F32)
    if epilogue is not None:
        acc = epilogue(acc, *[e[...] for e in extras])
    o_ref[...] = acc.astype(o_ref.dtype)


def _matmul(a, b, out_dtype, *, tm=512, tn=1024, prologue=None, epilogue=None, extras=(),
            a_spec=None, out_spec=None, out_shape=None, grid_m=None, b_layer=None, name="matmul"):
    k, n = b.shape[-2:]
    tn = _tile(n, tn, LANES)
    if b_layer is None:
        b_spec = pl.BlockSpec((k, tn), lambda i, j: (0, j))
    else:
        b_spec = pl.BlockSpec((1, k, tn), lambda i, j: (b_layer, 0, j))
    if a_spec is None:
        m = a.shape[0]
        tm = _tile(m, tm)
        grid_m = m // tm
        a_spec = pl.BlockSpec((tm, k), lambda i, j: (i, 0))
    if out_spec is None:
        out_spec = pl.BlockSpec((tm, tn), lambda i, j: (i, j))
        out_shape = (a.shape[0], n)
    body = functools.partial(_mm_body, n_extra=len(extras), prologue=prologue, epilogue=epilogue)
    return pl.pallas_call(
        body,
        grid=(grid_m, n // tn),
        in_specs=[a_spec, b_spec] + [s for _, s in extras],
        out_specs=out_spec,
        out_shape=jax.ShapeDtypeStruct(out_shape, out_dtype),
        compiler_params=_cparams(("parallel", "arbitrary")),
        name=name,
    )(a, b, *[x for x, _ in extras])


def _ada_table(cvec, ada_w, ada_b, layer):
    d6 = ada_w.shape[2]

    def prologue(a):
        return a * _sigmoid(a)

    def epilogue(acc, bias):
        return acc + bias

    out = _matmul(cvec, ada_w, F32, tm=16, tn=512, prologue=prologue, epilogue=epilogue, b_layer=layer,
                  extras=[(ada_b[layer].reshape(1, d6), pl.BlockSpec((1, 512), lambda i, j: (0, j)))],
                  name="ada_table")
    return out.reshape(cvec.shape[0], 1, d6)


def _mod_row_fn(rows_per_batch, tm):
    if rows_per_batch is None:
        return lambda i: MOD_CTX_ROW
    nb = rows_per_batch // tm
    return lambda i: i // nb


def _modulate_body(x_ref, sh_ref, sc_ref, o_ref):
    x = x_ref[...]
    ms = jnp.mean(x * x, axis=-1, keepdims=True)
    h = x * lax.rsqrt(ms + NORM_EPS) * (1.0 + sc_ref[0]) + sh_ref[0]
    o_ref[...] = h.astype(o_ref.dtype)


def _modulate(x2, mod, shift_k, scale_k, rows_per_batch):
    n, d = x2.shape
    tm = _tile(n if rows_per_batch is None else rows_per_batch, 256)
    row = _mod_row_fn(rows_per_batch, tm)
    return pl.pallas_call(
        _modulate_body,
        grid=(n // tm,),
        in_specs=[pl.BlockSpec((tm, d), lambda i: (i, 0)),
                  pl.BlockSpec((1, 1, d), lambda i: (row(i), 0, shift_k)),
                  pl.BlockSpec((1, 1, d), lambda i: (row(i), 0, scale_k))],
        out_specs=pl.BlockSpec((tm, d), lambda i: (i, 0)),
        out_shape=jax.ShapeDtypeStruct((n, d), BF16),
        compiler_params=_cparams(("parallel",)),
        name="modulate",
    )(x2, mod, mod)


def _pack_halves(x):
    w = x.shape[1] // 2
    lo = lax.bitcast_convert_type(x[:, :w].astype(BF16).astype(F32), jnp.uint32)
    hi = lax.bitcast_convert_type(x[:, w:].astype(BF16).astype(F32), jnp.uint32)
    return lax.shift_right_logical(lo, jnp.uint32(16)) | hi


def _unpack_halves(p):
    lo = lax.bitcast_convert_type(lax.shift_left(p, jnp.uint32(16)), F32)
    hi = lax.bitcast_convert_type(p & jnp.uint32(0xFFFF0000), F32)
    return lo, hi


def _route_body(x_ref, sh_ref, sc_ref, wr_ref, bias_ref, h_ref, hp_ref, comb_ref, cnt_ref):
    x = x_ref[...]
    ms = jnp.mean(x * x, axis=-1, keepdims=True)
    h = x * lax.rsqrt(ms + NORM_EPS) * (1.0 + sc_ref[0]) + sh_ref[0]
    h_ref[...] = h.astype(h_ref.dtype)
    hp_ref[...] = _pack_halves(h)
    tm = x.shape[0]
    wr = wr_ref[...]
    hh, hl = _hilo(h)
    wh, wl = _hilo(wr)
    logits = _dot_nt(wh, hh) + _dot_nt(wl, hh) + _dot_nt(wh, hl)
    scores = _sigmoid(logits[:N_EXPERTS])
    per_group = N_EXPERTS // N_GROUPS
    sc3 = scores.reshape(N_GROUPS, per_group, tm)
    sel = sc3 + bias_ref[...]
    midx = lax.broadcasted_iota(jnp.int32, sel.shape, 1)
    neg = jnp.float32(-jnp.inf)
    m1 = jnp.max(sel, axis=1, keepdims=True)
    first = jnp.min(jnp.where(sel == m1, midx, per_group), axis=1, keepdims=True)
    m2 = jnp.max(jnp.where(midx == first, neg, sel), axis=1, keepdims=True)
    gs = (m1 + m2).reshape(N_GROUPS, tm)
    gidx = lax.broadcasted_iota(jnp.int32, gs.shape, 0)
    gmask = jnp.zeros(gs.shape, jnp.bool_)
    for _ in range(TOPK_GROUPS):
        m = jnp.max(gs, axis=0, keepdims=True)
        f = jnp.min(jnp.where(gs == m, gidx, N_GROUPS), axis=0, keepdims=True)
        pick = gidx == f
        gmask = jnp.logical_or(gmask, pick)
        gs = jnp.where(pick, neg, gs)
    val = jnp.where(gmask.reshape(N_GROUPS, 1, tm), sel, neg)
    eidx = lax.broadcasted_iota(jnp.int32, sel.shape, 0) * per_group + midx
    chosen = jnp.zeros(sel.shape, jnp.bool_)
    for _ in range(TOP_K):
        m = jnp.max(jnp.max(val, axis=1, keepdims=True), axis=0, keepdims=True)
        f = jnp.min(jnp.min(jnp.where(val == m, eidx, N_EXPERTS), axis=1, keepdims=True), axis=0, keepdims=True)
        pick = eidx == f
        chosen = jnp.logical_or(chosen, pick)
        val = jnp.where(pick, neg, val)
    w = jnp.where(chosen, sc3, 0.0)
    wsum = jnp.sum(jnp.sum(w, axis=1, keepdims=True), axis=0, keepdims=True)
    comb = (w / wsum * ROUTED_SCALE).reshape(N_EXPERTS, tm)
    comb_ref[...] = comb

    @pl.when(pl.program_id(0) == 0)
    def _():
        cnt_ref[...] = jnp.zeros_like(cnt_ref)

    cnt_ref[...] += jnp.sum((comb > 0.0).astype(F32), axis=1, keepdims=True)


def _modulate_route(x2, mod, shift_k, scale_k, rows_per_batch, router_w_t, router_bias):
    n, d = x2.shape
    tm = _tile(n if rows_per_batch is None else rows_per_batch, 256, LANES)
    row = _mod_row_fn(rows_per_batch, tm)
    return pl.pallas_call(
        _route_body,
        grid=(n // tm,),
        in_specs=[pl.BlockSpec((tm, d), lambda i: (i, 0)),
                  pl.BlockSpec((1, 1, d), lambda i: (row(i), 0, shift_k)),
                  pl.BlockSpec((1, 1, d), lambda i: (row(i), 0, scale_k)),
                  pl.BlockSpec((LANES, d), lambda i: (0, 0)),
                  pl.BlockSpec((N_GROUPS, N_EXPERTS // N_GROUPS, 1), lambda i: (0, 0, 0))],
        out_specs=[pl.BlockSpec((tm, d), lambda i: (i, 0)),
                   pl.BlockSpec((tm, d // 2), lambda i: (i, 0)),
                   pl.BlockSpec((N_EXPERTS, tm), lambda i: (0, i)),
                   pl.BlockSpec((N_EXPERTS, LANES), lambda i: (0, 0))],
        out_shape=[jax.ShapeDtypeStruct((n, d), BF16), jax.ShapeDtypeStruct((n, d // 2), jnp.uint32),
                   jax.ShapeDtypeStruct((N_EXPERTS, n), F32), jax.ShapeDtypeStruct((N_EXPERTS, LANES), F32)],
        compiler_params=_cparams(("arbitrary",)),
        name="modulate_route",
    )(x2, mod, mod, router_w_t, router_bias.reshape(N_GROUPS, N_EXPERTS // N_GROUPS, 1))


def _group_ones(width, group):
    r = lax.broadcasted_iota(jnp.int32, (width, width), 0) // group
    c = lax.broadcasted_iota(jnp.int32, (width, width), 1) // group
    return (r == c).astype(BF16)


def _qk_norm_rope(x, gain, cos, sin, scale):
    ss = _dot_hp_lhs(x * x, _group_ones(LANES, DA_HEAD_DIM))
    xn = x * lax.rsqrt(ss * (1.0 / DA_HEAD_DIM) + NORM_EPS) * gain
    lane = lax.broadcasted_iota(jnp.int32, x.shape, 1)
    quarter = DA_HEAD_DIM // 4
    partner = jnp.where((lane % (2 * quarter)) < quarter,
                        pltpu.roll(xn, LANES - quarter, axis=1),
                        pltpu.roll(xn, quarter, axis=1))
    return (xn * cos + partner * sin) * scale


def _proj_qk(h, w, gain, cos, sin, scale, t_len, name):
    n = h.shape[0]
    tm = _tile(n if t_len is None else t_len, 512)
    nb = 1 if t_len is None else t_len // tm
    tab = pl.BlockSpec((tm, LANES), lambda i, j: (i % nb, 0))
    gain2 = jnp.tile(gain.reshape(1, DA_HEAD_DIM), (1, 2))

    def epilogue(acc, g, c, s):
        heads = [_qk_norm_rope(acc[:, k * LANES:(k + 1) * LANES], g, c, s, scale)
                 for k in range(acc.shape[1] // LANES)]
        return jnp.concatenate(heads, axis=1)

    return _matmul(h, w, BF16, tm=tm, epilogue=epilogue,
                   extras=[(gain2, pl.BlockSpec((1, LANES), lambda i, j: (0, 0))), (cos, tab), (sin, tab)],
                   name=name)


def _rope_tables(t_len):
    rows = t_len // GRID_W
    row = jnp.repeat(jnp.arange(rows, dtype=F32), GRID_W)
    col = jnp.tile(jnp.arange(GRID_W, dtype=F32), rows)
    half = DA_HEAD_DIM // 2
    inv_freq = 1.0 / (ROPE_BASE ** (jnp.arange(0, half, 2, dtype=F32) / half))
    ang_r = row[:, None] * inv_freq
    ang_c = col[:, None] * inv_freq
    cos64 = jnp.concatenate([jnp.cos(ang_r), jnp.cos(ang_r), jnp.cos(ang_c), jnp.cos(ang_c)], axis=1)
    sin64 = jnp.concatenate([-jnp.sin(ang_r), jnp.sin(ang_r), -jnp.sin(ang_c), jnp.sin(ang_c)], axis=1)
    return jnp.tile(cos64, (1, 2)), jnp.tile(sin64, (1, 2))


def _attn_body(*refs, has_lat, lam_init):
    if has_lat:
        q_ref, kl_ref, vl_ref, kc_ref, vc_ref, lq_ref, lk_ref, sub_ref, o_ref = refs
    else:
        q_ref, kc_ref, vc_ref, lq_ref, lk_ref, sub_ref, o_ref = refs
    lqk = lq_ref[...] * lk_ref[...]
    lsum = jnp.sum(lqk, axis=1, keepdims=True)
    e = jnp.exp(lsum)
    lam = e[0:1, :] - e[1:2, :] + lam_init
    q = q_ref[...]
    lane = lax.broadcasted_iota(jnp.int32, q.shape, 1)
    zero = jnp.zeros_like(q)
    ext = lambda v: jnp.concatenate([v, jnp.ones_like(v)], axis=1)
    vc_ext = ext(vc_ref[...])
    vl_ext = ext(vl_ref[...]) if has_lat else None
    mixed = []
    for m in range(2):
        in_map = (lane // DA_HEAD_DIM) == m
        qm = jnp.where(in_map, q, zero)
        s_c = _dot_nt(qm, kc_ref[...]).astype(BF16)
        mx = jnp.max(s_c, axis=-1, keepdims=True)
        if has_lat:
            s_l = _dot_nt(qm, kl_ref[...]).astype(BF16)
            mx = jnp.maximum(mx, jnp.max(s_l, axis=-1, keepdims=True))
        acc = jnp.dot(jnp.exp2(s_c - mx), vc_ext, preferred_element_type=F32)
        if has_lat:
            acc = acc + jnp.dot(jnp.exp2(s_l - mx), vl_ext, preferred_element_type=F32)
        mixed.append(acc[:, :DA_V_DIM] / acc[:, DA_V_DIM:])
    o = mixed[0] - lam * mixed[1]
    ms = jnp.mean(o * o, axis=-1, keepdims=True)
    o = o * lax.rsqrt(ms + NORM_EPS) * sub_ref[...] * (1.0 - lam_init)
    o_ref[...] = o.astype(o_ref.dtype)


def _diff_attention(q, k_lat, v_lat, k_ctx, v_ctx, lq, lk, subln, lam_init, batch):
    n, w = q.shape
    tq_len = n // batch
    c_len = k_ctx.shape[0] // batch
    tq = _tile(tq_len, 1024)
    nq = tq_len // tq
    has_lat = k_lat is not None
    blk = lambda rows: pl.BlockSpec((rows, LANES), lambda b, h, i: (b, h))
    in_specs = [pl.BlockSpec((tq, LANES), lambda b, h, i: (b * nq + i, h))]
    args = [q]
    if has_lat:
        t_len = k_lat.shape[0] // batch
        in_specs += [blk(t_len), blk(t_len)]
        args += [k_lat, v_lat]
    in_specs += [blk(c_len), blk(c_len),
                 pl.BlockSpec((2, DA_HEAD_DIM), lambda b, h, i: (0, 0)),
                 pl.BlockSpec((2, DA_HEAD_DIM), lambda b, h, i: (0, 0)),
                 pl.BlockSpec((1, LANES), lambda b, h, i: (0, 0))]
    args += [k_ctx, v_ctx, lq, lk, subln.reshape(1, DA_V_DIM)]
    return pl.pallas_call(
        functools.partial(_attn_body, has_lat=has_lat, lam_init=lam_init),
        grid=(batch, w // LANES, nq),
        in_specs=in_specs,
        out_specs=pl.BlockSpec((tq, LANES), lambda b, h, i: (b * nq + i, h)),
        out_shape=jax.ShapeDtypeStruct((n, w), BF16),
        compiler_params=_cparams(("parallel", "parallel", "arbitrary")),
        name="diff_attention",
    )(*args)


def _rwkv_prep_body(x_ref, xp_ref, xn_ref, conv_ref, lora_ref, w0a0_ref, kk_w_ref, ka_w_ref, ones_ref,
                    lwf_ref, lwb_ref, kk_ref, kkaf_ref, kkab_ref, kdf_ref, kdb_ref, v_ref, r_ref, gd_ref,
                    *, blocks_per_seq):
    i = pl.program_id(0)
    x = x_ref[...]
    tm = x.shape[0]
    first = (i % blocks_per_seq) == 0
    last = (i % blocks_per_seq) == blocks_per_seq - 1
    xp = jnp.where(first, 0.0, xp_ref[SUBLANES - 1:SUBLANES, :])
    xn = jnp.where(last, 0.0, xn_ref[0:1, :])
    row = lax.broadcasted_iota(jnp.int32, (tm, 1), 0)
    up = jnp.where(row == 0, xp, pltpu.roll(x, 1, axis=0))
    dn = jnp.where(row == tm - 1, xn, pltpu.roll(x, tm - 1, axis=0))
    cw = conv_ref[...]
    cv = up * cw[0:1, :] + x * cw[1:2, :] + dn * cw[2:3, :]
    w = RW_WIDTH
    k = cv[:, :w]
    v = cv[:, w:2 * w]
    lora_in = cv[:, 2 * w:RW_STATE_COLS]
    r = cv[:, RW_STATE_COLS:RW_STATE_COLS + w]
    gd = cv[:, RW_STATE_COLS + w:]
    lane = lax.broadcasted_iota(jnp.int32, lora_in.shape, 1)
    li = jnp.where(lane < 2 * RW_DECAY_RANK, jnp.tanh(lora_in), lora_in)
    li_hi, li_lo = _hilo(li)
    pre = (jnp.dot(li_hi, lora_ref[0], preferred_element_type=F32)
           + jnp.dot(li_lo, lora_ref[0], preferred_element_type=F32)
           + jnp.dot(li_hi, lora_ref[1], preferred_element_type=F32)) + w0a0_ref[...]
    kkr = k * kk_w_ref[...]
    ss = _dot_hp_lhs(kkr * kkr, ones_ref[...])
    kk = kkr * lax.rsqrt(ss + 1e-12)
    kk_ref[...] = kk
    v_ref[...] = v
    r_ref[...] = r
    gd_ref[...] = gd
    ka = ka_w_ref[...]
    for d, (lw_ref, kka_ref, kd_ref) in enumerate(((lwf_ref, kkaf_ref, kdf_ref), (lwb_ref, kkab_ref, kdb_ref))):
        lw_ref[...] = -math.exp(-0.5) * _sigmoid(pre[:, d * w:(d + 1) * w])
        a = _sigmoid(pre[:, (2 + d) * w:(3 + d) * w])
        kka_ref[...] = kk * a
        kd_ref[...] = k * (1.0 + (a - 1.0) * ka)


def _rwkv_prep(p_rw, seq_len, conv, lora_w, w0a0, k_k, k_a):
    n, c = p_rw.shape
    tm = _tile(seq_len, 128)
    bps = seq_len // tm
    sub = tm // SUBLANES
    nsub = n // SUBLANES
    w = RW_WIDTH
    wide = lambda: pl.BlockSpec((tm, w), lambda i: (i, 0))
    outs = [jax.ShapeDtypeStruct((n, w), F32)] * 9 + [jax.ShapeDtypeStruct((n, RW_GATE_RANK), F32)]
    return pl.pallas_call(
        functools.partial(_rwkv_prep_body, blocks_per_seq=bps),
        grid=(n // tm,),
        in_specs=[pl.BlockSpec((tm, c), lambda i: (i, 0)),
                  pl.BlockSpec((SUBLANES, c), lambda i: (jnp.maximum(i * sub - 1, 0), 0)),
                  pl.BlockSpec((SUBLANES, c), lambda i: (jnp.minimum((i + 1) * sub, nsub - 1), 0)),
                  pl.BlockSpec((3, c), lambda i: (0, 0)),
                  pl.BlockSpec(lora_w.shape, lambda i: (0, 0, 0)),
                  pl.BlockSpec((1, 4 * w), lambda i: (0, 0)),
                  pl.BlockSpec((1, w), lambda i: (0, 0)),
                  pl.BlockSpec((1, w), lambda i: (0, 0)),
                  pl.BlockSpec((w, w), lambda i: (0, 0))],
        out_specs=[wide() for _ in range(9)] + [pl.BlockSpec((tm, RW_GATE_RANK), lambda i: (i, 0))],
        out_shape=outs,
        compiler_params=_cparams(("parallel",)),
        name="rwkv_prep",
    )(p_rw, p_rw, p_rw, conv, lora_w, w0a0, k_k.reshape(1, w), k_a.reshape(1, w), _group_ones(w, RW_HEAD_DIM))


def _rwkv_scan_body(lwf_ref, kkaf_ref, kdf_ref, kkf_ref, vf_ref, rf_ref,
                    lwb_ref, kkab_ref, kdb_ref, kkb_ref, vb_ref, rb_ref, s0f_ref, s0b_ref,
                    of_ref, ob_ref, stf_ref, stb_ref, *, want_out):
    ci = pl.program_id(1)

    @pl.when(ci == 0)
    def _():
        stf_ref[...] = s0f_ref[...]
        stb_ref[...] = s0b_ref[...]

    n_l, width = lwf_ref.shape
    hd = RW_HEAD_DIM
    gw = RW_GROUP_HEADS * hd
    n_groups = width // gw
    row = lax.broadcasted_iota(jnp.int32, (n_l, n_l), 0)
    col = lax.broadcasted_iota(jnp.int32, (n_l, n_l), 1)
    trow = lax.broadcasted_iota(jnp.int32, (n_l, gw), 0)
    tcol = lax.broadcasted_iota(jnp.int32, (n_l, gw), 1) % hd
    eye = (tcol == trow).astype(F32)
    same_head = ((lax.broadcasted_iota(jnp.int32, (gw, gw), 0) // hd)
                 == (lax.broadcasted_iota(jnp.int32, (gw, gw), 1) // hd))
    reps = gw // n_l

    def bdiag(x):
        xb = x.astype(BF16)
        return jnp.where(same_head, jnp.concatenate([xb] * reps, axis=0), jnp.zeros((), BF16))

    units = []
    for reverse, refs in ((False, (lwf_ref, kkaf_ref, kdf_ref, kkf_ref, vf_ref, rf_ref, stf_ref, of_ref)),
                          (True, (lwb_ref, kkab_ref, kdb_ref, kkb_ref, vb_ref, rb_ref, stb_ref, ob_ref))):
        lw_ref, kka_ref, k_ref, kk_ref, v_ref, r_ref, st_ref, o_ref = refs
        lw = lw_ref[...]
        tri = ((col >= row) if reverse else (col <= row)).astype(BF16)
        lh, ll = _hilo(lw)
        c = jnp.dot(tri, lh, preferred_element_type=F32) + jnp.dot(tri, ll, preferred_element_type=F32)
        g_end = jnp.exp(c[0:1, :] if reverse else c[n_l - 1:n_l, :])
        e_inv = jnp.exp(-c)
        a_t = kk_ref[...] * jnp.exp(c - lw)
        r_t = r_ref[...] * jnp.exp(c)
        k_h = k_ref[...] * e_inv
        b_h = kka_ref[...] * e_inv
        v_all = v_ref[...]
        incl, strict = (tcol >= trow, tcol > trow) if reverse else (tcol <= trow, tcol < trow)
        for g in range(n_groups):
            s = slice(g * gw, (g + 1) * gw)
            units.append(dict(g=g, sl=s, incl=incl, strict=strict, st_ref=st_ref, o_ref=o_ref, state=st_ref[g],
                              ar=jnp.concatenate([a_t[:, s], r_t[:, s]], axis=0).astype(BF16),
                              k_h=k_h[:, s], b_h=b_h[:, s], v=v_all[:, s], g_end=g_end[:, s]))

    n_iter = max(1, (n_l - 1).bit_length()) - 1
    sc_k = [_dot_nt(x["ar"], bdiag(x["k_h"])) for x in units]
    sc_b = [_dot_nt(x["ar"], bdiag(x["b_h"])) for x in units]
    from_s = [_dot_nt(x["ar"], x["state"]) for x in units]
    p = [-jnp.where(x["strict"], sb[:n_l], 0.0) for x, sb in zip(units, sc_b)]
    t_inv = [eye + pi for pi in p]
    if n_iter:
        p = [_dot(pi, bdiag(pi)) for pi in p]
    for it in range(n_iter):
        if it + 1 < n_iter:
            y = [_dot(jnp.concatenate([ti, pi], axis=0), bdiag(pi)) for ti, pi in zip(t_inv, p)]
            t_inv = [ti + yi[:n_l] for ti, yi in zip(t_inv, y)]
            p = [yi[n_l:] for yi in y]
        else:
            t_inv = [ti + _dot(ti, bdiag(pi)) for ti, pi in zip(t_inv, p)]
    v_bd = [bdiag(x["v"]) for x in units]
    w = [fs[:n_l] + _dot(jnp.where(x["strict"], sk[:n_l], 0.0), vb)
         for x, fs, sk, vb in zip(units, from_s, sc_k, v_bd)]
    u = [_dot(ti, bdiag(wi)) for ti, wi in zip(t_inv, w)]
    if want_out:
        o_v = [_dot(jnp.where(x["incl"], sk[n_l:], 0.0), vb) for x, sk, vb in zip(units, sc_k, v_bd)]
        o_u = [_dot(jnp.where(x["incl"], sb[n_l:], 0.0), bdiag(ui)) for x, sb, ui in zip(units, sc_b, u)]
        for x, fs, ov, ou in zip(units, from_s, o_v, o_u):
            x["o_ref"][:, x["sl"]] = fs[n_l:] + ov - ou
    else:
        of_ref[...] = jnp.zeros_like(of_ref)
        ob_ref[...] = jnp.zeros_like(ob_ref)
    upd_k = [_dot_tn(x["v"], x["k_h"] * x["g_end"]) for x in units]
    upd_b = [_dot_tn(ui, x["b_h"] * x["g_end"]) for x, ui in zip(units, u)]
    for x, uk, ub in zip(units, upd_k, upd_b):
        x["st_ref"][x["g"]] = jnp.where(same_head, x["state"] * x["g_end"] + uk - ub, 0.0)


def _rwkv_scan(t, s0_f, s0_b, batch, *, want_out=True):
    n, w = t["kk"].shape
    seq = n // batch
    n_l = _tile(seq, RW_CHUNK)
    nch = seq // n_l
    gw = RW_GROUP_HEADS * RW_HEAD_DIM
    assert gw % n_l == 0 and w % gw == 0
    n_groups = w // gw
    fwd = pl.BlockSpec((n_l, w), lambda b, c: (b * nch + c, 0))
    rev = pl.BlockSpec((n_l, w), lambda b, c: (b * nch + nch - 1 - c, 0))
    st_spec = pl.BlockSpec((n_groups, gw, gw), lambda b, c: (b, 0, 0))
    seq_shape = jax.ShapeDtypeStruct((n, w), F32)
    st_shape = jax.ShapeDtypeStruct(s0_f.shape, F32)
    return pl.pallas_call(
        functools.partial(_rwkv_scan_body, want_out=want_out),
        grid=(batch, nch),
        in_specs=[fwd] * 6 + [rev] * 6 + [st_spec, st_spec],
        out_specs=[fwd, rev, st_spec, st_spec],
        out_shape=[seq_shape, seq_shape, st_shape, st_shape],
        compiler_params=_cparams(("parallel", "arbitrary")),
        name="rwkv_scan",
    )(t["lw_f"], t["kka_f"], t["kd_f"], t["kk"], t["v"], t["r"],
      t["lw_b"], t["kka_b"], t["kd_b"], t["kk"], t["v"], t["r"], s0_f, s0_b)


def _rwkv_readout_body(of_ref, ob_ref, r_ref, kdf_ref, kdb_ref, v_ref, gd_ref, gnw_ref, gnb_ref, rk_ref, g2_ref,
                       ones_ref, y_ref):
    ones = ones_ref[...]
    inv = 1.0 / RW_HEAD_DIM
    o = of_ref[...] + ob_ref[...]
    mu = _dot_hp_lhs(o, ones) * inv
    d = o - mu
    var = _dot_hp_lhs(d * d, ones) * inv
    on = d * lax.rsqrt(var + RW_GN_EPS) * gnw_ref[...] + gnb_ref[...]
    r = r_ref[...]
    rk = rk_ref[...]
    bonus = _dot_hp_lhs(r * kdf_ref[...] * rk, ones) + _dot_hp_lhs(r * kdb_ref[...] * rk, ones)
    y = on + bonus * v_ref[...]
    g = _dot(_sigmoid(gd_ref[...]), g2_ref[...])
    y_ref[...] = (y * g).astype(y_ref.dtype)


def _rwkv_readout(o_f, o_b, r, kd_f, kd_b, v, gd, gn_w, gn_b, r_k, g2):
    n, w = o_f.shape
    tm = _tile(n, 256)
    wide = pl.BlockSpec((tm, w), lambda i: (i, 0))
    vec = pl.BlockSpec((1, w), lambda i: (0, 0))
    return pl.pallas_call(
        _rwkv_readout_body,
        grid=(n // tm,),
        in_specs=[wide] * 6 + [pl.BlockSpec((tm, RW_GATE_RANK), lambda i: (i, 0)), vec, vec, vec,
                               pl.BlockSpec((RW_GATE_RANK, w), lambda i: (0, 0)),
                               pl.BlockSpec((w, w), lambda i: (0, 0))],
        out_specs=wide,
        out_shape=jax.ShapeDtypeStruct((n, w), BF16),
        compiler_params=_cparams(("parallel",)),
        name="rwkv_readout",
    )(o_f, o_b, r, kd_f, kd_b, v, gd, gn_w.reshape(1, w), gn_b.reshape(1, w), r_k.reshape(1, w), g2,
      _group_ones(w, RW_HEAD_DIM))


def _s5_scan_body(u_ref, bre_ref, bim_ref, are_ref, aim_ref, cre_ref, cim_ref, h0_ref, y_ref, ht_ref,
                  dre, dim, *, reverse, want_out):
    ci = pl.program_id(0)

    @pl.when(ci == 0)
    def _():
        ht_ref[...] = h0_ref[...]

    tt, nb, wu = u_ref.shape
    nblk = wu // LANES
    sw = S5_LANES // nblk
    u2 = u_ref[...].reshape(tt * nb, wu).astype(BF16)
    for c in range(nblk):
        uc = u2[:, c * LANES:(c + 1) * LANES]
        dre[:, :, c * sw:(c + 1) * sw] = jnp.dot(uc, bre_ref[c], preferred_element_type=F32).reshape(tt, nb, sw)
        dim[:, :, c * sw:(c + 1) * sw] = jnp.dot(uc, bim_ref[c], preferred_element_type=F32).reshape(tt, nb, sw)
    lw = 1024
    for c in range(S5_LANES // lw):
        ls = slice(c * lw, (c + 1) * lw)
        ar = jnp.broadcast_to(are_ref[:, ls], (nb, lw))
        ai = jnp.broadcast_to(aim_ref[:, ls], (nb, lw))

        def step(s, carry, ls=ls, ar=ar, ai=ai):
            t = (tt - 1 - s) if reverse else s
            hr, hi = carry
            nr = ar * hr - ai * hi + dre[t, :, ls]
            ni = ar * hi + ai * hr + dim[t, :, ls]
            dre[t, :, ls] = nr
            dim[t, :, ls] = ni
            return nr, ni

        hr, hi = lax.fori_loop(0, tt, step, (ht_ref[0, :, ls], ht_ref[1, :, ls]), unroll=2)
        ht_ref[0, :, ls] = hr
        ht_ref[1, :, ls] = hi
    if want_out:
        xr = dre[...].reshape(tt * nb, S5_LANES).astype(BF16)
        xi = dim[...].reshape(tt * nb, S5_LANES).astype(BF16)
        for c in range(nblk):
            yc = (jnp.dot(xr[:, c * sw:(c + 1) * sw], cre_ref[c], preferred_element_type=F32)
                  - jnp.dot(xi[:, c * sw:(c + 1) * sw], cim_ref[c], preferred_element_type=F32))
            y_ref[:, :, c * LANES:(c + 1) * LANES] = yc.reshape(tt, nb, LANES)
    else:
        y_ref[...] = jnp.zeros_like(y_ref)


def _s5_scan(u_tm, h0, p, *, reverse, want_out=True):
    t_len, nb, wu = u_tm.shape
    tt = _tile(t_len, S5_CHUNK)
    nch = t_len // tt
    chunk = (lambda c: (nch - 1 - c, 0, 0)) if reverse else (lambda c: (c, 0, 0))
    const3 = lambda a: pl.BlockSpec(a.shape, lambda c: (0, 0, 0))
    const2 = lambda a: pl.BlockSpec(a.shape, lambda c: (0, 0))
    return pl.pallas_call(
        functools.partial(_s5_scan_body, reverse=reverse, want_out=want_out),
        grid=(nch,),
        in_specs=[pl.BlockSpec((tt, nb, wu), chunk), const3(p["b_re"]), const3(p["b_im"]),
                  const2(p["a_re"]), const2(p["a_im"]), const3(p["c_re"]), const3(p["c_im"]), const3(h0)],
        out_specs=[pl.BlockSpec((tt, nb, wu), chunk), const3(h0)],
        out_shape=[jax.ShapeDtypeStruct(u_tm.shape, F32), jax.ShapeDtypeStruct(h0.shape, F32)],
        scratch_shapes=[pltpu.VMEM((tt, nb, S5_LANES), F32), pltpu.VMEM((tt, nb, S5_LANES), F32)],
        compiler_params=_cparams(("arbitrary",)),
        name="s5_scan_rev" if reverse else "s5_scan_fwd",
    )(u_tm, p["b_re"], p["b_im"], p["a_re"], p["a_im"], p["c_re"], p["c_im"], h0)


def _s5_dir_params(lam_re, lam_im, log_dt, b_re, b_im, c_re, c_im):
    g, pdim = lam_re.shape
    dt = jnp.exp(log_dt.astype(F32))[:, None]
    mag = jnp.exp(lam_re * dt)
    abar_re = mag * jnp.cos(lam_im * dt)
    abar_im = mag * jnp.sin(lam_im * dt)
    den = lam_re * lam_re + lam_im * lam_im
    nr = abar_re - 1.0
    g_re = (nr * lam_re + abar_im * lam_im) / den
    g_im = (abar_im * lam_re - nr * lam_im) / den
    bb_re = g_re[:, :, None] * b_re - g_im[:, :, None] * b_im
    bb_im = g_re[:, :, None] * b_im + g_im[:, :, None] * b_re
    gpb = LANES // S5_GROUP
    nblk = g // gpb
    eye = jnp.eye(gpb, dtype=F32)

    def drive_mat(bb):
        x = bb.reshape(nblk, gpb, pdim, S5_GROUP)
        x = jnp.einsum("cgph,gk->cghkp", x, eye)
        return x.reshape(nblk, gpb * S5_GROUP, gpb * pdim).astype(BF16)

    def read_mat(cc):
        x = cc.reshape(nblk, gpb, S5_GROUP, pdim)
        x = jnp.einsum("cghp,gk->cgpkh", x, eye)
        return x.reshape(nblk, gpb * pdim, gpb * S5_GROUP).astype(BF16)

    return {"a_re": abar_re.reshape(1, g * pdim), "a_im": abar_im.reshape(1, g * pdim),
            "b_re": drive_mat(bb_re), "b_im": drive_mat(bb_im),
            "c_re": read_mat(c_re.astype(F32)), "c_im": read_mat(c_im.astype(F32))}


def _s5_glu_body(u_ref, yf_ref, yb_ref, d_ref, w_ref, b_ref, o_ref):
    y = u_ref[...] * d_ref[...] + yf_ref[...] + yb_ref[...]
    y = _gelu_tanh(y)
    z = _dot(y, w_ref[...]) + b_ref[...]
    o_ref[...] = (y * _sigmoid(z)).astype(o_ref.dtype)


def _s5_glu(u, y_f, y_b, d_skip, glu_w, glu_b):
    n, w = u.shape
    tm = _tile(n, 512)
    wide = pl.BlockSpec((tm, w), lambda i: (i, 0))
    vec = pl.BlockSpec((1, w), lambda i: (0, 0))
    return pl.pallas_call(
        _s5_glu_body,
        grid=(n // tm,),
        in_specs=[wide, wide, wide, vec, pl.BlockSpec((w, w), lambda i: (0, 0)), vec],
        out_specs=wide,
        out_shape=jax.ShapeDtypeStruct((n, w), BF16),
        compiler_params=_cparams(("parallel",)),
        name="s5_glu",
    )(u, y_f, y_b, d_skip.reshape(1, w), glu_w, glu_b.reshape(1, w))


def _merge_body(ya_ref, yr_ref, ys_ref, wa_ref, wr_ref, ws_ref, ga_ref, gr_ref, gs_ref, o_ref):
    m = (ga_ref[...].astype(F32) * jnp.dot(ya_ref[...], wa_ref[...], preferred_element_type=F32)
         + gr_ref[...].astype(F32) * jnp.dot(yr_ref[...], wr_ref[...], preferred_element_type=F32)
         + gs_ref[...].astype(F32) * jnp.dot(ys_ref[...], ws_ref[...], preferred_element_type=F32))
    o_ref[...] = m.astype(o_ref.dtype)


def _merge(y_a, y_r, y_s_tm, gates, w_a, w_r, w_s, seq_len, batch):
    n = y_a.shape[0]
    d = w_a.shape[1]
    tm = _tile(seq_len, 512)
    tn = _tile(d, 1024, LANES)
    nt = seq_len // tm
    nd = d // tn
    row = lambda i, j: (i, 0)
    return pl.pallas_call(
        _merge_body,
        grid=(n // tm, nd),
        in_specs=[pl.BlockSpec((tm, y_a.shape[1]), row),
                  pl.BlockSpec((tm, y_r.shape[1]), row),
                  pl.BlockSpec((tm, S5_WIDTH), lambda i, j: (i % nt, i // nt)),
                  pl.BlockSpec((w_a.shape[0], tn), lambda i, j: (0, j)),
                  pl.BlockSpec((w_r.shape[0], tn), lambda i, j: (0, j)),
                  pl.BlockSpec((w_s.shape[0], tn), lambda i, j: (0, j)),
                  pl.BlockSpec((tm, tn), lambda i, j: (i, j)),
                  pl.BlockSpec((tm, tn), lambda i, j: (i, nd + j)),
                  pl.BlockSpec((tm, tn), lambda i, j: (i, 2 * nd + j))],
        out_specs=pl.BlockSpec((tm, tn), lambda i, j: (i, j)),
        out_shape=jax.ShapeDtypeStruct((n, d), BF16),
        compiler_params=_cparams(("parallel", "arbitrary")),
        name="merge",
    )(y_a, y_r, y_s_tm, w_a, w_r, w_s, gates, gates, gates)


def _moe_plan_body(comb_ref, cnt_ref, pos_ref, w_ref, te_ref, carry_ref, *, tile_rows, dummy_row):
    i = pl.program_id(0)
    ne, tm = comb_ref.shape
    tiles = jnp.floor((cnt_ref[...] + (tile_rows - 1)) * (1.0 / tile_rows))
    er = lax.broadcasted_iota(jnp.int32, (ne, ne), 0)
    ec = lax.broadcasted_iota(jnp.int32, (ne, ne), 1)
    t_hi, t_lo = _hilo(tiles)
    lower = (ec < er).astype(BF16)
    off_tiles = (jnp.dot(lower, t_hi, preferred_element_type=F32)
                 + jnp.dot(lower, t_lo, preferred_element_type=F32))

    @pl.when(i == 0)
    def _():
        carry_ref[...] = jnp.zeros_like(carry_ref)
        ntp = te_ref.shape[1]
        end_tiles = (off_tiles + tiles)[:, 0:1]
        tile_idx = lax.broadcasted_iota(jnp.int32, (1, ntp), 1).astype(F32)
        expert = jnp.sum((end_tiles <= tile_idx).astype(F32), axis=0, keepdims=True)
        expert = jnp.minimum(expert, ne - 1.0)
        valid = (tile_idx < jnp.max(end_tiles, axis=0, keepdims=True)).astype(F32)
        cnt = cnt_ref[:, 0:1]
        on_lane = lax.broadcasted_iota(jnp.int32, (ne, ntp), 0) == lax.broadcasted_iota(jnp.int32, (ne, ntp), 1)
        to_lanes = lambda col: jnp.sum(jnp.where(on_lane, col, 0.0), axis=0, keepdims=True)
        pad_start = to_lanes(off_tiles[:, 0:1] * tile_rows + cnt)
        pad_len = to_lanes(tiles[:, 0:1] * tile_rows - cnt)
        r8 = lax.broadcasted_iota(jnp.int32, te_ref.shape, 0)
        rows = jnp.where(r8 == 0, expert, jnp.where(r8 == 1, valid, jnp.where(r8 == 2, pad_start,
                                                                               jnp.where(r8 == 3, pad_len, 0.0))))
        te_ref[...] = rows.astype(jnp.int32)

    comb = comb_ref[...]
    chosen = comb > 0.0
    chf = chosen.astype(BF16)
    tr = lax.broadcasted_iota(jnp.int32, (tm, tm), 0)
    tc = lax.broadcasted_iota(jnp.int32, (tm, tm), 1)
    rank = jnp.dot(chf, (tr < tc).astype(BF16), preferred_element_type=F32)
    pos = off_tiles[:, 0:1] * tile_rows + carry_ref[:, 0:1] + rank
    carry_ref[...] += jnp.sum(chosen.astype(F32), axis=1, keepdims=True)
    eidx = lax.broadcasted_iota(jnp.int32, (ne, tm), 0)
    tok = lax.broadcasted_iota(jnp.int32, (1, tm), 1).astype(F32)
    remaining = chosen
    pos_rows, w_rows = [], []
    for k in range(TOP_K):
        first = jnp.min(jnp.where(remaining, eidx, ne), axis=0, keepdims=True)
        pick = eidx == first
        pos_k = jnp.sum(jnp.where(pick, pos, 0.0), axis=0, keepdims=True)
        pos_rows.append(jnp.where(first < ne, pos_k, float(dummy_row + k * tm) + tok))
        w_rows.append(jnp.sum(jnp.where(pick, comb, 0.0), axis=0, keepdims=True))
        remaining = jnp.logical_and(remaining, jnp.logical_not(pick))
    pos_ref[...] = jnp.concatenate(pos_rows, axis=0).astype(jnp.int32)
    wmat = jnp.concatenate(w_rows + [jnp.zeros((LANES - TOP_K, tm), F32)], axis=0)
    w_ref[...] = wmat.T


def _moe_token_tile(n):
    return _tile(n, 256, LANES)


def _moe_plan(comb_t, counts, n_tiles, dummy_row):
    ne, n = comb_t.shape
    tm = _moe_token_tile(n)
    ntp = -(-n_tiles // LANES) * LANES
    return pl.pallas_call(
        functools.partial(_moe_plan_body, tile_rows=MOE_TILE, dummy_row=dummy_row),
        grid=(n // tm,),
        in_specs=[pl.BlockSpec((ne, tm), lambda i: (0, i)),
                  pl.BlockSpec((ne, LANES), lambda i: (0, 0))],
        out_specs=[pl.BlockSpec((TOP_K, tm), lambda i: (0, i)),
                   pl.BlockSpec((tm, LANES), lambda i: (i, 0)),
                   pl.BlockSpec((SUBLANES, ntp), lambda i: (0, 0))],
        out_shape=[jax.ShapeDtypeStruct((TOP_K, n), jnp.int32), jax.ShapeDtypeStruct((n, LANES), F32),
                   jax.ShapeDtypeStruct((SUBLANES, ntp), jnp.int32)],
        scratch_shapes=[pltpu.VMEM((ne, LANES), F32)],
        compiler_params=_cparams(("arbitrary",)),
        name="moe_plan",
    )(comb_t, counts)


def _start_row_copies(pos_ref, tm, make_copy):
    def start(t, c):
        for k in range(TOP_K):
            make_copy(t, k, pos_ref[k, t]).start(priority=k % 2)
        return c

    lax.fori_loop(0, tm, start, 0)


def _moe_dispatch_body(pos_ref, plan_ref, hp_ref, xg_hbm, zrow, sem, zsem, *, tm, experts_per_step, n_experts):
    _start_row_copies(pos_ref, tm, lambda t, k, p: pltpu.make_async_copy(
        hp_ref.at[pl.ds(t, 1)], xg_hbm.at[pl.ds(p, 1)], sem))
    zrow[...] = jnp.zeros_like(zrow)
    zero_copy = lambda row: pltpu.make_async_copy(zrow.at[pl.ds(0, 1)], xg_hbm.at[pl.ds(row, 1)], zsem)
    for j in range(experts_per_step):
        e = pl.program_id(0) * experts_per_step + j

        @pl.when(e < n_experts)
        def _():
            start = plan_ref[2, e]
            count = plan_ref[3, e]
            lax.fori_loop(0, count, lambda r, c: (zero_copy(start + r).start(), c)[1], 0)
            lax.fori_loop(0, count, lambda r, c: (zero_copy(start + r).wait(), c)[1], 0)

    for _ in range(TOP_K):
        pltpu.make_async_copy(hp_ref, hp_ref, sem).wait()


def _moe_dispatch(pos, plan, hp, total_rows, n_experts):
    n, half = hp.shape
    tm = _moe_token_tile(n)
    steps = n // tm
    return pl.pallas_call(
        functools.partial(_moe_dispatch_body, tm=tm, experts_per_step=-(-n_experts // steps), n_experts=n_experts),
        grid=(steps,),
        in_specs=[pl.BlockSpec((TOP_K, tm), lambda i: (0, i), memory_space=pltpu.SMEM),
                  pl.BlockSpec(memory_space=pltpu.SMEM),
                  pl.BlockSpec((tm, half), lambda i: (i, 0))],
        out_specs=pl.BlockSpec(memory_space=pl.ANY),
        out_shape=jax.ShapeDtypeStruct((total_rows, half), jnp.uint32),
        scratch_shapes=[pltpu.VMEM((SUBLANES, half), jnp.uint32), pltpu.SemaphoreType.DMA, pltpu.SemaphoreType.DMA],
        compiler_params=_cparams(("arbitrary",)),
        name="moe_dispatch",
    )(pos, plan, hp)


def _moe_ffn_body(te_ref, tv_ref, xg_ref, wg_ref, wu_ref, wd_ref, ys_ref, wg_bf, wu_bf, wd_bf):
    j = pl.program_id(0)
    new_expert = jnp.logical_or(j == 0, te_ref[j] != te_ref[jnp.maximum(j - 1, 0)])

    @pl.when(new_expert)
    def _():
        wg_bf[...] = wg_ref[0, 0].astype(BF16)
        wu_bf[...] = wu_ref[0, 0].astype(BF16)
        wd_bf[...] = wd_ref[0, 0].astype(BF16)

    @pl.when(tv_ref[j] != 0)
    def _():
        lo, hi = _unpack_halves(xg_ref[...])
        half = lo.shape[1]
        hg = _dot(lo, wg_bf[:half, :]) + _dot(hi, wg_bf[half:, :])
        hu = _dot(lo, wu_bf[:half, :]) + _dot(hi, wu_bf[half:, :])
        act = hg * _sigmoid(hg) * hu
        ys_ref[...] = _pack_halves(_dot(act, wd_bf[...]))

    @pl.when(tv_ref[j] == 0)
    def _():
        ys_ref[...] = jnp.zeros_like(ys_ref)


def _moe_ffn(tile_expert, tile_valid, xg, w_gate, w_up, w_down, layer):
    rows, half = xg.shape
    _, ne, d, ff = w_gate.shape
    blk = pl.BlockSpec((MOE_TILE, half), lambda j, te, tv: (j, 0))
    return pl.pallas_call(
        _moe_ffn_body,
        grid_spec=pltpu.PrefetchScalarGridSpec(
            num_scalar_prefetch=2,
            grid=(rows // MOE_TILE,),
            in_specs=[blk,
                      pl.BlockSpec((1, 1, d, ff), lambda j, te, tv: (layer, te[j], 0, 0)),
                      pl.BlockSpec((1, 1, d, ff), lambda j, te, tv: (layer, te[j], 0, 0)),
                      pl.BlockSpec((1, 1, ff, d), lambda j, te, tv: (layer, te[j], 0, 0))],
            out_specs=blk,
            scratch_shapes=[pltpu.VMEM((d, ff), BF16), pltpu.VMEM((d, ff), BF16), pltpu.VMEM((ff, d), BF16)]),
        out_shape=jax.ShapeDtypeStruct((rows, half), jnp.uint32),
        compiler_params=_cparams(("arbitrary",)),
        name="moe_ffn",
    )(tile_expert, tile_valid, xg, w_gate, w_up, w_down)


def _moe_combine_body(pos_ref, w_ref, ys_hbm, o_ref, buf, sem, *, tm):
    _start_row_copies(pos_ref, tm, lambda t, k, p: pltpu.make_async_copy(
        ys_hbm.at[pl.ds(p, 1)], buf.at[k, pl.ds(t, 1)], sem))
    pltpu.make_async_copy(buf, buf, sem).wait()
    w = w_ref[...]
    half = buf.shape[2]
    acc_lo = jnp.zeros((tm, half), F32)
    acc_hi = jnp.zeros((tm, half), F32)
    for k in range(TOP_K):
        lo, hi = _unpack_halves(buf[k])
        wk = w[:, k:k + 1]
        acc_lo = acc_lo + wk * lo
        acc_hi = acc_hi + wk * hi
    o_ref[:, :half] = acc_lo
    o_ref[:, half:] = acc_hi


def _moe_combine(pos, w_tok, ys):
    n = w_tok.shape[0]
    half = ys.shape[1]
    tm = _tile(n, 128, LANES)
    return pl.pallas_call(
        functools.partial(_moe_combine_body, tm=tm),
        grid=(n // tm,),
        in_specs=[pl.BlockSpec((TOP_K, tm), lambda i: (0, i), memory_space=pltpu.SMEM),
                  pl.BlockSpec((tm, LANES), lambda i: (i, 0)),
                  pl.BlockSpec(memory_space=pl.ANY)],
        out_specs=pl.BlockSpec((tm, 2 * half), lambda i: (i, 0)),
        out_shape=jax.ShapeDtypeStruct((n, 2 * half), F32),
        scratch_shapes=[pltpu.VMEM((TOP_K, tm, half), jnp.uint32), pltpu.SemaphoreType.DMA],
        compiler_params=_cparams(("arbitrary",)),
        name="moe_combine",
    )(pos, w_tok, ys)


def _moe_routed(hp, comb_t, counts, w_gate, w_up, w_down, layer):
    n = hp.shape[0]
    ne = w_gate.shape[1]
    run_tiles = (n * TOP_K) // MOE_TILE + ne
    dummy_row = run_tiles * MOE_TILE
    spare_tiles = -(-(TOP_K * _moe_token_tile(n)) // MOE_TILE)
    n_tiles = run_tiles + spare_tiles
    pos, w_tok, te = _moe_plan(comb_t, counts, n_tiles, dummy_row)
    xg = _moe_dispatch(pos, te, hp, n_tiles * MOE_TILE, ne)
    ys = _moe_ffn(te[0, :n_tiles], te[1, :n_tiles], xg, w_gate, w_up, w_down, layer)
    return _moe_combine(pos, w_tok, ys)


def _swiglu_up_body(h_ref, wg_ref, wu_ref, o_ref):
    h = h_ref[...]
    g = jnp.dot(h, wg_ref[...], preferred_element_type=F32)
    u = jnp.dot(h, wu_ref[...], preferred_element_type=F32)
    o_ref[...] = (g * _sigmoid(g) * u).astype(o_ref.dtype)


def _moe_shared_residual(h, w_gate, w_up, w_down, f_routed, f_row0, x2, mod, gate_k, rows_per_batch):
    n, d = x2.shape
    ff = w_up.shape[1]
    tm = _tile(n, 512)
    tn = _tile(ff, 512, LANES)
    act = pl.pallas_call(
        _swiglu_up_body,
        grid=(n // tm, ff // tn),
        in_specs=[pl.BlockSpec((tm, d), lambda i, j: (i, 0)),
                  pl.BlockSpec((d, tn), lambda i, j: (0, j)),
                  pl.BlockSpec((d, tn), lambda i, j: (0, j))],
        out_specs=pl.BlockSpec((tm, tn), lambda i, j: (i, j)),
        out_shape=jax.ShapeDtypeStruct((n, ff), BF16),
        compiler_params=_cparams(("parallel", "arbitrary")),
        name="shared_up",
    )(h, w_gate, w_up)
    tm = _tile(n if rows_per_batch is None else rows_per_batch, 512)
    tn = _tile(d, 1024, LANES)
    nd = d // tn
    mrow = _mod_row_fn(rows_per_batch, tm)
    blk = pl.BlockSpec((tm, tn), lambda i, j: (i, j))
    assert f_row0 % tm == 0
    f_blk = pl.BlockSpec((tm, tn), lambda i, j: (f_row0 // tm + i, j))
    return _matmul(act, w_down, F32, tm=tm, tn=tn,
                   epilogue=lambda acc, fr, xb, g: xb + g[0] * (acc + fr),
                   extras=[(f_routed, f_blk), (x2, blk),
                           (mod, pl.BlockSpec((1, 1, tn), lambda i, j: (mrow(i), 0, gate_k * nd + j)))],
                   name="shared_down")


def _token_mixer(h, hc, lw, lam_init, tabs, need_ctx, batch, t_len, c_len):
    cos, sin = tabs
    flat_rows = _tile(hc.shape[0], 512)
    flat_tabs = (jnp.ones((flat_rows, LANES), F32), jnp.zeros((flat_rows, LANES), F32))

    def project(hh, seq_len, full, positional):
        n = hh.shape[0]
        rope = (cos, sin, seq_len) if positional else flat_tabs + (None,)
        out = {}
        out["k"] = _proj_qk(hh, lw["w_k"], lw["da_k_norm"], rope[0], rope[1], 1.0, rope[2], "proj_k")
        out["v"] = _matmul(hh, lw["w_v"], BF16, name="proj_v")
        out["rw"] = _matmul(hh, lw["w_rw"], F32, name="proj_rw")
        tm = _tile(seq_len, 512)
        nt = seq_len // tm
        tn = S5_WIDTH
        out["s5"] = _matmul(
            hh, lw["w_s5"], F32, tn=tn, grid_m=n // tm,
            a_spec=pl.BlockSpec((tm, hh.shape[1]), lambda i, j: (i, 0)),
            out_spec=pl.BlockSpec((tm, tn), lambda i, j: (i % nt, i // nt)),
            out_shape=(seq_len, batch * S5_WIDTH), name="proj_s5").reshape(seq_len, batch, S5_WIDTH)
        if full:
            out["q"] = _proj_qk(hh, lw["w_q"], lw["da_q_norm"], rope[0], rope[1], DA_Q_SCALE, rope[2], "proj_q")
            out["gates"] = _matmul(hh, lw["w_gates"], BF16, epilogue=_sigmoid, name="proj_gates")
        return out

    pl_ = project(h, t_len, True, True)
    pc_ = project(hc, c_len, need_ctx, False)

    y_a = _diff_attention(pl_["q"], pl_["k"], pl_["v"], pc_["k"], pc_["v"], lw["da_lambda_q"], lw["da_lambda_k"],
                          lw["da_subln"], lam_init, batch)
    y_ac = None
    if need_ctx:
        y_ac = _diff_attention(pc_["q"], None, None, pc_["k"], pc_["v"], lw["da_lambda_q"], lw["da_lambda_k"],
                               lw["da_subln"], lam_init, batch)

    def prep(p_rw, seq_len):
        names = ("lw_f", "lw_b", "kk", "kka_f", "kka_b", "kd_f", "kd_b", "v", "r", "gd")
        vals = _rwkv_prep(p_rw, seq_len, lw["rw_conv"], lw["rw_lora"], lw["rw_w0a0"], lw["rw_k_k"], lw["rw_k_a"])
        return dict(zip(names, vals))

    tl, tc = prep(pl_["rw"], t_len), prep(pc_["rw"], c_len)
    gw = RW_GROUP_HEADS * RW_HEAD_DIM
    s_zero = jnp.zeros((batch * (RW_WIDTH // gw), gw, gw), F32)
    ocf, ocb, s_ctx_f, s_ctx_b = _rwkv_scan(tc, s_zero, s_zero, batch, want_out=need_ctx)
    olf, olb, _, _ = _rwkv_scan(tl, s_ctx_f, s_ctx_b, batch)
    o_lat, o_ctx = {"f": olf, "b": olb}, {"f": ocf, "b": ocb}
    ro = lambda o, t: _rwkv_readout(o["f"], o["b"], t["r"], t["kd_f"], t["kd_b"], t["v"], t["gd"],
                                    lw["rw_gn_w"], lw["rw_gn_b"], lw["rw_r_k"], lw["rw_g2"])
    y_r = ro(o_lat, tl)
    y_rc = ro(o_ctx, tc) if need_ctx else None

    h_zero = jnp.zeros((2, batch, S5_LANES), F32)
    ys_lat, ys_ctx = {}, {}
    for d, rev in (("f", False), ("b", True)):
        yc, h_ctx = _s5_scan(pc_["s5"], h_zero, lw["s5_" + d], reverse=rev, want_out=need_ctx)
        yl, _ = _s5_scan(pl_["s5"], h_ctx, lw["s5_" + d], reverse=rev)
        ys_lat[d], ys_ctx[d] = yl, yc
    flat = lambda a: a.reshape(a.shape[0] * batch, S5_WIDTH)
    glu = lambda p, ys, seq: _s5_glu(flat(p["s5"]), flat(ys["f"]), flat(ys["b"]), lw["s5_d"], lw["s5_glu_w"],
                                     lw["s5_glu_b"]).reshape(seq, batch * S5_WIDTH)
    y_s = glu(pl_, ys_lat, t_len)
    y_sc = glu(pc_, ys_ctx, c_len) if need_ctx else None

    m = _merge(y_a, y_r, y_s, pl_["gates"], lw["w_branch_a"], lw["w_branch_r"], lw["w_branch_s"], t_len, batch)
    m_c = None
    if need_ctx:
        m_c = _merge(y_ac, y_rc, y_sc, pc_["gates"], lw["w_branch_a"], lw["w_branch_r"], lw["w_branch_s"],
                     c_len, batch)
    return m, m_c


def _out_proj_residual(m, w_out, x2, mod, gate_k, rows_per_batch):
    n, d = x2.shape
    tm = _tile(n if rows_per_batch is None else rows_per_batch, 512)
    tn = _tile(d, 1024, LANES)
    nd = d // tn
    mrow = _mod_row_fn(rows_per_batch, tm)

    def epilogue(acc, xb, g):
        return xb + g[0] * acc

    return _matmul(m, w_out, F32, tm=tm, tn=tn, epilogue=epilogue,
                   extras=[(x2, pl.BlockSpec((tm, tn), lambda i, j: (i, j))),
                           (mod, pl.BlockSpec((1, 1, tn), lambda i, j: (mrow(i), 0, gate_k * nd + j)))],
                   name="out_proj")


def _prepare_layer(i, p):
    w_in = p["w_in"][i]
    c0 = DA_WIDTH
    c1 = 2 * DA_WIDTH
    c2 = c1 + RW_STATE_COLS
    c3 = c2 + S5_WIDTH
    c4 = c3 + DA_WIDTH
    c5 = c4 + RW_OUT_COLS
    bf = lambda a: a.astype(BF16)
    lw = {
        "w_k": bf(w_in[:, :c0]), "w_v": bf(w_in[:, c0:c1]),
        "w_rw": bf(jnp.concatenate([w_in[:, c1:c2], w_in[:, c4:c5]], axis=1)),
        "w_s5": bf(w_in[:, c2:c3]), "w_q": bf(w_in[:, c3:c4]), "w_gates": bf(w_in[:, c5:]),
    }
    for name in ("da_q_norm", "da_k_norm", "da_lambda_q", "da_lambda_k", "da_subln", "rw_conv", "rw_k_k", "rw_k_a",
                 "rw_gn_w", "rw_gn_b", "s5_d", "s5_glu_b", "router_bias"):
        lw[name] = p[name][i].astype(F32)
    w = RW_WIDTH
    lora = jnp.zeros((2 * RW_DECAY_RANK + 2 * RW_A_RANK, 4 * w), F32)
    r0 = 0
    for blk, src in enumerate((p["rw_w2"][i][0], p["rw_w2"][i][1], p["rw_a2"][i][0], p["rw_a2"][i][1])):
        lora = lora.at[r0:r0 + src.shape[0], blk * w:(blk + 1) * w].set(src.astype(F32))
        r0 += src.shape[0]
    lw["rw_lora"] = jnp.stack(_hilo(lora))
    lw["rw_w0a0"] = jnp.concatenate([p["rw_w0"][i][0], p["rw_w0"][i][1], p["rw_a0"][i][0], p["rw_a0"][i][1]]
                                    ).astype(F32).reshape(1, 4 * w)
    lw["rw_r_k"] = p["rw_r_k"][i].astype(F32).reshape(w)
    lw["rw_g2"] = bf(p["rw_g2"][i])
    for d, name in enumerate(("s5_f", "s5_b")):
        lw[name] = _s5_dir_params(p["s5_lambda_re"][i][d], p["s5_lambda_im"][i][d], p["s5_log_dt"][i][d],
                                  p["s5_b_re"][i].astype(F32), p["s5_b_im"][i].astype(F32),
                                  p["s5_c_re"][i][d], p["s5_c_im"][i][d])
    lw["s5_glu_w"] = bf(p["s5_glu_w"][i])
    for name in ("w_branch_a", "w_branch_r", "w_branch_s", "w_out"):
        lw[name] = bf(p[name][i])
    for name in ("exp_w_gate", "exp_w_up", "exp_w_down"):
        lw[name] = p[name]
    lw["layer"] = i
    d_model = w_in.shape[0]
    rw_t = p["router_w"][i].astype(F32).T
    lw["router_w_t"] = jnp.concatenate([rw_t, jnp.zeros((LANES - N_EXPERTS, d_model), F32)], axis=0)
    for name in ("sh_w_gate", "sh_w_up", "sh_w_down"):
        lw[name] = bf(p[name][i])
    return lw


def kernel(x, c, ctx, c_ctx, ada_w, ada_b, w_in, da_q_norm, da_k_norm, da_lambda_q, da_lambda_k, da_subln, rw_conv, rw_w0, rw_w2, rw_a0, rw_a2, rw_g2, rw_k_k, rw_k_a, rw_r_k, rw_gn_w, rw_gn_b, s5_lambda_re, s5_lambda_im, s5_log_dt, s5_b_re, s5_b_im, s5_c_re, s5_c_im, s5_d, s5_glu_w, s5_glu_b, w_branch_a, w_branch_r, w_branch_s, w_out, router_w, router_bias, exp_w_gate, exp_w_up, exp_w_down, sh_w_gate, sh_w_up, sh_w_down):
    params = dict(w_in=w_in, da_q_norm=da_q_norm, da_k_norm=da_k_norm, da_lambda_q=da_lambda_q,
                  da_lambda_k=da_lambda_k, da_subln=da_subln, rw_conv=rw_conv, rw_w0=rw_w0, rw_w2=rw_w2,
                  rw_a0=rw_a0, rw_a2=rw_a2, rw_g2=rw_g2, rw_k_k=rw_k_k, rw_k_a=rw_k_a, rw_r_k=rw_r_k,
                  rw_gn_w=rw_gn_w, rw_gn_b=rw_gn_b, s5_lambda_re=s5_lambda_re, s5_lambda_im=s5_lambda_im,
                  s5_log_dt=s5_log_dt, s5_b_re=s5_b_re, s5_b_im=s5_b_im, s5_c_re=s5_c_re, s5_c_im=s5_c_im,
                  s5_d=s5_d, s5_glu_w=s5_glu_w, s5_glu_b=s5_glu_b, w_branch_a=w_branch_a, w_branch_r=w_branch_r,
                  w_branch_s=w_branch_s, w_out=w_out, router_w=router_w, router_bias=router_bias,
                  exp_w_gate=exp_w_gate, exp_w_up=exp_w_up, exp_w_down=exp_w_down, sh_w_gate=sh_w_gate,
                  sh_w_up=sh_w_up, sh_w_down=sh_w_down)
    batch, t_len, d_model = x.shape
    c_len = ctx.shape[1]
    depth = ada_w.shape[0]
    assert batch <= MOD_CTX_ROW
    tabs = _rope_tables(t_len)
    cvec = jnp.zeros((2 * SUBLANES, d_model), F32).at[:batch].set(c.astype(F32)).at[MOD_CTX_ROW].set(c_ctx.astype(F32))
    x2 = x.astype(F32).reshape(batch * t_len, d_model)
    ctx2 = ctx.astype(F32).reshape(batch * c_len, d_model)
    for i in range(depth):
        lw = _prepare_layer(i, params)
        need_ctx = i < depth - 1
        lam_init = 0.8 - 0.6 * math.exp(-0.3 * i)
        mod = _ada_table(cvec, ada_w, ada_b, i)
        h = _modulate(x2, mod, 0, 1, t_len)
        hc = _modulate(ctx2, mod, 0, 1, None)
        m, m_c = _token_mixer(h, hc, lw, lam_init, tabs, need_ctx, batch, t_len, c_len)
        x2 = _out_proj_residual(m, lw["w_out"], x2, mod, 2, t_len)
        streams = [(x2, t_len)]
        if need_ctx:
            ctx2 = _out_proj_residual(m_c, lw["w_out"], ctx2, mod, 2, None)
            streams.append((ctx2, None))
        routed = [_modulate_route(xs, mod, 3, 4, rpb, lw["router_w_t"], lw["router_bias"]) for xs, rpb in streams]
        f_r = _moe_routed(jnp.concatenate([r[1] for r in routed], axis=0),
                          jnp.concatenate([r[2] for r in routed], axis=1),
                          sum(r[3] for r in routed),
                          lw["exp_w_gate"], lw["exp_w_up"], lw["exp_w_down"], lw["layer"])
        outs, row0 = [], 0
        for (xs, rpb), r in zip(streams, routed):
            outs.append(_moe_shared_residual(r[0], lw["sh_w_gate"], lw["sh_w_up"], lw["sh_w_down"], f_r, row0,
                                             xs, mod, 5, rpb))
            row0 += xs.shape[0]
        x2 = outs[0]
        if need_ctx:
            ctx2 = outs[1]
    return x2.reshape(batch, t_len, d_model).astype(x.dtype)
```

```python
import functools
import math

import jax
import jax.numpy as jnp
from jax import lax
from jax.experimental import pallas as pl
from jax.experimental.pallas import tpu as pltpu

F32 = jnp.float32
BF16 = jnp.bfloat16

GRID_W = 64
NORM_EPS = 1e-6
DA_HEADS = 16
DA_HEAD_DIM = 64
DA_V_DIM = 2 * DA_HEAD_DIM
DA_WIDTH = DA_HEADS * DA_V_DIM
DA_SCALE = DA_HEAD_DIM ** -0.5
DA_Q_SCALE = DA_SCALE * math.log2(math.e)
ROPE_BASE = 10000.0
RW_HEADS = 16
RW_HEAD_DIM = 64
RW_WIDTH = RW_HEADS * RW_HEAD_DIM
RW_DECAY_RANK = 64
RW_A_RANK = 64
RW_GATE_RANK = 128
RW_GN_EPS = 64e-5
RW_STATE_COLS = 2 * RW_WIDTH + 2 * RW_DECAY_RANK + 2 * RW_A_RANK
RW_OUT_COLS = RW_WIDTH + RW_GATE_RANK
S5_GROUP = 16
S5_GROUPS = 64
S5_WIDTH = S5_GROUPS * S5_GROUP
S5_STATE = 64
S5_LANES = S5_GROUPS * S5_STATE
N_EXPERTS = 64
TOP_K = 8
N_GROUPS = 8
TOPK_GROUPS = 4
EXPERT_FF = 256
ROUTED_SCALE = 2.5

LANES = 128
SUBLANES = 8
VMEM_LIMIT_BYTES = 56 * 1024 * 1024

RW_CHUNK = 64
RW_GROUP_HEADS = 4
S5_CHUNK = 64
MOD_CTX_ROW = 8
MOE_TILE = 256


def _cparams(sem):
    return pltpu.CompilerParams(dimension_semantics=sem, vmem_limit_bytes=VMEM_LIMIT_BYTES)


def _tile(n, pref, mult=SUBLANES):
    if n <= pref:
        return n
    t = (pref // mult) * mult
    while t > mult and n % t:
        t -= mult
    assert n % t == 0, (n, pref)
    return t


def _dot(a, b):
    return jnp.dot(a.astype(BF16), b.astype(BF16), preferred_element_type=F32)


def _dot_nt(a, b):
    return lax.dot_general(a.astype(BF16), b.astype(BF16), (((1,), (1,)), ((), ())),
                           preferred_element_type=F32)


def _dot_tn(a, b):
    return lax.dot_general(a.astype(BF16), b.astype(BF16), (((0,), (0,)), ((), ())),
                           preferred_element_type=F32)


def _hilo(x):
    hi = x.astype(BF16)
    lo = (x - hi.astype(F32)).astype(BF16)
    return hi, lo


def _dot_hp_lhs(a, b_exact):
    hi, lo = _hilo(a)
    return (jnp.dot(hi, b_exact, preferred_element_type=F32)
            + jnp.dot(lo, b_exact, preferred_element_type=F32))


def _dot_hp(a, b):
    ah, al = _hilo(a)
    bh, bl = _hilo(b)
    return (jnp.dot(ah, bh, preferred_element_type=F32) + jnp.dot(al, bh, preferred_element_type=F32)
            + jnp.dot(ah, bl, preferred_element_type=F32))


def _sigmoid(x):
    return 1.0 / (1.0 + jnp.exp(-x))


def _softplus(x):
    return jnp.maximum(x, 0.0) + jnp.log(1.0 + jnp.exp(-jnp.abs(x)))


def _gelu_tanh(x):
    c = math.sqrt(2.0 / math.pi)
    return 0.5 * x * (1.0 + jnp.tanh(c * (x + 0.044715 * (x * x * x))))


def _mm_body(a_ref, b_ref, *rest, n_extra, prologue, epilogue):
    extras = rest[:n_extra]
    o_ref = rest[n_extra]
    a = a_ref[...]
    if prologue is not None:
        a = prologue(a)
    b = b_ref[0] if len(b_ref.shape) == 3 else b_ref[...]
    acc = jnp.dot(a.astype(BF16), b.astype(BF16), preferred_element_type=F32)
    if epilogue is not None:
        acc = epilogue(acc, *[e[...] for e in extras])
    o_ref[...] = acc.astype(o_ref.dtype)


def _matmul(a, b, out_dtype, *, tm=1024, tn=1024, prologue=None, epilogue=None, extras=(),
            a_spec=None, out_spec=None, out_shape=None, grid_m=None, b_layer=None, name="matmul"):
    k, n = b.shape[-2:]
    tn = _tile(n, tn, LANES)
    if b_layer is None:
        b_spec = pl.BlockSpec((k, tn), lambda i, j: (0, j))
    else:
        b_spec = pl.BlockSpec((1, k, tn), lambda i, j: (b_layer, 0, j))
    if a_spec is None:
        m = a.shape[0]
        tm = _tile(m, tm)
        grid_m = m // tm
        a_spec = pl.BlockSpec((tm, k), lambda i, j: (i, 0))
    if out_spec is None:
        out_spec = pl.BlockSpec((tm, tn), lambda i, j: (i, j))
        out_shape = (a.shape[0], n)
    body = functools.partial(_mm_body, n_extra=len(extras), prologue=prologue, epilogue=epilogue)
    return pl.pallas_call(
        body,
        grid=(grid_m, n // tn),
        in_specs=[a_spec, b_spec] + [s for _, s in extras],
        out_specs=out_spec,
        out_shape=jax.ShapeDtypeStruct(out_shape, out_dtype),
        compiler_params=_cparams(("parallel", "arbitrary")),
        name=name,
    )(a, b, *[x for x, _ in extras])


def _ada_table(cvec, ada_w, ada_b, layer):
    d6 = ada_w.shape[2]

    def prologue(a):
        return a * _sigmoid(a)

    def epilogue(acc, bias):
        return acc + bias

    out = _matmul(cvec, ada_w, F32, tm=16, tn=512, prologue=prologue, epilogue=epilogue, b_layer=layer,
                  extras=[(ada_b[layer].reshape(1, d6), pl.BlockSpec((1, 512), lambda i, j: (0, j)))],
                  name="ada_table")
    return out.reshape(cvec.shape[0], 1, d6)


def _mod_row_fn(rows_per_batch, tm):
    if rows_per_batch is None:
        return lambda i: MOD_CTX_ROW
    nb = rows_per_batch // tm
    return lambda i: i // nb


def _modulate_body(x_ref, sh_ref, sc_ref, o_ref):
    x = x_ref[...]
    ms = jnp.mean(x * x, axis=-1, keepdims=True)
    h = x * lax.rsqrt(ms + NORM_EPS) * (1.0 + sc_ref[0]) + sh_ref[0]
    o_ref[...] = h.astype(o_ref.dtype)


def _modulate(x2, mod, shift_k, scale_k, rows_per_batch):
    n, d = x2.shape
    tm = _tile(n if rows_per_batch is None else rows_per_batch, 256)
    row = _mod_row_fn(rows_per_batch, tm)
    return pl.pallas_call(
        _modulate_body,
        grid=(n // tm,),
        in_specs=[pl.BlockSpec((tm, d), lambda i: (i, 0)),
                  pl.BlockSpec((1, 1, d), lambda i: (row(i), 0, shift_k)),
                  pl.BlockSpec((1, 1, d), lambda i: (row(i), 0, scale_k))],
        out_specs=pl.BlockSpec((tm, d), lambda i: (i, 0)),
        out_shape=jax.ShapeDtypeStruct((n, d), BF16),
        compiler_params=_cparams(("parallel",)),
        name="modulate",
    )(x2, mod, mod)


def _pack_halves(x):
    w = x.shape[1] // 2
    lo = lax.bitcast_convert_type(x[:, :w].astype(BF16).astype(F32), jnp.uint32)
    hi = lax.bitcast_convert_type(x[:, w:].astype(BF16).astype(F32), jnp.uint32)
    return lax.shift_right_logical(lo, jnp.uint32(16)) | hi


def _unpack_halves(p):
    lo = lax.bitcast_convert_type(lax.shift_left(p, jnp.uint32(16)), F32)
    hi = lax.bitcast_convert_type(p & jnp.uint32(0xFFFF0000), F32)
    return lo, hi


def _route_body(x_ref, sh_ref, sc_ref, wr_ref, bias_ref, h_ref, hp_ref, comb_ref, cnt_ref):
    x = x_ref[...]
    ms = jnp.mean(x * x, axis=-1, keepdims=True)
    h = x * lax.rsqrt(ms + NORM_EPS) * (1.0 + sc_ref[0]) + sh_ref[0]
    h_ref[...] = h.astype(h_ref.dtype)
    hp_ref[...] = _pack_halves(h)
    tm = x.shape[0]
    wr = wr_ref[...]
    hh, hl = _hilo(h)
    wh, wl = _hilo(wr)
    logits = _dot_nt(wh, hh) + _dot_nt(wl, hh) + _dot_nt(wh, hl)
    scores = _sigmoid(logits[:N_EXPERTS])
    per_group = N_EXPERTS // N_GROUPS
    sc3 = scores.reshape(N_GROUPS, per_group, tm)
    sel = sc3 + bias_ref[...]
    midx = lax.broadcasted_iota(jnp.int32, sel.shape, 1)
    neg = jnp.float32(-jnp.inf)
    m1 = jnp.max(sel, axis=1, keepdims=True)
    first = jnp.min(jnp.where(sel == m1, midx, per_group), axis=1, keepdims=True)
    m2 = jnp.max(jnp.where(midx == first, neg, sel), axis=1, keepdims=True)
    gs = (m1 + m2).reshape(N_GROUPS, tm)
    gidx = lax.broadcasted_iota(jnp.int32, gs.shape, 0)
    gmask = jnp.zeros(gs.shape, jnp.bool_)
    for _ in range(TOPK_GROUPS):
        m = jnp.max(gs, axis=0, keepdims=True)
        f = jnp.min(jnp.where(gs == m, gidx, N_GROUPS), axis=0, keepdims=True)
        pick = gidx == f
        gmask = jnp.logical_or(gmask, pick)
        gs = jnp.where(pick, neg, gs)
    val = jnp.where(gmask.reshape(N_GROUPS, 1, tm), sel, neg)
    eidx = lax.broadcasted_iota(jnp.int32, sel.shape, 0) * per_group + midx
    chosen = jnp.zeros(sel.shape, jnp.bool_)
    for _ in range(TOP_K):
        m = jnp.max(jnp.max(val, axis=1, keepdims=True), axis=0, keepdims=True)
        f = jnp.min(jnp.min(jnp.where(val == m, eidx, N_EXPERTS), axis=1, keepdims=True), axis=0, keepdims=True)
        pick = eidx == f
        chosen = jnp.logical_or(chosen, pick)
        val = jnp.where(pick, neg, val)
    w = jnp.where(chosen, sc3, 0.0)
    wsum = jnp.sum(jnp.sum(w, axis=1, keepdims=True), axis=0, keepdims=True)
    comb = (w / wsum * ROUTED_SCALE).reshape(N_EXPERTS, tm)
    comb_ref[...] = comb

    @pl.when(pl.program_id(0) == 0)
    def _():
        cnt_ref[...] = jnp.zeros_like(cnt_ref)

    cnt_ref[...] += jnp.sum((comb > 0.0).astype(F32), axis=1, keepdims=True)


def _modulate_route(x2, mod, shift_k, scale_k, rows_per_batch, router_w_t, router_bias):
    n, d = x2.shape
    tm = _tile(n if rows_per_batch is None else rows_per_batch, 256, LANES)
    row = _mod_row_fn(rows_per_batch, tm)
    return pl.pallas_call(
        _route_body,
        grid=(n // tm,),
        in_specs=[pl.BlockSpec((tm, d), lambda i: (i, 0)),
                  pl.BlockSpec((1, 1, d), lambda i: (row(i), 0, shift_k)),
                  pl.BlockSpec((1, 1, d), lambda i: (row(i), 0, scale_k)),
                  pl.BlockSpec((LANES, d), lambda i: (0, 0)),
                  pl.BlockSpec((N_GROUPS, N_EXPERTS // N_GROUPS, 1), lambda i: (0, 0, 0))],
        out_specs=[pl.BlockSpec((tm, d), lambda i: (i, 0)),
                   pl.BlockSpec((tm, d // 2), lambda i: (i, 0)),
                   pl.BlockSpec((N_EXPERTS, tm), lambda i: (0, i)),
                   pl.BlockSpec((N_EXPERTS, LANES), lambda i: (0, 0))],
        out_shape=[jax.ShapeDtypeStruct((n, d), BF16), jax.ShapeDtypeStruct((n, d // 2), jnp.uint32),
                   jax.ShapeDtypeStruct((N_EXPERTS, n), F32), jax.ShapeDtypeStruct((N_EXPERTS, LANES), F32)],
        compiler_params=_cparams(("arbitrary",)),
        name="modulate_route",
    )(x2, mod, mod, router_w_t, router_bias.reshape(N_GROUPS, N_EXPERTS // N_GROUPS, 1))


def _group_ones(width, group):
    r = lax.broadcasted_iota(jnp.int32, (width, width), 0) // group
    c = lax.broadcasted_iota(jnp.int32, (width, width), 1) // group
    return (r == c).astype(BF16)


def _qk_norm_rope(x, gain, cos, sin, scale):
    ss = _dot_hp_lhs(x * x, _group_ones(LANES, DA_HEAD_DIM))
    xn = x * lax.rsqrt(ss * (1.0 / DA_HEAD_DIM) + NORM_EPS) * gain
    lane = lax.broadcasted_iota(jnp.int32, x.shape, 1)
    quarter = DA_HEAD_DIM // 4
    partner = jnp.where((lane % (2 * quarter)) < quarter,
                        pltpu.roll(xn, LANES - quarter, axis=1),
                        pltpu.roll(xn, quarter, axis=1))
    return (xn * cos + partner * sin) * scale


def _proj_qk(h, w, gain, cos, sin, scale, t_len, name):
    n = h.shape[0]
    tm = _tile(n if t_len is None else t_len, 512)
    nb = 1 if t_len is None else t_len // tm
    tab = pl.BlockSpec((tm, LANES), lambda i, j: (i % nb, 0))
    gain2 = jnp.tile(gain.reshape(1, DA_HEAD_DIM), (1, 2))

    def epilogue(acc, g, c, s):
        heads = [_qk_norm_rope(acc[:, k * LANES:(k + 1) * LANES], g, c, s, scale)
                 for k in range(acc.shape[1] // LANES)]
        return jnp.concatenate(heads, axis=1)

    return _matmul(h, w, BF16, tm=tm, epilogue=epilogue,
                   extras=[(gain2, pl.BlockSpec((1, LANES), lambda i, j: (0, 0))), (cos, tab), (sin, tab)],
                   name=name)


def _rope_tables(t_len):
    rows = t_len // GRID_W
    row = jnp.repeat(jnp.arange(rows, dtype=F32), GRID_W)
    col = jnp.tile(jnp.arange(GRID_W, dtype=F32), rows)
    half = DA_HEAD_DIM // 2
    inv_freq = 1.0 / (ROPE_BASE ** (jnp.arange(0, half, 2, dtype=F32) / half))
    ang_r = row[:, None] * inv_freq
    ang_c = col[:, None] * inv_freq
    cos64 = jnp.concatenate([jnp.cos(ang_r), jnp.cos(ang_r), jnp.cos(ang_c), jnp.cos(ang_c)], axis=1)
    sin64 = jnp.concatenate([-jnp.sin(ang_r), jnp.sin(ang_r), -jnp.sin(ang_c), jnp.sin(ang_c)], axis=1)
    return jnp.tile(cos64, (1, 2)), jnp.tile(sin64, (1, 2))


def _attn_body(*refs, has_lat, lam_init):
    if has_lat:
        q_ref, kl_ref, vl_ref, kc_ref, vc_ref, lq_ref, lk_ref, sub_ref, o_ref = refs
    else:
        q_ref, kc_ref, vc_ref, lq_ref, lk_ref, sub_ref, o_ref = refs
    lqk = lq_ref[...] * lk_ref[...]
    lsum = jnp.sum(lqk, axis=1, keepdims=True)
    e = jnp.exp(lsum)
    lam = e[0:1, :] - e[1:2, :] + lam_init
    q = q_ref[...]
    lane = lax.broadcasted_iota(jnp.int32, q.shape, 1)
    zero = jnp.zeros_like(q)
    ext = lambda v: jnp.concatenate([v, jnp.ones_like(v)], axis=1)
    vc_ext = ext(vc_ref[...])
    vl_ext = ext(vl_ref[...]) if has_lat else None
    mixed = []
    for m in range(2):
        in_map = (lane // DA_HEAD_DIM) == m
        qm = jnp.where(in_map, q, zero)
        s_c = _dot_nt(qm, kc_ref[...]).astype(BF16)
        mx = jnp.max(s_c, axis=-1, keepdims=True)
        if has_lat:
            s_l = _dot_nt(qm, kl_ref[...]).astype(BF16)
            mx = jnp.maximum(mx, jnp.max(s_l, axis=-1, keepdims=True))
        acc = jnp.dot(jnp.exp2(s_c - mx), vc_ext, preferred_element_type=F32)
        if has_lat:
            acc = acc + jnp.dot(jnp.exp2(s_l - mx), vl_ext, preferred_element_type=F32)
        mixed.append(acc[:, :DA_V_DIM] / acc[:, DA_V_DIM:])
    o = mixed[0] - lam * mixed[1]
    ms = jnp.mean(o * o, axis=-1, keepdims=True)
    o = o * lax.rsqrt(ms + NORM_EPS) * sub_ref[...] * (1.0 - lam_init)
    o_ref[...] = o.astype(o_ref.dtype)


def _diff_attention(q, k_lat, v_lat, k_ctx, v_ctx, lq, lk, subln, lam_init, batch):
    n, w = q.shape
    tq_len = n // batch
    c_len = k_ctx.shape[0] // batch
    tq = _tile(tq_len, 2048)
    nq = tq_len // tq
    has_lat = k_lat is not None
    blk = lambda rows: pl.BlockSpec((rows, LANES), lambda b, h, i: (b, h))
    in_specs = [pl.BlockSpec((tq, LANES), lambda b, h, i: (b * nq + i, h))]
    args = [q]
    if has_lat:
        t_len = k_lat.shape[0] // batch
        in_specs += [blk(t_len), blk(t_len)]
        args += [k_lat, v_lat]
    in_specs += [blk(c_len), blk(c_len),
                 pl.BlockSpec((2, DA_HEAD_DIM), lambda b, h, i: (0, 0)),
                 pl.BlockSpec((2, DA_HEAD_DIM), lambda b, h, i: (0, 0)),
                 pl.BlockSpec((1, LANES), lambda b, h, i: (0, 0))]
    args += [k_ctx, v_ctx, lq, lk, subln.reshape(1, DA_V_DIM)]
    return pl.pallas_call(
        functools.partial(_attn_body, has_lat=has_lat, lam_init=lam_init),
        grid=(batch, w // LANES, nq),
        in_specs=in_specs,
        out_specs=pl.BlockSpec((tq, LANES), lambda b, h, i: (b * nq + i, h)),
        out_shape=jax.ShapeDtypeStruct((n, w), BF16),
        compiler_params=_cparams(("parallel", "parallel", "arbitrary")),
        name="diff_attention",
    )(*args)


def _rwkv_prep_body(x_ref, xp_ref, xn_ref, conv_ref, lora_ref, w0a0_ref, kk_w_ref, ka_w_ref, ones_ref,
                    lwf_ref, lwb_ref, kk_ref, kkaf_ref, kkab_ref, kdf_ref, kdb_ref, v_ref, r_ref, gd_ref,
                    *, blocks_per_seq):
    i = pl.program_id(0)
    x = x_ref[...]
    tm = x.shape[0]
    first = (i % blocks_per_seq) == 0
    last = (i % blocks_per_seq) == blocks_per_seq - 1
    xp = jnp.where(first, 0.0, xp_ref[SUBLANES - 1:SUBLANES, :])
    xn = jnp.where(last, 0.0, xn_ref[0:1, :])
    row = lax.broadcasted_iota(jnp.int32, (tm, 1), 0)
    up = jnp.where(row == 0, xp, pltpu.roll(x, 1, axis=0))
    dn = jnp.where(row == tm - 1, xn, pltpu.roll(x, tm - 1, axis=0))
    cw = conv_ref[...]
    cv = up * cw[0:1, :] + x * cw[1:2, :] + dn * cw[2:3, :]
    w = RW_WIDTH
    k = cv[:, :w]
    v = cv[:, w:2 * w]
    lora_in = cv[:, 2 * w:RW_STATE_COLS]
    r = cv[:, RW_STATE_COLS:RW_STATE_COLS + w]
    gd = cv[:, RW_STATE_COLS + w:]
    lane = lax.broadcasted_iota(jnp.int32, lora_in.shape, 1)
    li = jnp.where(lane < 2 * RW_DECAY_RANK, jnp.tanh(lora_in), lora_in)
    li_hi, li_lo = _hilo(li)
    pre = (jnp.dot(li_hi, lora_ref[0], preferred_element_type=F32)
           + jnp.dot(li_lo, lora_ref[0], preferred_element_type=F32)
           + jnp.dot(li_hi, lora_ref[1], preferred_element_type=F32)) + w0a0_ref[...]
    kkr = k * kk_w_ref[...]
    ss = _dot_hp_lhs(kkr * kkr, ones_ref[...])
    kk = kkr * lax.rsqrt(ss + 1e-12)
    kk_ref[...] = kk
    v_ref[...] = v
    r_ref[...] = r
    gd_ref[...] = gd
    ka = ka_w_ref[...]
    for d, (lw_ref, kka_ref, kd_ref) in enumerate(((lwf_ref, kkaf_ref, kdf_ref), (lwb_ref, kkab_ref, kdb_ref))):
        lw_ref[...] = -math.exp(-0.5) * _sigmoid(pre[:, d * w:(d + 1) * w])
        a = _sigmoid(pre[:, (2 + d) * w:(3 + d) * w])
        kka_ref[...] = kk * a
        kd_ref[...] = k * (1.0 + (a - 1.0) * ka)


def _rwkv_prep(p_rw, seq_len, conv, lora_w, w0a0, k_k, k_a):
    n, c = p_rw.shape
    tm = _tile(seq_len, 128)
    bps = seq_len // tm
    sub = tm // SUBLANES
    nsub = n // SUBLANES
    w = RW_WIDTH
    wide = lambda: pl.BlockSpec((tm, w), lambda i: (i, 0))
    outs = [jax.ShapeDtypeStruct((n, w), F32)] * 9 + [jax.ShapeDtypeStruct((n, RW_GATE_RANK), F32)]
    return pl.pallas_call(
        functools.partial(_rwkv_prep_body, blocks_per_seq=bps),
        grid=(n // tm,),
        in_specs=[pl.BlockSpec((tm, c), lambda i: (i, 0)),
                  pl.BlockSpec((SUBLANES, c), lambda i: (jnp.maximum(i * sub - 1, 0), 0)),
                  pl.BlockSpec((SUBLANES, c), lambda i: (jnp.minimum((i + 1) * sub, nsub - 1), 0)),
                  pl.BlockSpec((3, c), lambda i: (0, 0)),
                  pl.BlockSpec(lora_w.shape, lambda i: (0, 0, 0)),
                  pl.BlockSpec((1, 4 * w), lambda i: (0, 0)),
                  pl.BlockSpec((1, w), lambda i: (0, 0)),
                  pl.BlockSpec((1, w), lambda i: (0, 0)),
                  pl.BlockSpec((w, w), lambda i: (0, 0))],
        out_specs=[wide() for _ in range(9)] + [pl.BlockSpec((tm, RW_GATE_RANK), lambda i: (i, 0))],
        out_shape=outs,
        compiler_params=_cparams(("parallel",)),
        name="rwkv_prep",
    )(p_rw, p_rw, p_rw, conv, lora_w, w0a0, k_k.reshape(1, w), k_a.reshape(1, w), _group_ones(w, RW_HEAD_DIM))


def _rwkv_scan_body(lwf_ref, kkaf_ref, kdf_ref, kkf_ref, vf_ref, rf_ref,
                    lwb_ref, kkab_ref, kdb_ref, kkb_ref, vb_ref, rb_ref, s0f_ref, s0b_ref,
                    of_ref, ob_ref, stf_ref, stb_ref, *, want_out):
    ci = pl.program_id(1)

    @pl.when(ci == 0)
    def _():
        stf_ref[...] = s0f_ref[...]
        stb_ref[...] = s0b_ref[...]

    n_l, width = lwf_ref.shape
    hd = RW_HEAD_DIM
    gw = RW_GROUP_HEADS * hd
    n_groups = width // gw
    row = lax.broadcasted_iota(jnp.int32, (n_l, n_l), 0)
    col = lax.broadcasted_iota(jnp.int32, (n_l, n_l), 1)
    trow = lax.broadcasted_iota(jnp.int32, (n_l, gw), 0)
    tcol = lax.broadcasted_iota(jnp.int32, (n_l, gw), 1) % hd
    eye = (tcol == trow).astype(F32)
    same_head = ((lax.broadcasted_iota(jnp.int32, (gw, gw), 0) // hd)
                 == (lax.broadcasted_iota(jnp.int32, (gw, gw), 1) // hd))
    reps = gw // n_l

    def bdiag(x):
        xb = x.astype(BF16)
        return jnp.where(same_head, jnp.concatenate([xb] * reps, axis=0), jnp.zeros((), BF16))

    units = []
    for reverse, refs in ((False, (lwf_ref, kkaf_ref, kdf_ref, kkf_ref, vf_ref, rf_ref, stf_ref, of_ref)),
                          (True, (lwb_ref, kkab_ref, kdb_ref, kkb_ref, vb_ref, rb_ref, stb_ref, ob_ref))):
        lw_ref, kka_ref, k_ref, kk_ref, v_ref, r_ref, st_ref, o_ref = refs
        lw = lw_ref[...]
        tri = ((col >= row) if reverse else (col <= row)).astype(BF16)
        lh, ll = _hilo(lw)
        c = jnp.dot(tri, lh, preferred_element_type=F32) + jnp.dot(tri, ll, preferred_element_type=F32)
        g_end = jnp.exp(c[0:1, :] if reverse else c[n_l - 1:n_l, :])
        e_inv = jnp.exp(-c)
        a_t = kk_ref[...] * jnp.exp(c - lw)
        r_t = r_ref[...] * jnp.exp(c)
        k_h = k_ref[...] * e_inv
        b_h = kka_ref[...] * e_inv
        v_all = v_ref[...]
        incl, strict = (tcol >= trow, tcol > trow) if reverse else (tcol <= trow, tcol < trow)
        for g in range(n_groups):
            s = slice(g * gw, (g + 1) * gw)
            units.append(dict(g=g, sl=s, incl=incl, strict=strict, st_ref=st_ref, o_ref=o_ref, state=st_ref[g],
                              ar=jnp.concatenate([a_t[:, s], r_t[:, s]], axis=0).astype(BF16),
                              k_h=k_h[:, s], b_h=b_h[:, s], v=v_all[:, s], g_end=g_end[:, s]))

    n_iter = max(1, (n_l - 1).bit_length()) - 1
    sc_k = [_dot_nt(x["ar"], bdiag(x["k_h"])) for x in units]
    sc_b = [_dot_nt(x["ar"], bdiag(x["b_h"])) for x in units]
    from_s = [_dot_nt(x["ar"], x["state"]) for x in units]
    p = [-jnp.where(x["strict"], sb[:n_l], 0.0) for x, sb in zip(units, sc_b)]
    t_inv = [eye + pi for pi in p]
    if n_iter:
        p = [_dot(pi, bdiag(pi)) for pi in p]
    for it in range(n_iter):
        if it + 1 < n_iter:
            y = [_dot(jnp.concatenate([ti, pi], axis=0), bdiag(pi)) for ti, pi in zip(t_inv, p)]
            t_inv = [ti + yi[:n_l] for ti, yi in zip(t_inv, y)]
            p = [yi[n_l:] for yi in y]
        else:
            t_inv = [ti + _dot(ti, bdiag(pi)) for ti, pi in zip(t_inv, p)]
    v_bd = [bdiag(x["v"]) for x in units]
    w = [fs[:n_l] + _dot(jnp.where(x["strict"], sk[:n_l], 0.0), vb)
         for x, fs, sk, vb in zip(units, from_s, sc_k, v_bd)]
    u = [_dot(ti, bdiag(wi)) for ti, wi in zip(t_inv, w)]
    if want_out:
        o_v = [_dot(jnp.where(x["incl"], sk[n_l:], 0.0), vb) for x, sk, vb in zip(units, sc_k, v_bd)]
        o_u = [_dot(jnp.where(x["incl"], sb[n_l:], 0.0), bdiag(ui)) for x, sb, ui in zip(units, sc_b, u)]
        for x, fs, ov, ou in zip(units, from_s, o_v, o_u):
            x["o_ref"][:, x["sl"]] = fs[n_l:] + ov - ou
    else:
        of_ref[...] = jnp.zeros_like(of_ref)
        ob_ref[...] = jnp.zeros_like(ob_ref)
    upd_k = [_dot_tn(x["v"], x["k_h"] * x["g_end"]) for x in units]
    upd_b = [_dot_tn(ui, x["b_h"] * x["g_end"]) for x, ui in zip(units, u)]
    for x, uk, ub in zip(units, upd_k, upd_b):
        x["st_ref"][x["g"]] = jnp.where(same_head, x["state"] * x["g_end"] + uk - ub, 0.0)


def _rwkv_scan(t, s0_f, s0_b, batch, *, want_out=True):
    n, w = t["kk"].shape
    seq = n // batch
    n_l = _tile(seq, RW_CHUNK)
    nch = seq // n_l
    gw = RW_GROUP_HEADS * RW_HEAD_DIM
    assert gw % n_l == 0 and w % gw == 0
    n_groups = w // gw
    fwd = pl.BlockSpec((n_l, w), lambda b, c: (b * nch + c, 0))
    rev = pl.BlockSpec((n_l, w), lambda b, c: (b * nch + nch - 1 - c, 0))
    st_spec = pl.BlockSpec((n_groups, gw, gw), lambda b, c: (b, 0, 0))
    seq_shape = jax.ShapeDtypeStruct((n, w), F32)
    st_shape = jax.ShapeDtypeStruct(s0_f.shape, F32)
    return pl.pallas_call(
        functools.partial(_rwkv_scan_body, want_out=want_out),
        grid=(batch, nch),
        in_specs=[fwd] * 6 + [rev] * 6 + [st_spec, st_spec],
        out_specs=[fwd, rev, st_spec, st_spec],
        out_shape=[seq_shape, seq_shape, st_shape, st_shape],
        compiler_params=_cparams(("parallel", "arbitrary")),
        name="rwkv_scan",
    )(t["lw_f"], t["kka_f"], t["kd_f"], t["kk"], t["v"], t["r"],
      t["lw_b"], t["kka_b"], t["kd_b"], t["kk"], t["v"], t["r"], s0_f, s0_b)


def _rwkv_readout_body(of_ref, ob_ref, r_ref, kdf_ref, kdb_ref, v_ref, gd_ref, gnw_ref, gnb_ref, rk_ref, g2_ref,
                       ones_ref, y_ref):
    ones = ones_ref[...]
    inv = 1.0 / RW_HEAD_DIM
    o = of_ref[...] + ob_ref[...]
    mu = _dot_hp_lhs(o, ones) * inv
    d = o - mu
    var = _dot_hp_lhs(d * d, ones) * inv
    on = d * lax.rsqrt(var + RW_GN_EPS) * gnw_ref[...] + gnb_ref[...]
    r = r_ref[...]
    rk = rk_ref[...]
    bonus = _dot_hp_lhs(r * kdf_ref[...] * rk, ones) + _dot_hp_lhs(r * kdb_ref[...] * rk, ones)
    y = on + bonus * v_ref[...]
    g = _dot(_sigmoid(gd_ref[...]), g2_ref[...])
    y_ref[...] = (y * g).astype(y_ref.dtype)


def _rwkv_readout(o_f, o_b, r, kd_f, kd_b, v, gd, gn_w, gn_b, r_k, g2):
    n, w = o_f.shape
    tm = _tile(n, 256)
    wide = pl.BlockSpec((tm, w), lambda i: (i, 0))
    vec = pl.BlockSpec((1, w), lambda i: (0, 0))
    return pl.pallas_call(
        _rwkv_readout_body,
        grid=(n // tm,),
        in_specs=[wide] * 6 + [pl.BlockSpec((tm, RW_GATE_RANK), lambda i: (i, 0)), vec, vec, vec,
                               pl.BlockSpec((RW_GATE_RANK, w), lambda i: (0, 0)),
                               pl.BlockSpec((w, w), lambda i: (0, 0))],
        out_specs=wide,
        out_shape=jax.ShapeDtypeStruct((n, w), BF16),
        compiler_params=_cparams(("parallel",)),
        name="rwkv_readout",
    )(o_f, o_b, r, kd_f, kd_b, v, gd, gn_w.reshape(1, w), gn_b.reshape(1, w), r_k.reshape(1, w), g2,
      _group_ones(w, RW_HEAD_DIM))


def _s5_scan_body(u_ref, bre_ref, bim_ref, are_ref, aim_ref, cre_ref, cim_ref, h0_ref, y_ref, ht_ref,
                  dre, dim, *, reverse, want_out):
    ci = pl.program_id(0)

    @pl.when(ci == 0)
    def _():
        ht_ref[...] = h0_ref[...]

    tt, nb, wu = u_ref.shape
    nblk = wu // LANES
    sw = S5_LANES // nblk
    u2 = u_ref[...].reshape(tt * nb, wu).astype(BF16)
    for c in range(nblk):
        uc = u2[:, c * LANES:(c + 1) * LANES]
        dre[:, :, c * sw:(c + 1) * sw] = jnp.dot(uc, bre_ref[c], preferred_element_type=F32).reshape(tt, nb, sw)
        dim[:, :, c * sw:(c + 1) * sw] = jnp.dot(uc, bim_ref[c], preferred_element_type=F32).reshape(tt, nb, sw)
    lw = 1024
    for c in range(S5_LANES // lw):
        ls = slice(c * lw, (c + 1) * lw)
        ar = jnp.broadcast_to(are_ref[:, ls], (nb, lw))
        ai = jnp.broadcast_to(aim_ref[:, ls], (nb, lw))

        def step(s, carry, ls=ls, ar=ar, ai=ai):
            t = (tt - 1 - s) if reverse else s
            hr, hi = carry
            nr = ar * hr - ai * hi + dre[t, :, ls]
            ni = ar * hi + ai * hr + dim[t, :, ls]
            dre[t, :, ls] = nr
            dim[t, :, ls] = ni
            return nr, ni

        hr, hi = lax.fori_loop(0, tt, step, (ht_ref[0, :, ls], ht_ref[1, :, ls]), unroll=2)
        ht_ref[0, :, ls] = hr
        ht_ref[1, :, ls] = hi
    if want_out:
        xr = dre[...].reshape(tt * nb, S5_LANES).astype(BF16)
        xi = dim[...].reshape(tt * nb, S5_LANES).astype(BF16)
        for c in range(nblk):
            yc = (jnp.dot(xr[:, c * sw:(c + 1) * sw], cre_ref[c], preferred_element_type=F32)
                  - jnp.dot(xi[:, c * sw:(c + 1) * sw], cim_ref[c], preferred_element_type=F32))
            y_ref[:, :, c * LANES:(c + 1) * LANES] = yc.reshape(tt, nb, LANES)
    else:
        y_ref[...] = jnp.zeros_like(y_ref)


def _s5_scan(u_tm, h0, p, *, reverse, want_out=True):
    t_len, nb, wu = u_tm.shape
    tt = _tile(t_len, S5_CHUNK)
    nch = t_len // tt
    chunk = (lambda c: (nch - 1 - c, 0, 0)) if reverse else (lambda c: (c, 0, 0))
    const3 = lambda a: pl.BlockSpec(a.shape, lambda c: (0, 0, 0))
    const2 = lambda a: pl.BlockSpec(a.shape, lambda c: (0, 0))
    return pl.pallas_call(
        functools.partial(_s5_scan_body, reverse=reverse, want_out=want_out),
        grid=(nch,),
        in_specs=[pl.BlockSpec((tt, nb, wu), chunk), const3(p["b_re"]), const3(p["b_im"]),
                  const2(p["a_re"]), const2(p["a_im"]), const3(p["c_re"]), const3(p["c_im"]), const3(h0)],
        out_specs=[pl.BlockSpec((tt, nb, wu), chunk), const3(h0)],
        out_shape=[jax.ShapeDtypeStruct(u_tm.shape, F32), jax.ShapeDtypeStruct(h0.shape, F32)],
        scratch_shapes=[pltpu.VMEM((tt, nb, S5_LANES), F32), pltpu.VMEM((tt, nb, S5_LANES), F32)],
        compiler_params=_cparams(("arbitrary",)),
        name="s5_scan_rev" if reverse else "s5_scan_fwd",
    )(u_tm, p["b_re"], p["b_im"], p["a_re"], p["a_im"], p["c_re"], p["c_im"], h0)


def _s5_dir_params(lam_re, lam_im, log_dt, b_re, b_im, c_re, c_im):
    g, pdim = lam_re.shape
    dt = jnp.exp(log_dt.astype(F32))[:, None]
    mag = jnp.exp(lam_re * dt)
    abar_re = mag * jnp.cos(lam_im * dt)
    abar_im = mag * jnp.sin(lam_im * dt)
    den = lam_re * lam_re + lam_im * lam_im
    nr = abar_re - 1.0
    g_re = (nr * lam_re + abar_im * lam_im) / den
    g_im = (abar_im * lam_re - nr * lam_im) / den
    bb_re = g_re[:, :, None] * b_re - g_im[:, :, None] * b_im
    bb_im = g_re[:, :, None] * b_im + g_im[:, :, None] * b_re
    gpb = LANES // S5_GROUP
    nblk = g // gpb
    eye = jnp.eye(gpb, dtype=F32)

    def drive_mat(bb):
        x = bb.reshape(nblk, gpb, pdim, S5_GROUP)
        x = jnp.einsum("cgph,gk->cghkp", x, eye)
        return x.reshape(nblk, gpb * S5_GROUP, gpb * pdim).astype(BF16)

    def read_mat(cc):
        x = cc.reshape(nblk, gpb, S5_GROUP, pdim)
        x = jnp.einsum("cghp,gk->cgpkh", x, eye)
        return x.reshape(nblk, gpb * pdim, gpb * S5_GROUP).astype(BF16)

    return {"a_re": abar_re.reshape(1, g * pdim), "a_im": abar_im.reshape(1, g * pdim),
            "b_re": drive_mat(bb_re), "b_im": drive_mat(bb_im),
            "c_re": read_mat(c_re.astype(F32)), "c_im": read_mat(c_im.astype(F32))}


def _s5_glu_body(u_ref, yf_ref, yb_ref, d_ref, w_ref, b_ref, o_ref):
    y = u_ref[...] * d_ref[...] + yf_ref[...] + yb_ref[...]
    y = _gelu_tanh(y)
    z = _dot(y, w_ref[...]) + b_ref[...]
    o_ref[...] = (y * _sigmoid(z)).astype(o_ref.dtype)


def _s5_glu(u, y_f, y_b, d_skip, glu_w, glu_b):
    n, w = u.shape
    tm = _tile(n, 512)
    wide = pl.BlockSpec((tm, w), lambda i: (i, 0))
    vec = pl.BlockSpec((1, w), lambda i: (0, 0))
    return pl.pallas_call(
        _s5_glu_body,
        grid=(n // tm,),
        in_specs=[wide, wide, wide, vec, pl.BlockSpec((w, w), lambda i: (0, 0)), vec],
        out_specs=wide,
        out_shape=jax.ShapeDtypeStruct((n, w), BF16),
        compiler_params=_cparams(("parallel",)),
        name="s5_glu",
    )(u, y_f, y_b, d_skip.reshape(1, w), glu_w, glu_b.reshape(1, w))


def _merge_body(ya_ref, yr_ref, ys_ref, wa_ref, wr_ref, ws_ref, ga_ref, gr_ref, gs_ref, o_ref):
    m = (ga_ref[...].astype(F32) * jnp.dot(ya_ref[...], wa_ref[...], preferred_element_type=F32)
         + gr_ref[...].astype(F32) * jnp.dot(yr_ref[...], wr_ref[...], preferred_element_type=F32)
         + gs_ref[...].astype(F32) * jnp.dot(ys_ref[...], ws_ref[...], preferred_element_type=F32))
    o_ref[...] = m.astype(o_ref.dtype)


def _merge(y_a, y_r, y_s_tm, gates, w_a, w_r, w_s, seq_len, batch):
    n = y_a.shape[0]
    d = w_a.shape[1]
    tm = _tile(seq_len, 512)
    tn = _tile(d, 1024, LANES)
    nt = seq_len // tm
    nd = d // tn
    row = lambda i, j: (i, 0)
    return pl.pallas_call(
        _merge_body,
        grid=(n // tm, nd),
        in_specs=[pl.BlockSpec((tm, y_a.shape[1]), row),
                  pl.BlockSpec((tm, y_r.shape[1]), row),
                  pl.BlockSpec((tm, S5_WIDTH), lambda i, j: (i % nt, i // nt)),
                  pl.BlockSpec((w_a.shape[0], tn), lambda i, j: (0, j)),
                  pl.BlockSpec((w_r.shape[0], tn), lambda i, j: (0, j)),
                  pl.BlockSpec((w_s.shape[0], tn), lambda i, j: (0, j)),
                  pl.BlockSpec((tm, tn), lambda i, j: (i, j)),
                  pl.BlockSpec((tm, tn), lambda i, j: (i, nd + j)),
                  pl.BlockSpec((tm, tn), lambda i, j: (i, 2 * nd + j))],
        out_specs=pl.BlockSpec((tm, tn), lambda i, j: (i, j)),
        out_shape=jax.ShapeDtypeStruct((n, d), BF16),
        compiler_params=_cparams(("parallel", "arbitrary")),
        name="merge",
    )(y_a, y_r, y_s_tm, w_a, w_r, w_s, gates, gates, gates)


def _moe_plan_body(comb_ref, cnt_ref, pos_ref, w_ref, te_ref, carry_ref, *, tile_rows, dummy_row):
    i = pl.program_id(0)
    ne, tm = comb_ref.shape
    tiles = jnp.floor((cnt_ref[...] + (tile_rows - 1)) * (1.0 / tile_rows))
    er = lax.broadcasted_iota(jnp.int32, (ne, ne), 0)
    ec = lax.broadcasted_iota(jnp.int32, (ne, ne), 1)
    t_hi, t_lo = _hilo(tiles)
    lower = (ec < er).astype(BF16)
    off_tiles = (jnp.dot(lower, t_hi, preferred_element_type=F32)
                 + jnp.dot(lower, t_lo, preferred_element_type=F32))

    @pl.when(i == 0)
    def _():
        carry_ref[...] = jnp.zeros_like(carry_ref)
        ntp = te_ref.shape[1]
        end_tiles = (off_tiles + tiles)[:, 0:1]
        tile_idx = lax.broadcasted_iota(jnp.int32, (1, ntp), 1).astype(F32)
        expert = jnp.sum((end_tiles <= tile_idx).astype(F32), axis=0, keepdims=True)
        expert = jnp.minimum(expert, ne - 1.0)
        valid = (tile_idx < jnp.max(end_tiles, axis=0, keepdims=True)).astype(F32)
        cnt = cnt_ref[:, 0:1]
        on_lane = lax.broadcasted_iota(jnp.int32, (ne, ntp), 0) == lax.broadcasted_iota(jnp.int32, (ne, ntp), 1)
        to_lanes = lambda col: jnp.sum(jnp.where(on_lane, col, 0.0), axis=0, keepdims=True)
        pad_start = to_lanes(off_tiles[:, 0:1] * tile_rows + cnt)
        pad_len = to_lanes(tiles[:, 0:1] * tile_rows - cnt)
        r8 = lax.broadcasted_iota(jnp.int32, te_ref.shape, 0)
        rows = jnp.where(r8 == 0, expert, jnp.where(r8 == 1, valid, jnp.where(r8 == 2, pad_start,
                                                                               jnp.where(r8 == 3, pad_len, 0.0))))
        te_ref[...] = rows.astype(jnp.int32)

    comb = comb_ref[...]
    chosen = comb > 0.0
    chf = chosen.astype(BF16)
    tr = lax.broadcasted_iota(jnp.int32, (tm, tm), 0)
    tc = lax.broadcasted_iota(jnp.int32, (tm, tm), 1)
    rank = jnp.dot(chf, (tr < tc).astype(BF16), preferred_element_type=F32)
    pos = off_tiles[:, 0:1] * tile_rows + carry_ref[:, 0:1] + rank
    carry_ref[...] += jnp.sum(chosen.astype(F32), axis=1, keepdims=True)
    eidx = lax.broadcasted_iota(jnp.int32, (ne, tm), 0)
    tok = lax.broadcasted_iota(jnp.int32, (1, tm), 1).astype(F32)
    remaining = chosen
    pos_rows, w_rows = [], []
    for k in range(TOP_K):
        first = jnp.min(jnp.where(remaining, eidx, ne), axis=0, keepdims=True)
        pick = eidx == first
        pos_k = jnp.sum(jnp.where(pick, pos, 0.0), axis=0, keepdims=True)
        pos_rows.append(jnp.where(first < ne, pos_k, float(dummy_row + k * tm) + tok))
        w_rows.append(jnp.sum(jnp.where(pick, comb, 0.0), axis=0, keepdims=True))
        remaining = jnp.logical_and(remaining, jnp.logical_not(pick))
    pos_ref[...] = jnp.concatenate(pos_rows, axis=0).astype(jnp.int32)
    wmat = jnp.concatenate(w_rows + [jnp.zeros((LANES - TOP_K, tm), F32)], axis=0)
    w_ref[...] = wmat.T


def _moe_token_tile(n):
    return _tile(n, 256, LANES)


def _moe_plan(comb_t, counts, n_tiles, dummy_row):
    ne, n = comb_t.shape
    tm = _moe_token_tile(n)
    ntp = -(-n_tiles // LANES) * LANES
    return pl.pallas_call(
        functools.partial(_moe_plan_body, tile_rows=MOE_TILE, dummy_row=dummy_row),
        grid=(n // tm,),
        in_specs=[pl.BlockSpec((ne, tm), lambda i: (0, i)),
                  pl.BlockSpec((ne, LANES), lambda i: (0, 0))],
        out_specs=[pl.BlockSpec((TOP_K, tm), lambda i: (0, i)),
                   pl.BlockSpec((tm, LANES), lambda i: (i, 0)),
                   pl.BlockSpec((SUBLANES, ntp), lambda i: (0, 0))],
        out_shape=[jax.ShapeDtypeStruct((TOP_K, n), jnp.int32), jax.ShapeDtypeStruct((n, LANES), F32),
                   jax.ShapeDtypeStruct((SUBLANES, ntp), jnp.int32)],
        scratch_shapes=[pltpu.VMEM((ne, LANES), F32)],
        compiler_params=_cparams(("arbitrary",)),
        name="moe_plan",
    )(comb_t, counts)


def _start_row_copies(pos_ref, tm, make_copy):
    def start(t, c):
        for k in range(TOP_K):
            make_copy(t, k, pos_ref[k, t]).start(priority=k % 2)
        return c

    lax.fori_loop(0, tm, start, 0)


def _moe_dispatch_body(pos_ref, plan_ref, hp_ref, xg_hbm, zrow, sem, zsem, *, tm, experts_per_step, n_experts):
    _start_row_copies(pos_ref, tm, lambda t, k, p: pltpu.make_async_copy(
        hp_ref.at[pl.ds(t, 1)], xg_hbm.at[pl.ds(p, 1)], sem))
    zrow[...] = jnp.zeros_like(zrow)
    zero_copy = lambda row: pltpu.make_async_copy(zrow.at[pl.ds(0, 1)], xg_hbm.at[pl.ds(row, 1)], zsem)
    for j in range(experts_per_step):
        e = pl.program_id(0) * experts_per_step + j

        @pl.when(e < n_experts)
        def _():
            start = plan_ref[2, e]
            count = plan_ref[3, e]
            lax.fori_loop(0, count, lambda r, c: (zero_copy(start + r).start(), c)[1], 0)
            lax.fori_loop(0, count, lambda r, c: (zero_copy(start + r).wait(), c)[1], 0)

    for _ in range(TOP_K):
        pltpu.make_async_copy(hp_ref, hp_ref, sem).wait()


def _moe_dispatch(pos, plan, hp, total_rows, n_experts):
    n, half = hp.shape
    tm = _moe_token_tile(n)
    steps = n // tm
    return pl.pallas_call(
        functools.partial(_moe_dispatch_body, tm=tm, experts_per_step=-(-n_experts // steps), n_experts=n_experts),
        grid=(steps,),
        in_specs=[pl.BlockSpec((TOP_K, tm), lambda i: (0, i), memory_space=pltpu.SMEM),
                  pl.BlockSpec(memory_space=pltpu.SMEM),
                  pl.BlockSpec((tm, half), lambda i: (i, 0))],
        out_specs=pl.BlockSpec(memory_space=pl.ANY),
        out_shape=jax.ShapeDtypeStruct((total_rows, half), jnp.uint32),
        scratch_shapes=[pltpu.VMEM((SUBLANES, half), jnp.uint32), pltpu.SemaphoreType.DMA, pltpu.SemaphoreType.DMA],
        compiler_params=_cparams(("arbitrary",)),
        name="moe_dispatch",
    )(pos, plan, hp)


def _moe_ffn_body(te_ref, tv_ref, xg_ref, wg_ref, wu_ref, wd_ref, ys_ref, wg_bf, wu_bf, wd_bf):
    j = pl.program_id(0)
    new_expert = jnp.logical_or(j == 0, te_ref[j] != te_ref[jnp.maximum(j - 1, 0)])

    @pl.when(new_expert)
    def _():
        wg_bf[...] = wg_ref[0, 0].astype(BF16)
        wu_bf[...] = wu_ref[0, 0].astype(BF16)
        wd_bf[...] = wd_ref[0, 0].astype(BF16)

    @pl.when(tv_ref[j] != 0)
    def _():
        lo, hi = _unpack_halves(xg_ref[...])
        half = lo.shape[1]
        hg = _dot(lo, wg_bf[:half, :]) + _dot(hi, wg_bf[half:, :])
        hu = _dot(lo, wu_bf[:half, :]) + _dot(hi, wu_bf[half:, :])
        act = hg * _sigmoid(hg) * hu
        ys_ref[...] = _pack_halves(_dot(act, wd_bf[...]))

    @pl.when(tv_ref[j] == 0)
    def _():
        ys_ref[...] = jnp.zeros_like(ys_ref)


def _moe_ffn(tile_expert, tile_valid, xg, w_gate, w_up, w_down, layer):
    rows, half = xg.shape
    _, ne, d, ff = w_gate.shape
    blk = pl.BlockSpec((MOE_TILE, half), lambda j, te, tv: (j, 0))
    return pl.pallas_call(
        _moe_ffn_body,
        grid_spec=pltpu.PrefetchScalarGridSpec(
            num_scalar_prefetch=2,
            grid=(rows // MOE_TILE,),
            in_specs=[blk,
                      pl.BlockSpec((1, 1, d, ff), lambda j, te, tv: (layer, te[j], 0, 0)),
                      pl.BlockSpec((1, 1, d, ff), lambda j, te, tv: (layer, te[j], 0, 0)),
                      pl.BlockSpec((1, 1, ff, d), lambda j, te, tv: (layer, te[j], 0, 0))],
            out_specs=blk,
            scratch_shapes=[pltpu.VMEM((d, ff), BF16), pltpu.VMEM((d, ff), BF16), pltpu.VMEM((ff, d), BF16)]),
        out_shape=jax.ShapeDtypeStruct((rows, half), jnp.uint32),
        compiler_params=_cparams(("arbitrary",)),
        name="moe_ffn",
    )(tile_expert, tile_valid, xg, w_gate, w_up, w_down)


def _moe_combine_body(pos_ref, w_ref, ys_hbm, o_ref, buf, sem, *, tm):
    _start_row_copies(pos_ref, tm, lambda t, k, p: pltpu.make_async_copy(
        ys_hbm.at[pl.ds(p, 1)], buf.at[k, pl.ds(t, 1)], sem))
    pltpu.make_async_copy(buf, buf, sem).wait()
    w = w_ref[...]
    half = buf.shape[2]
    acc_lo = jnp.zeros((tm, half), F32)
    acc_hi = jnp.zeros((tm, half), F32)
    for k in range(TOP_K):
        lo, hi = _unpack_halves(buf[k])
        wk = w[:, k:k + 1]
        acc_lo = acc_lo + wk * lo
        acc_hi = acc_hi + wk * hi
    o_ref[:, :half] = acc_lo
    o_ref[:, half:] = acc_hi


def _moe_combine(pos, w_tok, ys):
    n = w_tok.shape[0]
    half = ys.shape[1]
    tm = _tile(n, 128, LANES)
    return pl.pallas_call(
        functools.partial(_moe_combine_body, tm=tm),
        grid=(n // tm,),
        in_specs=[pl.BlockSpec((TOP_K, tm), lambda i: (0, i), memory_space=pltpu.SMEM),
                  pl.BlockSpec((tm, LANES), lambda i: (i, 0)),
                  pl.BlockSpec(memory_space=pl.ANY)],
        out_specs=pl.BlockSpec((tm, 2 * half), lambda i: (i, 0)),
        out_shape=jax.ShapeDtypeStruct((n, 2 * half), F32),
        scratch_shapes=[pltpu.VMEM((TOP_K, tm, half), jnp.uint32), pltpu.SemaphoreType.DMA],
        compiler_params=_cparams(("arbitrary",)),
        name="moe_combine",
    )(pos, w_tok, ys)


def _moe_routed(hp, comb_t, counts, w_gate, w_up, w_down, layer):
    n = hp.shape[0]
    ne = w_gate.shape[1]
    run_tiles = (n * TOP_K) // MOE_TILE + ne
    dummy_row = run_tiles * MOE_TILE
    spare_tiles = -(-(TOP_K * _moe_token_tile(n)) // MOE_TILE)
    n_tiles = run_tiles + spare_tiles
    pos, w_tok, te = _moe_plan(comb_t, counts, n_tiles, dummy_row)
    xg = _moe_dispatch(pos, te, hp, n_tiles * MOE_TILE, ne)
    ys = _moe_ffn(te[0, :n_tiles], te[1, :n_tiles], xg, w_gate, w_up, w_down, layer)
    return _moe_combine(pos, w_tok, ys)


def _swiglu_up_body(h_ref, wg_ref, wu_ref, o_ref):
    h = h_ref[...]
    g = jnp.dot(h, wg_ref[...], preferred_element_type=F32)
    u = jnp.dot(h, wu_ref[...], preferred_element_type=F32)
    o_ref[...] = (g * _sigmoid(g) * u).astype(o_ref.dtype)


def _moe_shared_residual(h, w_gate, w_up, w_down, f_routed, f_row0, x2, mod, gate_k, rows_per_batch):
    n, d = x2.shape
    ff = w_up.shape[1]
    tm = _tile(n, 512)
    tn = _tile(ff, 512, LANES)
    act = pl.pallas_call(
        _swiglu_up_body,
        grid=(n // tm, ff // tn),
        in_specs=[pl.BlockSpec((tm, d), lambda i, j: (i, 0)),
                  pl.BlockSpec((d, tn), lambda i, j: (0, j)),
                  pl.BlockSpec((d, tn), lambda i, j: (0, j))],
        out_specs=pl.BlockSpec((tm, tn), lambda i, j: (i, j)),
        out_shape=jax.ShapeDtypeStruct((n, ff), BF16),
        compiler_params=_cparams(("parallel", "arbitrary")),
        name="shared_up",
    )(h, w_gate, w_up)
    tm = _tile(n if rows_per_batch is None else rows_per_batch, 1024)
    tn = _tile(d, 1024, LANES)
    nd = d // tn
    mrow = _mod_row_fn(rows_per_batch, tm)
    blk = pl.BlockSpec((tm, tn), lambda i, j: (i, j))
    assert f_row0 % tm == 0
    f_blk = pl.BlockSpec((tm, tn), lambda i, j: (f_row0 // tm + i, j))
    return _matmul(act, w_down, F32, tm=tm, tn=tn,
                   epilogue=lambda acc, fr, xb, g: xb + g[0] * (acc + fr),
                   extras=[(f_routed, f_blk), (x2, blk),
                           (mod, pl.BlockSpec((1, 1, tn), lambda i, j: (mrow(i), 0, gate_k * nd + j)))],
                   name="shared_down")


def _token_mixer(h, hc, lw, lam_init, tabs, need_ctx, batch, t_len, c_len):
    cos, sin = tabs
    flat_rows = _tile(hc.shape[0], 512)
    flat_tabs = (jnp.ones((flat_rows, LANES), F32), jnp.zeros((flat_rows, LANES), F32))

    def project(hh, seq_len, full, positional):
        n = hh.shape[0]
        rope = (cos, sin, seq_len) if positional else flat_tabs + (None,)
        out = {}
        out["k"] = _proj_qk(hh, lw["w_k"], lw["da_k_norm"], rope[0], rope[1], 1.0, rope[2], "proj_k")
        out["v"] = _matmul(hh, lw["w_v"], BF16, name="proj_v")
        out["rw"] = _matmul(hh, lw["w_rw"], F32, name="proj_rw")
        tm = _tile(seq_len, 512)
        nt = seq_len // tm
        tn = S5_WIDTH
        out["s5"] = _matmul(
            hh, lw["w_s5"], F32, tn=tn, grid_m=n // tm,
            a_spec=pl.BlockSpec((tm, hh.shape[1]), lambda i, j: (i, 0)),
            out_spec=pl.BlockSpec((tm, tn), lambda i, j: (i % nt, i // nt)),
            out_shape=(seq_len, batch * S5_WIDTH), name="proj_s5").reshape(seq_len, batch, S5_WIDTH)
        if full:
            out["q"] = _proj_qk(hh, lw["w_q"], lw["da_q_norm"], rope[0], rope[1], DA_Q_SCALE, rope[2], "proj_q")
            out["gates"] = _matmul(hh, lw["w_gates"], BF16, epilogue=_sigmoid, name="proj_gates")
        return out

    pl_ = project(h, t_len, True, True)
    pc_ = project(hc, c_len, need_ctx, False)

    y_a = _diff_attention(pl_["q"], pl_["k"], pl_["v"], pc_["k"], pc_["v"], lw["da_lambda_q"], lw["da_lambda_k"],
                          lw["da_subln"], lam_init, batch)
    y_ac = None
    if need_ctx:
        y_ac = _diff_attention(pc_["q"], None, None, pc_["k"], pc_["v"], lw["da_lambda_q"], lw["da_lambda_k"],
                               lw["da_subln"], lam_init, batch)

    def prep(p_rw, seq_len):
        names = ("lw_f", "lw_b", "kk", "kka_f", "kka_b", "kd_f", "kd_b", "v", "r", "gd")
        vals = _rwkv_prep(p_rw, seq_len, lw["rw_conv"], lw["rw_lora"], lw["rw_w0a0"], lw["rw_k_k"], lw["rw_k_a"])
        return dict(zip(names, vals))

    tl, tc = prep(pl_["rw"], t_len), prep(pc_["rw"], c_len)
    gw = RW_GROUP_HEADS * RW_HEAD_DIM
    s_zero = jnp.zeros((batch * (RW_WIDTH // gw), gw, gw), F32)
    ocf, ocb, s_ctx_f, s_ctx_b = _rwkv_scan(tc, s_zero, s_zero, batch, want_out=need_ctx)
    olf, olb, _, _ = _rwkv_scan(tl, s_ctx_f, s_ctx_b, batch)
    o_lat, o_ctx = {"f": olf, "b": olb}, {"f": ocf, "b": ocb}
    ro = lambda o, t: _rwkv_readout(o["f"], o["b"], t["r"], t["kd_f"], t["kd_b"], t["v"], t["gd"],
                                    lw["rw_gn_w"], lw["rw_gn_b"], lw["rw_r_k"], lw["rw_g2"])
    y_r = ro(o_lat, tl)
    y_rc = ro(o_ctx, tc) if need_ctx else None

    h_zero = jnp.zeros((2, batch, S5_LANES), F32)
    ys_lat, ys_ctx = {}, {}
    for d, rev in (("f", False), ("b", True)):
        yc, h_ctx = _s5_scan(pc_["s5"], h_zero, lw["s5_" + d], reverse=rev, want_out=need_ctx)
        yl, _ = _s5_scan(pl_["s5"], h_ctx, lw["s5_" + d], reverse=rev)
        ys_lat[d], ys_ctx[d] = yl, yc
    flat = lambda a: a.reshape(a.shape[0] * batch, S5_WIDTH)
    glu = lambda p, ys, seq: _s5_glu(flat(p["s5"]), flat(ys["f"]), flat(ys["b"]), lw["s5_d"], lw["s5_glu_w"],
                                     lw["s5_glu_b"]).reshape(seq, batch * S5_WIDTH)
    y_s = glu(pl_, ys_lat, t_len)
    y_sc = glu(pc_, ys_ctx, c_len) if need_ctx else None

    m = _merge(y_a, y_r, y_s, pl_["gates"], lw["w_branch_a"], lw["w_branch_r"], lw["w_branch_s"], t_len, batch)
    m_c = None
    if need_ctx:
        m_c = _merge(y_ac, y_rc, y_sc, pc_["gates"], lw["w_branch_a"], lw["w_branch_r"], lw["w_branch_s"],
                     c_len, batch)
    return m, m_c


def _out_proj_residual(m, w_out, x2, mod, gate_k, rows_per_batch):
    n, d = x2.shape
    tm = _tile(n if rows_per_batch is None else rows_per_batch, 512)
    tn = _tile(d, 1024, LANES)
    nd = d // tn
    mrow = _mod_row_fn(rows_per_batch, tm)

    def epilogue(acc, xb, g):
        return xb + g[0] * acc

    return _matmul(m, w_out, F32, tm=tm, tn=tn, epilogue=epilogue,
                   extras=[(x2, pl.BlockSpec((tm, tn), lambda i, j: (i, j))),
                           (mod, pl.BlockSpec((1, 1, tn), lambda i, j: (mrow(i), 0, gate_k * nd + j)))],
                   name="out_proj")


def _prepare_layer(i, p):
    w_in = p["w_in"][i]
    c0 = DA_WIDTH
    c1 = 2 * DA_WIDTH
    c2 = c1 + RW_STATE_COLS
    c3 = c2 + S5_WIDTH
    c4 = c3 + DA_WIDTH
    c5 = c4 + RW_OUT_COLS
    bf = lambda a: a.astype(BF16)
    lw = {
        "w_k": bf(w_in[:, :c0]), "w_v": bf(w_in[:, c0:c1]),
        "w_rw": bf(jnp.concatenate([w_in[:, c1:c2], w_in[:, c4:c5]], axis=1)),
        "w_s5": bf(w_in[:, c2:c3]), "w_q": bf(w_in[:, c3:c4]), "w_gates": bf(w_in[:, c5:]),
    }
    for name in ("da_q_norm", "da_k_norm", "da_lambda_q", "da_lambda_k", "da_subln", "rw_conv", "rw_k_k", "rw_k_a",
                 "rw_gn_w", "rw_gn_b", "s5_d", "s5_glu_b", "router_bias"):
        lw[name] = p[name][i].astype(F32)
    w = RW_WIDTH
    lora = jnp.zeros((2 * RW_DECAY_RANK + 2 * RW_A_RANK, 4 * w), F32)
    r0 = 0
    for blk, src in enumerate((p["rw_w2"][i][0], p["rw_w2"][i][1], p["rw_a2"][i][0], p["rw_a2"][i][1])):
        lora = lora.at[r0:r0 + src.shape[0], blk * w:(blk + 1) * w].set(src.astype(F32))
        r0 += src.shape[0]
    lw["rw_lora"] = jnp.stack(_hilo(lora))
    lw["rw_w0a0"] = jnp.concatenate([p["rw_w0"][i][0], p["rw_w0"][i][1], p["rw_a0"][i][0], p["rw_a0"][i][1]]
                                    ).astype(F32).reshape(1, 4 * w)
    lw["rw_r_k"] = p["rw_r_k"][i].astype(F32).reshape(w)
    lw["rw_g2"] = bf(p["rw_g2"][i])
    for d, name in enumerate(("s5_f", "s5_b")):
        lw[name] = _s5_dir_params(p["s5_lambda_re"][i][d], p["s5_lambda_im"][i][d], p["s5_log_dt"][i][d],
                                  p["s5_b_re"][i].astype(F32), p["s5_b_im"][i].astype(F32),
                                  p["s5_c_re"][i][d], p["s5_c_im"][i][d])
    lw["s5_glu_w"] = bf(p["s5_glu_w"][i])
    for name in ("w_branch_a", "w_branch_r", "w_branch_s", "w_out"):
        lw[name] = bf(p[name][i])
    for name in ("exp_w_gate", "exp_w_up", "exp_w_down"):
        lw[name] = p[name]
    lw["layer"] = i
    d_model = w_in.shape[0]
    rw_t = p["router_w"][i].astype(F32).T
    lw["router_w_t"] = jnp.concatenate([rw_t, jnp.zeros((LANES - N_EXPERTS, d_model), F32)], axis=0)
    for name in ("sh_w_gate", "sh_w_up", "sh_w_down"):
        lw[name] = bf(p[name][i])
    return lw


def kernel(x, c, ctx, c_ctx, ada_w, ada_b, w_in, da_q_norm, da_k_norm, da_lambda_q, da_lambda_k, da_subln, rw_conv, rw_w0, rw_w2, rw_a0, rw_a2, rw_g2, rw_k_k, rw_k_a, rw_r_k, rw_gn_w, rw_gn_b, s5_lambda_re, s5_lambda_im, s5_log_dt, s5_b_re, s5_b_im, s5_c_re, s5_c_im, s5_d, s5_glu_w, s5_glu_b, w_branch_a, w_branch_r, w_branch_s, w_out, router_w, router_bias, exp_w_gate, exp_w_up, exp_w_down, sh_w_gate, sh_w_up, sh_w_down):
    params = dict(w_in=w_in, da_q_norm=da_q_norm, da_k_norm=da_k_norm, da_lambda_q=da_lambda_q,
                  da_lambda_k=da_lambda_k, da_subln=da_subln, rw_conv=rw_conv, rw_w0=rw_w0, rw_w2=rw_w2,
                  rw_a0=rw_a0, rw_a2=rw_a2, rw_g2=rw_g2, rw_k_k=rw_k_k, rw_k_a=rw_k_a, rw_r_k=rw_r_k,
                  rw_gn_w=rw_gn_w, rw_gn_b=rw_gn_b, s5_lambda_re=s5_lambda_re, s5_lambda_im=s5_lambda_im,
                  s5_log_dt=s5_log_dt, s5_b_re=s5_b_re, s5_b_im=s5_b_im, s5_c_re=s5_c_re, s5_c_im=s5_c_im,
                  s5_d=s5_d, s5_glu_w=s5_glu_w, s5_glu_b=s5_glu_b, w_branch_a=w_branch_a, w_branch_r=w_branch_r,
                  w_branch_s=w_branch_s, w_out=w_out, router_w=router_w, router_bias=router_bias,
                  exp_w_gate=exp_w_gate, exp_w_up=exp_w_up, exp_w_down=exp_w_down, sh_w_gate=sh_w_gate,
                  sh_w_up=sh_w_up, sh_w_down=sh_w_down)
    batch, t_len, d_model = x.shape
    c_len = ctx.shape[1]
    depth = ada_w.shape[0]
    assert batch <= MOD_CTX_ROW
    tabs = _rope_tables(t_len)
    cvec = jnp.zeros((2 * SUBLANES, d_model), F32).at[:batch].set(c.astype(F32)).at[MOD_CTX_ROW].set(c_ctx.astype(F32))
    x2 = x.astype(F32).reshape(batch * t_len, d_model)
    ctx2 = ctx.astype(F32).reshape(batch * c_len, d_model)
    for i in range(depth):
        lw = _prepare_layer(i, params)
        need_ctx = i < depth - 1
        lam_init = 0.8 - 0.6 * math.exp(-0.3 * i)
        mod = _ada_table(cvec, ada_w, ada_b, i)
        h = _modulate(x2, mod, 0, 1, t_len)
        hc = _modulate(ctx2, mod, 0, 1, None)
        m, m_c = _token_mixer(h, hc, lw, lam_init, tabs, need_ctx, batch, t_len, c_len)
        x2 = _out_proj_residual(m, lw["w_out"], x2, mod, 2, t_len)
        streams = [(x2, t_len)]
        if need_ctx:
            ctx2 = _out_proj_residual(m_c, lw["w_out"], ctx2, mod, 2, None)
            streams.append((ctx2, None))
        routed = [_modulate_route(xs, mod, 3, 4, rpb, lw["router_w_t"], lw["router_bias"]) for xs, rpb in streams]
        f_r = _moe_routed(jnp.concatenate([r[1] for r in routed], axis=0),
                          jnp.concatenate([r[2] for r in routed], axis=1),
                          sum(r[3] for r in routed),
                          lw["exp_w_gate"], lw["exp_w_up"], lw["exp_w_down"], lw["layer"])
        outs, row0 = [], 0
        for (xs, rpb), r in zip(streams, routed):
            outs.append(_moe_shared_residual(r[0], lw["sh_w_gate"], lw["sh_w_up"], lw["sh_w_down"], f_r, row0,
                                             xs, mod, 5, rpb))
            row0 += xs.shape[0]
        x2 = outs[0]
        if need_ctx:
            ctx2 = outs[1]
    return x2.reshape(batch, t_len, d_model).astype(x.dtype)
```

```python
import functools
import math

import jax
import jax.numpy as jnp
from jax import lax
from jax.experimental import pallas as pl
from jax.experimental.pallas import tpu as pltpu

F32 = jnp.float32
BF16 = jnp.bfloat16

GRID_W = 64
NORM_EPS = 1e-6
DA_HEADS = 16
DA_HEAD_DIM = 64
DA_V_DIM = 2 * DA_HEAD_DIM
DA_WIDTH = DA_HEADS * DA_V_DIM
DA_SCALE = DA_HEAD_DIM ** -0.5
DA_Q_SCALE = DA_SCALE * math.log2(math.e)
SAFE_SHIFT = 60.0
ROPE_BASE = 10000.0
RW_HEADS = 16
RW_HEAD_DIM = 64
RW_WIDTH = RW_HEADS * RW_HEAD_DIM
RW_DECAY_RANK = 64
RW_A_RANK = 64
RW_GATE_RANK = 128
RW_GN_EPS = 64e-5
RW_STATE_COLS = 2 * RW_WIDTH + 2 * RW_DECAY_RANK + 2 * RW_A_RANK
RW_OUT_COLS = RW_WIDTH + RW_GATE_RANK
S5_GROUP = 16
S5_GROUPS = 64
S5_WIDTH = S5_GROUPS * S5_GROUP
S5_STATE = 64
S5_LANES = S5_GROUPS * S5_STATE
N_EXPERTS = 64
TOP_K = 8
N_GROUPS = 8
TOPK_GROUPS = 4
EXPERT_FF = 256
ROUTED_SCALE = 2.5

LANES = 128
SUBLANES = 8
VMEM_LIMIT_BYTES = 56 * 1024 * 1024

RW_CHUNK = 64
RW_GROUP_HEADS = 4
S5_CHUNK = 64
MOD_CTX_ROW = 8
MOE_TILE = 256


def _cparams(sem):
    return pltpu.CompilerParams(dimension_semantics=sem, vmem_limit_bytes=VMEM_LIMIT_BYTES)


def _tile(n, pref, mult=SUBLANES):
    if n <= pref:
        return n
    t = (pref // mult) * mult
    while t > mult and n % t:
        t -= mult
    assert n % t == 0, (n, pref)
    return t


def _dot(a, b):
    return jnp.dot(a.astype(BF16), b.astype(BF16), preferred_element_type=F32)


def _dot_nt(a, b):
    return lax.dot_general(a.astype(BF16), b.astype(BF16), (((1,), (1,)), ((), ())),
                           preferred_element_type=F32)


def _dot_tn(a, b):
    return lax.dot_general(a.astype(BF16), b.astype(BF16), (((0,), (0,)), ((), ())),
                           preferred_element_type=F32)


def _hilo(x):
    hi = x.astype(BF16)
    lo = (x - hi.astype(F32)).astype(BF16)
    return hi, lo


def _dot_hp_lhs(a, b_exact):
    hi, lo = _hilo(a)
    return (jnp.dot(hi, b_exact, preferred_element_type=F32)
            + jnp.dot(lo, b_exact, preferred_element_type=F32))


def _dot_hp(a, b):
    ah, al = _hilo(a)
    bh, bl = _hilo(b)
    return (jnp.dot(ah, bh, preferred_element_type=F32) + jnp.dot(al, bh, preferred_element_type=F32)
            + jnp.dot(ah, bl, preferred_element_type=F32))


def _sigmoid(x):
    return 1.0 / (1.0 + jnp.exp(-x))


def _softplus(x):
    return jnp.maximum(x, 0.0) + jnp.log(1.0 + jnp.exp(-jnp.abs(x)))


def _gelu_tanh(x):
    c = math.sqrt(2.0 / math.pi)
    return 0.5 * x * (1.0 + jnp.tanh(c * (x + 0.044715 * (x * x * x))))


def _mm_body(a_ref, b_ref, *rest, n_extra, prologue, epilogue):
    extras = rest[:n_extra]
    o_ref = rest[n_extra]
    a = a_ref[...]
    if prologue is not None:
        a = prologue(a)
    b = b_ref[0] if len(b_ref.shape) == 3 else b_ref[...]
    acc = jnp.dot(a.astype(BF16), b.astype(BF16), preferred_element_type=F32)
    if epilogue is not None:
        acc = epilogue(acc, *[e[...] for e in extras])
    o_ref[...] = acc.astype(o_ref.dtype)


def _matmul(a, b, out_dtype, *, tm=1024, tn=1024, prologue=None, epilogue=None, extras=(),
            a_spec=None, out_spec=None, out_shape=None, grid_m=None, b_layer=None, name="matmul"):
    k, n = b.shape[-2:]
    tn = _tile(n, tn, LANES)
    if b_layer is None:
        b_spec = pl.BlockSpec((k, tn), lambda i, j: (0, j))
    else:
        b_spec = pl.BlockSpec((1, k, tn), lambda i, j: (b_layer, 0, j))
    if a_spec is None:
        m = a.shape[0]
        tm = _tile(m, tm)
        grid_m = m // tm
        a_spec = pl.BlockSpec((tm, k), lambda i, j: (i, 0))
    if out_spec is None:
        out_spec = pl.BlockSpec((tm, tn), lambda i, j: (i, j))
        out_shape = (a.shape[0], n)
    body = functools.partial(_mm_body, n_extra=len(extras), prologue=prologue, epilogue=epilogue)
    return pl.pallas_call(
        body,
        grid=(grid_m, n // tn),
        in_specs=[a_spec, b_spec] + [s for _, s in extras],
        out_specs=out_spec,
        out_shape=jax.ShapeDtypeStruct(out_shape, out_dtype),
        compiler_params=_cparams(("parallel", "arbitrary")),
        name=name,
    )(a, b, *[x for x, _ in extras])


def _ada_table(cvec, ada_w, ada_b, layer):
    d6 = ada_w.shape[2]

    def prologue(a):
        return a * _sigmoid(a)

    def epilogue(acc, bias):
        return acc + bias

    out = _matmul(cvec, ada_w, F32, tm=16, tn=512, prologue=prologue, epilogue=epilogue, b_layer=layer,
                  extras=[(ada_b[layer].reshape(1, d6), pl.BlockSpec((1, 512), lambda i, j: (0, j)))],
                  name="ada_table")
    return out.reshape(cvec.shape[0], 1, d6)


def _mod_row_fn(rows_per_batch, tm):
    if rows_per_batch is None:
        return lambda i: MOD_CTX_ROW
    nb = rows_per_batch // tm
    return lambda i: i // nb


def _modulate_body(x_ref, sh_ref, sc_ref, o_ref):
    x = x_ref[...]
    ms = jnp.mean(x * x, axis=-1, keepdims=True)
    h = x * lax.rsqrt(ms + NORM_EPS) * (1.0 + sc_ref[0]) + sh_ref[0]
    o_ref[...] = h.astype(o_ref.dtype)


def _modulate(x2, mod, shift_k, scale_k, rows_per_batch):
    n, d = x2.shape
    tm = _tile(n if rows_per_batch is None else rows_per_batch, 256)
    row = _mod_row_fn(rows_per_batch, tm)
    return pl.pallas_call(
        _modulate_body,
        grid=(n // tm,),
        in_specs=[pl.BlockSpec((tm, d), lambda i: (i, 0)),
                  pl.BlockSpec((1, 1, d), lambda i: (row(i), 0, shift_k)),
                  pl.BlockSpec((1, 1, d), lambda i: (row(i), 0, scale_k))],
        out_specs=pl.BlockSpec((tm, d), lambda i: (i, 0)),
        out_shape=jax.ShapeDtypeStruct((n, d), BF16),
        compiler_params=_cparams(("parallel",)),
        name="modulate",
    )(x2, mod, mod)


def _pack_halves(x):
    w = x.shape[1] // 2
    lo = lax.bitcast_convert_type(x[:, :w].astype(BF16).astype(F32), jnp.uint32)
    hi = lax.bitcast_convert_type(x[:, w:].astype(BF16).astype(F32), jnp.uint32)
    return lax.shift_right_logical(lo, jnp.uint32(16)) | hi


def _unpack_halves(p):
    lo = lax.bitcast_convert_type(lax.shift_left(p, jnp.uint32(16)), F32)
    hi = lax.bitcast_convert_type(p & jnp.uint32(0xFFFF0000), F32)
    return lo, hi


def _route_body(x_ref, sh_ref, sc_ref, wr_ref, bias_ref, h_ref, hp_ref, comb_ref, cnt_ref):
    x = x_ref[...]
    ms = jnp.mean(x * x, axis=-1, keepdims=True)
    h = x * lax.rsqrt(ms + NORM_EPS) * (1.0 + sc_ref[0]) + sh_ref[0]
    h_ref[...] = h.astype(h_ref.dtype)
    hp_ref[...] = _pack_halves(h)
    tm = x.shape[0]
    wr = wr_ref[...]
    hh, hl = _hilo(h)
    wh, wl = _hilo(wr)
    logits = _dot_nt(wh, hh) + _dot_nt(wl, hh) + _dot_nt(wh, hl)
    scores = _sigmoid(logits[:N_EXPERTS])
    per_group = N_EXPERTS // N_GROUPS
    sc3 = scores.reshape(N_GROUPS, per_group, tm)
    sel = sc3 + bias_ref[...]
    midx = lax.broadcasted_iota(jnp.int32, sel.shape, 1)
    neg = jnp.float32(-jnp.inf)
    m1 = jnp.max(sel, axis=1, keepdims=True)
    first = jnp.min(jnp.where(sel == m1, midx, per_group), axis=1, keepdims=True)
    m2 = jnp.max(jnp.where(midx == first, neg, sel), axis=1, keepdims=True)
    gs = (m1 + m2).reshape(N_GROUPS, tm)
    gidx = lax.broadcasted_iota(jnp.int32, gs.shape, 0)
    gmask = jnp.zeros(gs.shape, jnp.bool_)
    for _ in range(TOPK_GROUPS):
        m = jnp.max(gs, axis=0, keepdims=True)
        f = jnp.min(jnp.where(gs == m, gidx, N_GROUPS), axis=0, keepdims=True)
        pick = gidx == f
        gmask = jnp.logical_or(gmask, pick)
        gs = jnp.where(pick, neg, gs)
    val = jnp.where(gmask.reshape(N_GROUPS, 1, tm), sel, neg)
    eidx = lax.broadcasted_iota(jnp.int32, sel.shape, 0) * per_group + midx
    chosen = jnp.zeros(sel.shape, jnp.bool_)
    for _ in range(TOP_K):
        m = jnp.max(jnp.max(val, axis=1, keepdims=True), axis=0, keepdims=True)
        f = jnp.min(jnp.min(jnp.where(val == m, eidx, N_EXPERTS), axis=1, keepdims=True), axis=0, keepdims=True)
        pick = eidx == f
        chosen = jnp.logical_or(chosen, pick)
        val = jnp.where(pick, neg, val)
    w = jnp.where(chosen, sc3, 0.0)
    wsum = jnp.sum(jnp.sum(w, axis=1, keepdims=True), axis=0, keepdims=True)
    comb = (w / wsum * ROUTED_SCALE).reshape(N_EXPERTS, tm)
    comb_ref[...] = comb

    @pl.when(pl.program_id(0) == 0)
    def _():
        cnt_ref[...] = jnp.zeros_like(cnt_ref)

    cnt_ref[...] += jnp.sum((comb > 0.0).astype(F32), axis=1, keepdims=True)


def _modulate_route(x2, mod, shift_k, scale_k, rows_per_batch, router_w_t, router_bias):
    n, d = x2.shape
    tm = _tile(n if rows_per_batch is None else rows_per_batch, 256, LANES)
    row = _mod_row_fn(rows_per_batch, tm)
    return pl.pallas_call(
        _route_body,
        grid=(n // tm,),
        in_specs=[pl.BlockSpec((tm, d), lambda i: (i, 0)),
                  pl.BlockSpec((1, 1, d), lambda i: (row(i), 0, shift_k)),
                  pl.BlockSpec((1, 1, d), lambda i: (row(i), 0, scale_k)),
                  pl.BlockSpec((LANES, d), lambda i: (0, 0)),
                  pl.BlockSpec((N_GROUPS, N_EXPERTS // N_GROUPS, 1), lambda i: (0, 0, 0))],
        out_specs=[pl.BlockSpec((tm, d), lambda i: (i, 0)),
                   pl.BlockSpec((tm, d // 2), lambda i: (i, 0)),
                   pl.BlockSpec((N_EXPERTS, tm), lambda i: (0, i)),
                   pl.BlockSpec((N_EXPERTS, LANES), lambda i: (0, 0))],
        out_shape=[jax.ShapeDtypeStruct((n, d), BF16), jax.ShapeDtypeStruct((n, d // 2), jnp.uint32),
                   jax.ShapeDtypeStruct((N_EXPERTS, n), F32), jax.ShapeDtypeStruct((N_EXPERTS, LANES), F32)],
        compiler_params=_cparams(("arbitrary",)),
        name="modulate_route",
    )(x2, mod, mod, router_w_t, router_bias.reshape(N_GROUPS, N_EXPERTS // N_GROUPS, 1))


def _group_ones(width, group):
    r = lax.broadcasted_iota(jnp.int32, (width, width), 0) // group
    c = lax.broadcasted_iota(jnp.int32, (width, width), 1) // group
    return (r == c).astype(BF16)


def _qk_norm_rope(x, gain, cos, sin, scale):
    ss = _dot_hp_lhs(x * x, _group_ones(LANES, DA_HEAD_DIM))
    xn = x * lax.rsqrt(ss * (1.0 / DA_HEAD_DIM) + NORM_EPS) * gain
    lane = lax.broadcasted_iota(jnp.int32, x.shape, 1)
    quarter = DA_HEAD_DIM // 4
    partner = jnp.where((lane % (2 * quarter)) < quarter,
                        pltpu.roll(xn, LANES - quarter, axis=1),
                        pltpu.roll(xn, quarter, axis=1))
    return (xn * cos + partner * sin) * scale


def _proj_qk(h, w, gain, cos, sin, scale, t_len, name):
    n = h.shape[0]
    tm = _tile(n if t_len is None else t_len, 512)
    nb = 1 if t_len is None else t_len // tm
    tab = pl.BlockSpec((tm, LANES), lambda i, j: (i % nb, 0))
    gain2 = jnp.tile(gain.reshape(1, DA_HEAD_DIM), (1, 2))

    def epilogue(acc, g, c, s):
        heads = [_qk_norm_rope(acc[:, k * LANES:(k + 1) * LANES], g, c, s, scale)
                 for k in range(acc.shape[1] // LANES)]
        return jnp.concatenate(heads, axis=1)

    return _matmul(h, w, BF16, tm=tm, epilogue=epilogue,
                   extras=[(gain2, pl.BlockSpec((1, LANES), lambda i, j: (0, 0))), (cos, tab), (sin, tab)],
                   name=name)


def _rope_tables(t_len):
    rows = t_len // GRID_W
    row = jnp.repeat(jnp.arange(rows, dtype=F32), GRID_W)
    col = jnp.tile(jnp.arange(GRID_W, dtype=F32), rows)
    half = DA_HEAD_DIM // 2
    inv_freq = 1.0 / (ROPE_BASE ** (jnp.arange(0, half, 2, dtype=F32) / half))
    ang_r = row[:, None] * inv_freq
    ang_c = col[:, None] * inv_freq
    cos64 = jnp.concatenate([jnp.cos(ang_r), jnp.cos(ang_r), jnp.cos(ang_c), jnp.cos(ang_c)], axis=1)
    sin64 = jnp.concatenate([-jnp.sin(ang_r), jnp.sin(ang_r), -jnp.sin(ang_c), jnp.sin(ang_c)], axis=1)
    return jnp.tile(cos64, (1, 2)), jnp.tile(sin64, (1, 2))


def _attn_body(*refs, has_lat, lam_init):
    if has_lat:
        q_ref, kl_ref, vl_ref, kc_ref, vc_ref, lq_ref, lk_ref, sub_ref, o_ref = refs
    else:
        q_ref, kc_ref, vc_ref, lq_ref, lk_ref, sub_ref, o_ref = refs
    lqk = lq_ref[...] * lk_ref[...]
    lsum = jnp.sum(lqk, axis=1, keepdims=True)
    e = jnp.exp(lsum)
    lam = e[0:1, :] - e[1:2, :] + lam_init
    q = q_ref[...]
    lane = lax.broadcasted_iota(jnp.int32, q.shape, 1)
    zero = jnp.zeros_like(q)
    ext = lambda v: jnp.concatenate([v, jnp.ones_like(v)], axis=1)
    vc_ext = ext(vc_ref[...])
    vl_ext = ext(vl_ref[...]) if has_lat else None
    mixed = []
    for m in range(2):
        in_map = (lane // DA_HEAD_DIM) == m
        qm = jnp.where(in_map, q, zero)
        s_c = _dot_nt(qm, kc_ref[...]).astype(BF16)
        s_l = _dot_nt(qm, kl_ref[...]).astype(BF16) if has_lat else None
        key_norm2 = lambda k_ref: jnp.max(jnp.sum(jnp.where(
            (lax.broadcasted_iota(jnp.int32, k_ref.shape, 1) // DA_HEAD_DIM) == m,
            k_ref[...].astype(F32) ** 2, 0.0), axis=-1, keepdims=True), axis=0, keepdims=True)
        kn2 = key_norm2(kc_ref)
        if has_lat:
            kn2 = jnp.maximum(kn2, key_norm2(kl_ref))
        bound = jnp.sqrt(jnp.sum(qm.astype(F32) ** 2, axis=-1, keepdims=True) * kn2)

        def exact_max(s_c=s_c, s_l=s_l):
            mx = jnp.max(s_c, axis=-1, keepdims=True)
            if has_lat:
                mx = jnp.maximum(mx, jnp.max(s_l, axis=-1, keepdims=True))
            return mx

        mx = lax.cond(jnp.max(bound) <= SAFE_SHIFT, lambda bound=bound: bound.astype(BF16), exact_max)
        acc = jnp.dot(jnp.exp2(s_c - mx), vc_ext, preferred_element_type=F32)
        if has_lat:
            acc = acc + jnp.dot(jnp.exp2(s_l - mx), vl_ext, preferred_element_type=F32)
        mixed.append(acc[:, :DA_V_DIM] / acc[:, DA_V_DIM:])
    o = mixed[0] - lam * mixed[1]
    ms = jnp.mean(o * o, axis=-1, keepdims=True)
    o = o * lax.rsqrt(ms + NORM_EPS) * sub_ref[...] * (1.0 - lam_init)
    o_ref[...] = o.astype(o_ref.dtype)


def _diff_attention(q, k_lat, v_lat, k_ctx, v_ctx, lq, lk, subln, lam_init, batch):
    n, w = q.shape
    tq_len = n // batch
    c_len = k_ctx.shape[0] // batch
    tq = _tile(tq_len, 2048)
    nq = tq_len // tq
    has_lat = k_lat is not None
    blk = lambda rows: pl.BlockSpec((rows, LANES), lambda b, h, i: (b, h))
    in_specs = [pl.BlockSpec((tq, LANES), lambda b, h, i: (b * nq + i, h))]
    args = [q]
    if has_lat:
        t_len = k_lat.shape[0] // batch
        in_specs += [blk(t_len), blk(t_len)]
        args += [k_lat, v_lat]
    in_specs += [blk(c_len), blk(c_len),
                 pl.BlockSpec((2, DA_HEAD_DIM), lambda b, h, i: (0, 0)),
                 pl.BlockSpec((2, DA_HEAD_DIM), lambda b, h, i: (0, 0)),
                 pl.BlockSpec((1, LANES), lambda b, h, i: (0, 0))]
    args += [k_ctx, v_ctx, lq, lk, subln.reshape(1, DA_V_DIM)]
    return pl.pallas_call(
        functools.partial(_attn_body, has_lat=has_lat, lam_init=lam_init),
        grid=(batch, w // LANES, nq),
        in_specs=in_specs,
        out_specs=pl.BlockSpec((tq, LANES), lambda b, h, i: (b * nq + i, h)),
        out_shape=jax.ShapeDtypeStruct((n, w), BF16),
        compiler_params=_cparams(("parallel", "parallel", "arbitrary")),
        name="diff_attention",
    )(*args)


def _rwkv_prep_body(x_ref, xp_ref, xn_ref, conv_ref, lora_ref, w0a0_ref, kk_w_ref, ka_w_ref, ones_ref,
                    lwf_ref, lwb_ref, kk_ref, kkaf_ref, kkab_ref, kdf_ref, kdb_ref, v_ref, r_ref, gd_ref,
                    *, blocks_per_seq):
    i = pl.program_id(0)
    x = x_ref[...]
    tm = x.shape[0]
    first = (i % blocks_per_seq) == 0
    last = (i % blocks_per_seq) == blocks_per_seq - 1
    xp = jnp.where(first, 0.0, xp_ref[SUBLANES - 1:SUBLANES, :])
    xn = jnp.where(last, 0.0, xn_ref[0:1, :])
    row = lax.broadcasted_iota(jnp.int32, (tm, 1), 0)
    up = jnp.where(row == 0, xp, pltpu.roll(x, 1, axis=0))
    dn = jnp.where(row == tm - 1, xn, pltpu.roll(x, tm - 1, axis=0))
    cw = conv_ref[...]
    cv = up * cw[0:1, :] + x * cw[1:2, :] + dn * cw[2:3, :]
    w = RW_WIDTH
    k = cv[:, :w]
    v = cv[:, w:2 * w]
    lora_in = cv[:, 2 * w:RW_STATE_COLS]
    r = cv[:, RW_STATE_COLS:RW_STATE_COLS + w]
    gd = cv[:, RW_STATE_COLS + w:]
    lane = lax.broadcasted_iota(jnp.int32, lora_in.shape, 1)
    li = jnp.where(lane < 2 * RW_DECAY_RANK, jnp.tanh(lora_in), lora_in)
    li_hi, li_lo = _hilo(li)
    pre = (jnp.dot(li_hi, lora_ref[0], preferred_element_type=F32)
           + jnp.dot(li_lo, lora_ref[0], preferred_element_type=F32)
           + jnp.dot(li_hi, lora_ref[1], preferred_element_type=F32)) + w0a0_ref[...]
    kkr = k * kk_w_ref[...]
    ss = _dot_hp_lhs(kkr * kkr, ones_ref[...])
    kk = kkr * lax.rsqrt(ss + 1e-12)
    kk_ref[...] = kk
    v_ref[...] = v
    r_ref[...] = r
    gd_ref[...] = gd
    ka = ka_w_ref[...]
    for d, (lw_ref, kka_ref, kd_ref) in enumerate(((lwf_ref, kkaf_ref, kdf_ref), (lwb_ref, kkab_ref, kdb_ref))):
        lw_ref[...] = -math.exp(-0.5) * _sigmoid(pre[:, d * w:(d + 1) * w])
        a = _sigmoid(pre[:, (2 + d) * w:(3 + d) * w])
        kka_ref[...] = kk * a
        kd_ref[...] = k * (1.0 + (a - 1.0) * ka)


def _rwkv_prep(p_rw, seq_len, conv, lora_w, w0a0, k_k, k_a):
    n, c = p_rw.shape
    tm = _tile(seq_len, 128)
    bps = seq_len // tm
    sub = tm // SUBLANES
    nsub = n // SUBLANES
    w = RW_WIDTH
    wide = lambda: pl.BlockSpec((tm, w), lambda i: (i, 0))
    outs = [jax.ShapeDtypeStruct((n, w), F32)] * 9 + [jax.ShapeDtypeStruct((n, RW_GATE_RANK), F32)]
    return pl.pallas_call(
        functools.partial(_rwkv_prep_body, blocks_per_seq=bps),
        grid=(n // tm,),
        in_specs=[pl.BlockSpec((tm, c), lambda i: (i, 0)),
                  pl.BlockSpec((SUBLANES, c), lambda i: (jnp.maximum(i * sub - 1, 0), 0)),
                  pl.BlockSpec((SUBLANES, c), lambda i: (jnp.minimum((i + 1) * sub, nsub - 1), 0)),
                  pl.BlockSpec((3, c), lambda i: (0, 0)),
                  pl.BlockSpec(lora_w.shape, lambda i: (0, 0, 0)),
                  pl.BlockSpec((1, 4 * w), lambda i: (0, 0)),
                  pl.BlockSpec((1, w), lambda i: (0, 0)),
                  pl.BlockSpec((1, w), lambda i: (0, 0)),
                  pl.BlockSpec((w, w), lambda i: (0, 0))],
        out_specs=[wide() for _ in range(9)] + [pl.BlockSpec((tm, RW_GATE_RANK), lambda i: (i, 0))],
        out_shape=outs,
        compiler_params=_cparams(("parallel",)),
        name="rwkv_prep",
    )(p_rw, p_rw, p_rw, conv, lora_w, w0a0, k_k.reshape(1, w), k_a.reshape(1, w), _group_ones(w, RW_HEAD_DIM))


def _rwkv_scan_body(lwf_ref, kkaf_ref, kdf_ref, kkf_ref, vf_ref, rf_ref,
                    lwb_ref, kkab_ref, kdb_ref, kkb_ref, vb_ref, rb_ref, s0f_ref, s0b_ref,
                    of_ref, ob_ref, stf_ref, stb_ref, *, want_out):
    ci = pl.program_id(1)

    @pl.when(ci == 0)
    def _():
        stf_ref[...] = s0f_ref[...]
        stb_ref[...] = s0b_ref[...]

    n_l, width = lwf_ref.shape
    hd = RW_HEAD_DIM
    gw = RW_GROUP_HEADS * hd
    n_groups = width // gw
    row = lax.broadcasted_iota(jnp.int32, (n_l, n_l), 0)
    col = lax.broadcasted_iota(jnp.int32, (n_l, n_l), 1)
    trow = lax.broadcasted_iota(jnp.int32, (n_l, gw), 0)
    tcol = lax.broadcasted_iota(jnp.int32, (n_l, gw), 1) % hd
    eye = (tcol == trow).astype(F32)
    same_head = ((lax.broadcasted_iota(jnp.int32, (gw, gw), 0) // hd)
                 == (lax.broadcasted_iota(jnp.int32, (gw, gw), 1) // hd))
    reps = gw // n_l

    def bdiag(x):
        xb = x.astype(BF16)
        return jnp.where(same_head, jnp.concatenate([xb] * reps, axis=0), jnp.zeros((), BF16))

    units = []
    for reverse, refs in ((False, (lwf_ref, kkaf_ref, kdf_ref, kkf_ref, vf_ref, rf_ref, stf_ref, of_ref)),
                          (True, (lwb_ref, kkab_ref, kdb_ref, kkb_ref, vb_ref, rb_ref, stb_ref, ob_ref))):
        lw_ref, kka_ref, k_ref, kk_ref, v_ref, r_ref, st_ref, o_ref = refs
        lw = lw_ref[...]
        tri = ((col >= row) if reverse else (col <= row)).astype(BF16)
        lh, ll = _hilo(lw)
        c = jnp.dot(tri, lh, preferred_element_type=F32) + jnp.dot(tri, ll, preferred_element_type=F32)
        g_end = jnp.exp(c[0:1, :] if reverse else c[n_l - 1:n_l, :])
        e_inv = jnp.exp(-c)
        a_t = kk_ref[...] * jnp.exp(c - lw)
        r_t = r_ref[...] * jnp.exp(c)
        k_h = k_ref[...] * e_inv
        b_h = kka_ref[...] * e_inv
        v_all = v_ref[...]
        incl, strict = (tcol >= trow, tcol > trow) if reverse else (tcol <= trow, tcol < trow)
        for g in range(n_groups):
            s = slice(g * gw, (g + 1) * gw)
            units.append(dict(g=g, sl=s, incl=incl, strict=strict, st_ref=st_ref, o_ref=o_ref, state=st_ref[g],
                              ar=jnp.concatenate([a_t[:, s], r_t[:, s]], axis=0).astype(BF16),
                              k_h=k_h[:, s], b_h=b_h[:, s], v=v_all[:, s], g_end=g_end[:, s]))

    n_iter = max(1, (n_l - 1).bit_length()) - 1
    sc_k = [_dot_nt(x["ar"], bdiag(x["k_h"])) for x in units]
    sc_b = [_dot_nt(x["ar"], bdiag(x["b_h"])) for x in units]
    from_s = [_dot_nt(x["ar"], x["state"]) for x in units]
    p = [-jnp.where(x["strict"], sb[:n_l], 0.0) for x, sb in zip(units, sc_b)]
    t_inv = [eye + pi for pi in p]
    if n_iter:
        p = [_dot(pi, bdiag(pi)) for pi in p]
    for it in range(n_iter):
        if it + 1 < n_iter:
            y = [_dot(jnp.concatenate([ti, pi], axis=0), bdiag(pi)) for ti, pi in zip(t_inv, p)]
            t_inv = [ti + yi[:n_l] for ti, yi in zip(t_inv, y)]
            p = [yi[n_l:] for yi in y]
        else:
            t_inv = [ti + _dot(ti, bdiag(pi)) for ti, pi in zip(t_inv, p)]
    v_bd = [bdiag(x["v"]) for x in units]
    w = [fs[:n_l] + _dot(jnp.where(x["strict"], sk[:n_l], 0.0), vb)
         for x, fs, sk, vb in zip(units, from_s, sc_k, v_bd)]
    u = [_dot(ti, bdiag(wi)) for ti, wi in zip(t_inv, w)]
    if want_out:
        o_v = [_dot(jnp.where(x["incl"], sk[n_l:], 0.0), vb) for x, sk, vb in zip(units, sc_k, v_bd)]
        o_u = [_dot(jnp.where(x["incl"], sb[n_l:], 0.0), bdiag(ui)) for x, sb, ui in zip(units, sc_b, u)]
        for x, fs, ov, ou in zip(units, from_s, o_v, o_u):
            x["o_ref"][:, x["sl"]] = fs[n_l:] + ov - ou
    else:
        of_ref[...] = jnp.zeros_like(of_ref)
        ob_ref[...] = jnp.zeros_like(ob_ref)
    upd_k = [_dot_tn(x["v"], x["k_h"] * x["g_end"]) for x in units]
    upd_b = [_dot_tn(ui, x["b_h"] * x["g_end"]) for x, ui in zip(units, u)]
    for x, uk, ub in zip(units, upd_k, upd_b):
        x["st_ref"][x["g"]] = jnp.where(same_head, x["state"] * x["g_end"] + uk - ub, 0.0)


def _rwkv_scan(t, s0_f, s0_b, batch, *, want_out=True):
    n, w = t["kk"].shape
    seq = n // batch
    n_l = _tile(seq, RW_CHUNK)
    nch = seq // n_l
    gw = RW_GROUP_HEADS * RW_HEAD_DIM
    assert gw % n_l == 0 and w % gw == 0
    n_groups = w // gw
    fwd = pl.BlockSpec((n_l, w), lambda b, c: (b * nch + c, 0))
    rev = pl.BlockSpec((n_l, w), lambda b, c: (b * nch + nch - 1 - c, 0))
    st_spec = pl.BlockSpec((n_groups, gw, gw), lambda b, c: (b, 0, 0))
    seq_shape = jax.ShapeDtypeStruct((n, w), F32)
    st_shape = jax.ShapeDtypeStruct(s0_f.shape, F32)
    return pl.pallas_call(
        functools.partial(_rwkv_scan_body, want_out=want_out),
        grid=(batch, nch),
        in_specs=[fwd] * 6 + [rev] * 6 + [st_spec, st_spec],
        out_specs=[fwd, rev, st_spec, st_spec],
        out_shape=[seq_shape, seq_shape, st_shape, st_shape],
        compiler_params=_cparams(("parallel", "arbitrary")),
        name="rwkv_scan",
    )(t["lw_f"], t["kka_f"], t["kd_f"], t["kk"], t["v"], t["r"],
      t["lw_b"], t["kka_b"], t["kd_b"], t["kk"], t["v"], t["r"], s0_f, s0_b)


def _rwkv_readout_body(of_ref, ob_ref, r_ref, kdf_ref, kdb_ref, v_ref, gd_ref, gnw_ref, gnb_ref, rk_ref, g2_ref,
                       ones_ref, y_ref):
    ones = ones_ref[...]
    inv = 1.0 / RW_HEAD_DIM
    o = of_ref[...] + ob_ref[...]
    mu = _dot_hp_lhs(o, ones) * inv
    d = o - mu
    var = _dot_hp_lhs(d * d, ones) * inv
    on = d * lax.rsqrt(var + RW_GN_EPS) * gnw_ref[...] + gnb_ref[...]
    r = r_ref[...]
    rk = rk_ref[...]
    bonus = _dot_hp_lhs(r * kdf_ref[...] * rk, ones) + _dot_hp_lhs(r * kdb_ref[...] * rk, ones)
    y = on + bonus * v_ref[...]
    g = _dot(_sigmoid(gd_ref[...]), g2_ref[...])
    y_ref[...] = (y * g).astype(y_ref.dtype)


def _rwkv_readout(o_f, o_b, r, kd_f, kd_b, v, gd, gn_w, gn_b, r_k, g2):
    n, w = o_f.shape
    tm = _tile(n, 256)
    wide = pl.BlockSpec((tm, w), lambda i: (i, 0))
    vec = pl.BlockSpec((1, w), lambda i: (0, 0))
    return pl.pallas_call(
        _rwkv_readout_body,
        grid=(n // tm,),
        in_specs=[wide] * 6 + [pl.BlockSpec((tm, RW_GATE_RANK), lambda i: (i, 0)), vec, vec, vec,
                               pl.BlockSpec((RW_GATE_RANK, w), lambda i: (0, 0)),
                               pl.BlockSpec((w, w), lambda i: (0, 0))],
        out_specs=wide,
        out_shape=jax.ShapeDtypeStruct((n, w), BF16),
        compiler_params=_cparams(("parallel",)),
        name="rwkv_readout",
    )(o_f, o_b, r, kd_f, kd_b, v, gd, gn_w.reshape(1, w), gn_b.reshape(1, w), r_k.reshape(1, w), g2,
      _group_ones(w, RW_HEAD_DIM))


def _s5_scan_body(u_ref, bre_ref, bim_ref, are_ref, aim_ref, cre_ref, cim_ref, h0_ref, y_ref, ht_ref,
                  dre, dim, *, reverse, want_out):
    ci = pl.program_id(0)

    @pl.when(ci == 0)
    def _():
        ht_ref[...] = h0_ref[...]

    tt, nb, wu = u_ref.shape
    nblk = wu // LANES
    sw = S5_LANES // nblk
    u2 = u_ref[...].reshape(tt * nb, wu).astype(BF16)
    for c in range(nblk):
        uc = u2[:, c * LANES:(c + 1) * LANES]
        dre[:, :, c * sw:(c + 1) * sw] = jnp.dot(uc, bre_ref[c], preferred_element_type=F32).reshape(tt, nb, sw)
        dim[:, :, c * sw:(c + 1) * sw] = jnp.dot(uc, bim_ref[c], preferred_element_type=F32).reshape(tt, nb, sw)
    lw = 1024
    for c in range(S5_LANES // lw):
        ls = slice(c * lw, (c + 1) * lw)
        ar = jnp.broadcast_to(are_ref[:, ls], (nb, lw))
        ai = jnp.broadcast_to(aim_ref[:, ls], (nb, lw))

        def step(s, carry, ls=ls, ar=ar, ai=ai):
            t = (tt - 1 - s) if reverse else s
            hr, hi = carry
            nr = ar * hr - ai * hi + dre[t, :, ls]
            ni = ar * hi + ai * hr + dim[t, :, ls]
            dre[t, :, ls] = nr
            dim[t, :, ls] = ni
            return nr, ni

        hr, hi = lax.fori_loop(0, tt, step, (ht_ref[0, :, ls], ht_ref[1, :, ls]), unroll=2)
        ht_ref[0, :, ls] = hr
        ht_ref[1, :, ls] = hi
    if want_out:
        xr = dre[...].reshape(tt * nb, S5_LANES).astype(BF16)
        xi = dim[...].reshape(tt * nb, S5_LANES).astype(BF16)
        for c in range(nblk):
            yc = (jnp.dot(xr[:, c * sw:(c + 1) * sw], cre_ref[c], preferred_element_type=F32)
                  - jnp.dot(xi[:, c * sw:(c + 1) * sw], cim_ref[c], preferred_element_type=F32))
            y_ref[:, :, c * LANES:(c + 1) * LANES] = yc.reshape(tt, nb, LANES)
    else:
        y_ref[...] = jnp.zeros_like(y_ref)


def _s5_scan(u_tm, h0, p, *, reverse, want_out=True):
    t_len, nb, wu = u_tm.shape
    tt = _tile(t_len, S5_CHUNK)
    nch = t_len // tt
    chunk = (lambda c: (nch - 1 - c, 0, 0)) if reverse else (lambda c: (c, 0, 0))
    const3 = lambda a: pl.BlockSpec(a.shape, lambda c: (0, 0, 0))
    const2 = lambda a: pl.BlockSpec(a.shape, lambda c: (0, 0))
    return pl.pallas_call(
        functools.partial(_s5_scan_body, reverse=reverse, want_out=want_out),
        grid=(nch,),
        in_specs=[pl.BlockSpec((tt, nb, wu), chunk), const3(p["b_re"]), const3(p["b_im"]),
                  const2(p["a_re"]), const2(p["a_im"]), const3(p["c_re"]), const3(p["c_im"]), const3(h0)],
        out_specs=[pl.BlockSpec((tt, nb, wu), chunk), const3(h0)],
        out_shape=[jax.ShapeDtypeStruct(u_tm.shape, F32), jax.ShapeDtypeStruct(h0.shape, F32)],
        scratch_shapes=[pltpu.VMEM((tt, nb, S5_LANES), F32), pltpu.VMEM((tt, nb, S5_LANES), F32)],
        compiler_params=_cparams(("arbitrary",)),
        name="s5_scan_rev" if reverse else "s5_scan_fwd",
    )(u_tm, p["b_re"], p["b_im"], p["a_re"], p["a_im"], p["c_re"], p["c_im"], h0)


def _s5_dir_params(lam_re, lam_im, log_dt, b_re, b_im, c_re, c_im):
    g, pdim = lam_re.shape
    dt = jnp.exp(log_dt.astype(F32))[:, None]
    mag = jnp.exp(lam_re * dt)
    abar_re = mag * jnp.cos(lam_im * dt)
    abar_im = mag * jnp.sin(lam_im * dt)
    den = lam_re * lam_re + lam_im * lam_im
    nr = abar_re - 1.0
    g_re = (nr * lam_re + abar_im * lam_im) / den
    g_im = (abar_im * lam_re - nr * lam_im) / den
    bb_re = g_re[:, :, None] * b_re - g_im[:, :, None] * b_im
    bb_im = g_re[:, :, None] * b_im + g_im[:, :, None] * b_re
    gpb = LANES // S5_GROUP
    nblk = g // gpb
    eye = jnp.eye(gpb, dtype=F32)

    def drive_mat(bb):
        x = bb.reshape(nblk, gpb, pdim, S5_GROUP)
        x = jnp.einsum("cgph,gk->cghkp", x, eye)
        return x.reshape(nblk, gpb * S5_GROUP, gpb * pdim).astype(BF16)

    def read_mat(cc):
        x = cc.reshape(nblk, gpb, S5_GROUP, pdim)
        x = jnp.einsum("cghp,gk->cgpkh", x, eye)
        return x.reshape(nblk, gpb * pdim, gpb * S5_GROUP).astype(BF16)

    return {"a_re": abar_re.reshape(1, g * pdim), "a_im": abar_im.reshape(1, g * pdim),
            "b_re": drive_mat(bb_re), "b_im": drive_mat(bb_im),
            "c_re": read_mat(c_re.astype(F32)), "c_im": read_mat(c_im.astype(F32))}


def _s5_glu_body(u_ref, yf_ref, yb_ref, d_ref, w_ref, b_ref, o_ref):
    y = u_ref[...] * d_ref[...] + yf_ref[...] + yb_ref[...]
    y = _gelu_tanh(y)
    z = _dot(y, w_ref[...]) + b_ref[...]
    o_ref[...] = (y * _sigmoid(z)).astype(o_ref.dtype)


def _s5_glu(u, y_f, y_b, d_skip, glu_w, glu_b):
    n, w = u.shape
    tm = _tile(n, 512)
    wide = pl.BlockSpec((tm, w), lambda i: (i, 0))
    vec = pl.BlockSpec((1, w), lambda i: (0, 0))
    return pl.pallas_call(
        _s5_glu_body,
        grid=(n // tm,),
        in_specs=[wide, wide, wide, vec, pl.BlockSpec((w, w), lambda i: (0, 0)), vec],
        out_specs=wide,
        out_shape=jax.ShapeDtypeStruct((n, w), BF16),
        compiler_params=_cparams(("parallel",)),
        name="s5_glu",
    )(u, y_f, y_b, d_skip.reshape(1, w), glu_w, glu_b.reshape(1, w))


def _merge_body(ya_ref, yr_ref, ys_ref, wa_ref, wr_ref, ws_ref, ga_ref, gr_ref, gs_ref, o_ref):
    m = (ga_ref[...].astype(F32) * jnp.dot(ya_ref[...], wa_ref[...], preferred_element_type=F32)
         + gr_ref[...].astype(F32) * jnp.dot(yr_ref[...], wr_ref[...], preferred_element_type=F32)
         + gs_ref[...].astype(F32) * jnp.dot(ys_ref[...], ws_ref[...], preferred_element_type=F32))
    o_ref[...] = m.astype(o_ref.dtype)


def _merge(y_a, y_r, y_s_tm, gates, w_a, w_r, w_s, seq_len, batch):
    n = y_a.shape[0]
    d = w_a.shape[1]
    tm = _tile(seq_len, 512)
    tn = _tile(d, 1024, LANES)
    nt = seq_len // tm
    nd = d // tn
    row = lambda i, j: (i, 0)
    return pl.pallas_call(
        _merge_body,
        grid=(n // tm, nd),
        in_specs=[pl.BlockSpec((tm, y_a.shape[1]), row),
                  pl.BlockSpec((tm, y_r.shape[1]), row),
                  pl.BlockSpec((tm, S5_WIDTH), lambda i, j: (i % nt, i // nt)),
                  pl.BlockSpec((w_a.shape[0], tn), lambda i, j: (0, j)),
                  pl.BlockSpec((w_r.shape[0], tn), lambda i, j: (0, j)),
                  pl.BlockSpec((w_s.shape[0], tn), lambda i, j: (0, j)),
                  pl.BlockSpec((tm, tn), lambda i, j: (i, j)),
                  pl.BlockSpec((tm, tn), lambda i, j: (i, nd + j)),
                  pl.BlockSpec((tm, tn), lambda i, j: (i, 2 * nd + j))],
        out_specs=pl.BlockSpec((tm, tn), lambda i, j: (i, j)),
        out_shape=jax.ShapeDtypeStruct((n, d), BF16),
        compiler_params=_cparams(("parallel", "arbitrary")),
        name="merge",
    )(y_a, y_r, y_s_tm, w_a, w_r, w_s, gates, gates, gates)


def _moe_plan_body(comb_ref, cnt_ref, pos_ref, w_ref, te_ref, carry_ref, *, tile_rows, dummy_row):
    i = pl.program_id(0)
    ne, tm = comb_ref.shape
    tiles = jnp.floor((cnt_ref[...] + (tile_rows - 1)) * (1.0 / tile_rows))
    er = lax.broadcasted_iota(jnp.int32, (ne, ne), 0)
    ec = lax.broadcasted_iota(jnp.int32, (ne, ne), 1)
    t_hi, t_lo = _hilo(tiles)
    lower = (ec < er).astype(BF16)
    off_tiles = (jnp.dot(lower, t_hi, preferred_element_type=F32)
                 + jnp.dot(lower, t_lo, preferred_element_type=F32))

    @pl.when(i == 0)
    def _():
        carry_ref[...] = jnp.zeros_like(carry_ref)
        ntp = te_ref.shape[1]
        end_tiles = (off_tiles + tiles)[:, 0:1]
        tile_idx = lax.broadcasted_iota(jnp.int32, (1, ntp), 1).astype(F32)
        expert = jnp.sum((end_tiles <= tile_idx).astype(F32), axis=0, keepdims=True)
        expert = jnp.minimum(expert, ne - 1.0)
        valid = (tile_idx < jnp.max(end_tiles, axis=0, keepdims=True)).astype(F32)
        cnt = cnt_ref[:, 0:1]
        on_lane = lax.broadcasted_iota(jnp.int32, (ne, ntp), 0) == lax.broadcasted_iota(jnp.int32, (ne, ntp), 1)
        to_lanes = lambda col: jnp.sum(jnp.where(on_lane, col, 0.0), axis=0, keepdims=True)
        pad_start = to_lanes(off_tiles[:, 0:1] * tile_rows + cnt)
        pad_len = to_lanes(tiles[:, 0:1] * tile_rows - cnt)
        r8 = lax.broadcasted_iota(jnp.int32, te_ref.shape, 0)
        rows = jnp.where(r8 == 0, expert, jnp.where(r8 == 1, valid, jnp.where(r8 == 2, pad_start,
                                                                               jnp.where(r8 == 3, pad_len, 0.0))))
        te_ref[...] = rows.astype(jnp.int32)

    comb = comb_ref[...]
    chosen = comb > 0.0
    chf = chosen.astype(BF16)
    tr = lax.broadcasted_iota(jnp.int32, (tm, tm), 0)
    tc = lax.broadcasted_iota(jnp.int32, (tm, tm), 1)
    rank = jnp.dot(chf, (tr < tc).astype(BF16), preferred_element_type=F32)
    pos = off_tiles[:, 0:1] * tile_rows + carry_ref[:, 0:1] + rank
    carry_ref[...] += jnp.sum(chosen.astype(F32), axis=1, keepdims=True)
    eidx = lax.broadcasted_iota(jnp.int32, (ne, tm), 0)
    tok = lax.broadcasted_iota(jnp.int32, (1, tm), 1).astype(F32)
    remaining = chosen
    pos_rows, w_rows = [], []
    for k in range(TOP_K):
        first = jnp.min(jnp.where(remaining, eidx, ne), axis=0, keepdims=True)
        pick = eidx == first
        pos_k = jnp.sum(jnp.where(pick, pos, 0.0), axis=0, keepdims=True)
        pos_rows.append(jnp.where(first < ne, pos_k, float(dummy_row + k * tm) + tok))
        w_rows.append(jnp.sum(jnp.where(pick, comb, 0.0), axis=0, keepdims=True))
        remaining = jnp.logical_and(remaining, jnp.logical_not(pick))
    pos_ref[...] = jnp.concatenate(pos_rows, axis=0).astype(jnp.int32)
    wmat = jnp.concatenate(w_rows + [jnp.zeros((LANES - TOP_K, tm), F32)], axis=0)
    w_ref[...] = wmat.T


def _moe_token_tile(n):
    return _tile(n, 256, LANES)


def _moe_plan(comb_t, counts, n_tiles, dummy_row):
    ne, n = comb_t.shape
    tm = _moe_token_tile(n)
    ntp = -(-n_tiles // LANES) * LANES
    return pl.pallas_call(
        functools.partial(_moe_plan_body, tile_rows=MOE_TILE, dummy_row=dummy_row),
        grid=(n // tm,),
        in_specs=[pl.BlockSpec((ne, tm), lambda i: (0, i)),
                  pl.BlockSpec((ne, LANES), lambda i: (0, 0))],
        out_specs=[pl.BlockSpec((TOP_K, tm), lambda i: (0, i)),
                   pl.BlockSpec((tm, LANES), lambda i: (i, 0)),
                   pl.BlockSpec((SUBLANES, ntp), lambda i: (0, 0))],
        out_shape=[jax.ShapeDtypeStruct((TOP_K, n), jnp.int32), jax.ShapeDtypeStruct((n, LANES), F32),
                   jax.ShapeDtypeStruct((SUBLANES, ntp), jnp.int32)],
        scratch_shapes=[pltpu.VMEM((ne, LANES), F32)],
        compiler_params=_cparams(("arbitrary",)),
        name="moe_plan",
    )(comb_t, counts)


def _start_row_copies(pos_ref, tm, make_copy):
    def start(t, c):
        for k in range(TOP_K):
            make_copy(t, k, pos_ref[k, t]).start(priority=k % 2)
        return c

    lax.fori_loop(0, tm, start, 0)


def _moe_dispatch_body(pos_ref, plan_ref, hp_ref, xg_hbm, zrow, sem, zsem, *, tm, experts_per_step, n_experts):
    _start_row_copies(pos_ref, tm, lambda t, k, p: pltpu.make_async_copy(
        hp_ref.at[pl.ds(t, 1)], xg_hbm.at[pl.ds(p, 1)], sem))
    zrow[...] = jnp.zeros_like(zrow)
    zero_copy = lambda row: pltpu.make_async_copy(zrow.at[pl.ds(0, 1)], xg_hbm.at[pl.ds(row, 1)], zsem)
    for j in range(experts_per_step):
        e = pl.program_id(0) * experts_per_step + j

        @pl.when(e < n_experts)
        def _():
            start = plan_ref[2, e]
            count = plan_ref[3, e]
            lax.fori_loop(0, count, lambda r, c: (zero_copy(start + r).start(), c)[1], 0)
            lax.fori_loop(0, count, lambda r, c: (zero_copy(start + r).wait(), c)[1], 0)

    for _ in range(TOP_K):
        pltpu.make_async_copy(hp_ref, hp_ref, sem).wait()


def _moe_dispatch(pos, plan, hp, total_rows, n_experts):
    n, half = hp.shape
    tm = _moe_token_tile(n)
    steps = n // tm
    return pl.pallas_call(
        functools.partial(_moe_dispatch_body, tm=tm, experts_per_step=-(-n_experts // steps), n_experts=n_experts),
        grid=(steps,),
        in_specs=[pl.BlockSpec((TOP_K, tm), lambda i: (0, i), memory_space=pltpu.SMEM),
                  pl.BlockSpec(memory_space=pltpu.SMEM),
                  pl.BlockSpec((tm, half), lambda i: (i, 0))],
        out_specs=pl.BlockSpec(memory_space=pl.ANY),
        out_shape=jax.ShapeDtypeStruct((total_rows, half), jnp.uint32),
        scratch_shapes=[pltpu.VMEM((SUBLANES, half), jnp.uint32), pltpu.SemaphoreType.DMA, pltpu.SemaphoreType.DMA],
        compiler_params=_cparams(("arbitrary",)),
        name="moe_dispatch",
    )(pos, plan, hp)


def _moe_ffn_body(te_ref, tv_ref, xg_ref, wg_ref, wu_ref, wd_ref, ys_ref, wg_bf, wu_bf, wd_bf):
    j = pl.program_id(0)
    new_expert = jnp.logical_or(j == 0, te_ref[j] != te_ref[jnp.maximum(j - 1, 0)])

    @pl.when(new_expert)
    def _():
        wg_bf[...] = wg_ref[0, 0].astype(BF16)
        wu_bf[...] = wu_ref[0, 0].astype(BF16)
        wd_bf[...] = wd_ref[0, 0].astype(BF16)

    @pl.when(tv_ref[j] != 0)
    def _():
        lo, hi = _unpack_halves(xg_ref[...])
        half = lo.shape[1]
        hg = _dot(lo, wg_bf[:half, :]) + _dot(hi, wg_bf[half:, :])
        hu = _dot(lo, wu_bf[:half, :]) + _dot(hi, wu_bf[half:, :])
        act = hg * _sigmoid(hg) * hu
        ys_ref[...] = _pack_halves(_dot(act, wd_bf[...]))

    @pl.when(tv_ref[j] == 0)
    def _():
        ys_ref[...] = jnp.zeros_like(ys_ref)


def _moe_ffn(tile_expert, tile_valid, xg, w_gate, w_up, w_down, layer):
    rows, half = xg.shape
    _, ne, d, ff = w_gate.shape
    blk = pl.BlockSpec((MOE_TILE, half), lambda j, te, tv: (j, 0))
    return pl.pallas_call(
        _moe_ffn_body,
        grid_spec=pltpu.PrefetchScalarGridSpec(
            num_scalar_prefetch=2,
            grid=(rows // MOE_TILE,),
            in_specs=[blk,
                      pl.BlockSpec((1, 1, d, ff), lambda j, te, tv: (layer, te[j], 0, 0)),
                      pl.BlockSpec((1, 1, d, ff), lambda j, te, tv: (layer, te[j], 0, 0)),
                      pl.BlockSpec((1, 1, ff, d), lambda j, te, tv: (layer, te[j], 0, 0))],
            out_specs=blk,
            scratch_shapes=[pltpu.VMEM((d, ff), BF16), pltpu.VMEM((d, ff), BF16), pltpu.VMEM((ff, d), BF16)]),
        out_shape=jax.ShapeDtypeStruct((rows, half), jnp.uint32),
        compiler_params=_cparams(("arbitrary",)),
        name="moe_ffn",
    )(tile_expert, tile_valid, xg, w_gate, w_up, w_down)


def _moe_combine_body(pos_ref, w_ref, ys_hbm, o_ref, buf, sem, *, tm):
    _start_row_copies(pos_ref, tm, lambda t, k, p: pltpu.make_async_copy(
        ys_hbm.at[pl.ds(p, 1)], buf.at[k, pl.ds(t, 1)], sem))
    pltpu.make_async_copy(buf, buf, sem).wait()
    w = w_ref[...]
    half = buf.shape[2]
    acc_lo = jnp.zeros((tm, half), F32)
    acc_hi = jnp.zeros((tm, half), F32)
    for k in range(TOP_K):
        lo, hi = _unpack_halves(buf[k])
        wk = w[:, k:k + 1]
        acc_lo = acc_lo + wk * lo
        acc_hi = acc_hi + wk * hi
    o_ref[:, :half] = acc_lo
    o_ref[:, half:] = acc_hi


def _moe_combine(pos, w_tok, ys):
    n = w_tok.shape[0]
    half = ys.shape[1]
    tm = _tile(n, 128, LANES)
    return pl.pallas_call(
        functools.partial(_moe_combine_body, tm=tm),
        grid=(n // tm,),
        in_specs=[pl.BlockSpec((TOP_K, tm), lambda i: (0, i), memory_space=pltpu.SMEM),
                  pl.BlockSpec((tm, LANES), lambda i: (i, 0)),
                  pl.BlockSpec(memory_space=pl.ANY)],
        out_specs=pl.BlockSpec((tm, 2 * half), lambda i: (i, 0)),
        out_shape=jax.ShapeDtypeStruct((n, 2 * half), F32),
        scratch_shapes=[pltpu.VMEM((TOP_K, tm, half), jnp.uint32), pltpu.SemaphoreType.DMA],
        compiler_params=_cparams(("arbitrary",)),
        name="moe_combine",
    )(pos, w_tok, ys)


def _moe_routed(hp, comb_t, counts, w_gate, w_up, w_down, layer):
    n = hp.shape[0]
    ne = w_gate.shape[1]
    run_tiles = (n * TOP_K) // MOE_TILE + ne
    dummy_row = run_tiles * MOE_TILE
    spare_tiles = -(-(TOP_K * _moe_token_tile(n)) // MOE_TILE)
    n_tiles = run_tiles + spare_tiles
    pos, w_tok, te = _moe_plan(comb_t, counts, n_tiles, dummy_row)
    xg = _moe_dispatch(pos, te, hp, n_tiles * MOE_TILE, ne)
    ys = _moe_ffn(te[0, :n_tiles], te[1, :n_tiles], xg, w_gate, w_up, w_down, layer)
    return _moe_combine(pos, w_tok, ys)


def _swiglu_up_body(h_ref, wg_ref, wu_ref, o_ref):
    h = h_ref[...]
    g = jnp.dot(h, wg_ref[...], preferred_element_type=F32)
    u = jnp.dot(h, wu_ref[...], preferred_element_type=F32)
    o_ref[...] = (g * _sigmoid(g) * u).astype(o_ref.dtype)


def _moe_shared_residual(h, w_gate, w_up, w_down, f_routed, f_row0, x2, mod, gate_k, rows_per_batch):
    n, d = x2.shape
    ff = w_up.shape[1]
    tm = _tile(n, 512)
    tn = _tile(ff, 512, LANES)
    act = pl.pallas_call(
        _swiglu_up_body,
        grid=(n // tm, ff // tn),
        in_specs=[pl.BlockSpec((tm, d), lambda i, j: (i, 0)),
                  pl.BlockSpec((d, tn), lambda i, j: (0, j)),
                  pl.BlockSpec((d, tn), lambda i, j: (0, j))],
        out_specs=pl.BlockSpec((tm, tn), lambda i, j: (i, j)),
        out_shape=jax.ShapeDtypeStruct((n, ff), BF16),
        compiler_params=_cparams(("parallel", "arbitrary")),
        name="shared_up",
    )(h, w_gate, w_up)
    tm = _tile(n if rows_per_batch is None else rows_per_batch, 1024)
    tn = _tile(d, 1024, LANES)
    nd = d // tn
    mrow = _mod_row_fn(rows_per_batch, tm)
    blk = pl.BlockSpec((tm, tn), lambda i, j: (i, j))
    assert f_row0 % tm == 0
    f_blk = pl.BlockSpec((tm, tn), lambda i, j: (f_row0 // tm + i, j))
    return _matmul(act, w_down, F32, tm=tm, tn=tn,
                   epilogue=lambda acc, fr, xb, g: xb + g[0] * (acc + fr),
                   extras=[(f_routed, f_blk), (x2, blk),
                           (mod, pl.BlockSpec((1, 1, tn), lambda i, j: (mrow(i), 0, gate_k * nd + j)))],
                   name="shared_down")


def _token_mixer(h, hc, lw, lam_init, tabs, need_ctx, batch, t_len, c_len):
    cos, sin = tabs
    flat_rows = _tile(hc.shape[0], 512)
    flat_tabs = (jnp.ones((flat_rows, LANES), F32), jnp.zeros((flat_rows, LANES), F32))

    def project(hh, seq_len, full, positional):
        n = hh.shape[0]
        rope = (cos, sin, seq_len) if positional else flat_tabs + (None,)
        out = {}
        out["k"] = _proj_qk(hh, lw["w_k"], lw["da_k_norm"], rope[0], rope[1], 1.0, rope[2], "proj_k")
        out["v"] = _matmul(hh, lw["w_v"], BF16, name="proj_v")
        out["rw"] = _matmul(hh, lw["w_rw"], F32, name="proj_rw")
        tm = _tile(seq_len, 512)
        nt = seq_len // tm
        tn = S5_WIDTH
        out["s5"] = _matmul(
            hh, lw["w_s5"], F32, tn=tn, grid_m=n // tm,
            a_spec=pl.BlockSpec((tm, hh.shape[1]), lambda i, j: (i, 0)),
            out_spec=pl.BlockSpec((tm, tn), lambda i, j: (i % nt, i // nt)),
            out_shape=(seq_len, batch * S5_WIDTH), name="proj_s5").reshape(seq_len, batch, S5_WIDTH)
        if full:
            out["q"] = _proj_qk(hh, lw["w_q"], lw["da_q_norm"], rope[0], rope[1], DA_Q_SCALE, rope[2], "proj_q")
            out["gates"] = _matmul(hh, lw["w_gates"], BF16, epilogue=_sigmoid, name="proj_gates")
        return out

    pl_ = project(h, t_len, True, True)
    pc_ = project(hc, c_len, need_ctx, False)

    y_a = _diff_attention(pl_["q"], pl_["k"], pl_["v"], pc_["k"], pc_["v"], lw["da_lambda_q"], lw["da_lambda_k"],
                          lw["da_subln"], lam_init, batch)
    y_ac = None
    if need_ctx:
        y_ac = _diff_attention(pc_["q"], None, None, pc_["k"], pc_["v"], lw["da_lambda_q"], lw["da_lambda_k"],
                               lw["da_subln"], lam_init, batch)

    def prep(p_rw, seq_len):
        names = ("lw_f", "lw_b", "kk", "kka_f", "kka_b", "kd_f", "kd_b", "v", "r", "gd")
        vals = _rwkv_prep(p_rw, seq_len, lw["rw_conv"], lw["rw_lora"], lw["rw_w0a0"], lw["rw_k_k"], lw["rw_k_a"])
        return dict(zip(names, vals))

    tl, tc = prep(pl_["rw"], t_len), prep(pc_["rw"], c_len)
    gw = RW_GROUP_HEADS * RW_HEAD_DIM
    s_zero = jnp.zeros((batch * (RW_WIDTH // gw), gw, gw), F32)
    ocf, ocb, s_ctx_f, s_ctx_b = _rwkv_scan(tc, s_zero, s_zero, batch, want_out=need_ctx)
    olf, olb, _, _ = _rwkv_scan(tl, s_ctx_f, s_ctx_b, batch)
    o_lat, o_ctx = {"f": olf, "b": olb}, {"f": ocf, "b": ocb}
    ro = lambda o, t: _rwkv_readout(o["f"], o["b"], t["r"], t["kd_f"], t["kd_b"], t["v"], t["gd"],
                                    lw["rw_gn_w"], lw["rw_gn_b"], lw["rw_r_k"], lw["rw_g2"])
    y_r = ro(o_lat, tl)
    y_rc = ro(o_ctx, tc) if need_ctx else None

    h_zero = jnp.zeros((2, batch, S5_LANES), F32)
    ys_lat, ys_ctx = {}, {}
    for d, rev in (("f", False), ("b", True)):
        yc, h_ctx = _s5_scan(pc_["s5"], h_zero, lw["s5_" + d], reverse=rev, want_out=need_ctx)
        yl, _ = _s5_scan(pl_["s5"], h_ctx, lw["s5_" + d], reverse=rev)
        ys_lat[d], ys_ctx[d] = yl, yc
    flat = lambda a: a.reshape(a.shape[0] * batch, S5_WIDTH)
    glu = lambda p, ys, seq: _s5_glu(flat(p["s5"]), flat(ys["f"]), flat(ys["b"]), lw["s5_d"], lw["s5_glu_w"],
                                     lw["s5_glu_b"]).reshape(seq, batch * S5_WIDTH)
    y_s = glu(pl_, ys_lat, t_len)
    y_sc = glu(pc_, ys_ctx, c_len) if need_ctx else None

    m = _merge(y_a, y_r, y_s, pl_["gates"], lw["w_branch_a"], lw["w_branch_r"], lw["w_branch_s"], t_len, batch)
    m_c = None
    if need_ctx:
        m_c = _merge(y_ac, y_rc, y_sc, pc_["gates"], lw["w_branch_a"], lw["w_branch_r"], lw["w_branch_s"],
                     c_len, batch)
    return m, m_c


def _out_proj_residual(m, w_out, x2, mod, gate_k, rows_per_batch):
    n, d = x2.shape
    tm = _tile(n if rows_per_batch is None else rows_per_batch, 512)
    tn = _tile(d, 1024, LANES)
    nd = d // tn
    mrow = _mod_row_fn(rows_per_batch, tm)

    def epilogue(acc, xb, g):
        return xb + g[0] * acc

    return _matmul(m, w_out, F32, tm=tm, tn=tn, epilogue=epilogue,
                   extras=[(x2, pl.BlockSpec((tm, tn), lambda i, j: (i, j))),
                           (mod, pl.BlockSpec((1, 1, tn), lambda i, j: (mrow(i), 0, gate_k * nd + j)))],
                   name="out_proj")


def _prepare_layer(i, p):
    w_in = p["w_in"][i]
    c0 = DA_WIDTH
    c1 = 2 * DA_WIDTH
    c2 = c1 + RW_STATE_COLS
    c3 = c2 + S5_WIDTH
    c4 = c3 + DA_WIDTH
    c5 = c4 + RW_OUT_COLS
    bf = lambda a: a.astype(BF16)
    lw = {
        "w_k": bf(w_in[:, :c0]), "w_v": bf(w_in[:, c0:c1]),
        "w_rw": bf(jnp.concatenate([w_in[:, c1:c2], w_in[:, c4:c5]], axis=1)),
        "w_s5": bf(w_in[:, c2:c3]), "w_q": bf(w_in[:, c3:c4]), "w_gates": bf(w_in[:, c5:]),
    }
    for name in ("da_q_norm", "da_k_norm", "da_lambda_q", "da_lambda_k", "da_subln", "rw_conv", "rw_k_k", "rw_k_a",
                 "rw_gn_w", "rw_gn_b", "s5_d", "s5_glu_b", "router_bias"):
        lw[name] = p[name][i].astype(F32)
    w = RW_WIDTH
    lora = jnp.zeros((2 * RW_DECAY_RANK + 2 * RW_A_RANK, 4 * w), F32)
    r0 = 0
    for blk, src in enumerate((p["rw_w2"][i][0], p["rw_w2"][i][1], p["rw_a2"][i][0], p["rw_a2"][i][1])):
        lora = lora.at[r0:r0 + src.shape[0], blk * w:(blk + 1) * w].set(src.astype(F32))
        r0 += src.shape[0]
    lw["rw_lora"] = jnp.stack(_hilo(lora))
    lw["rw_w0a0"] = jnp.concatenate([p["rw_w0"][i][0], p["rw_w0"][i][1], p["rw_a0"][i][0], p["rw_a0"][i][1]]
                                    ).astype(F32).reshape(1, 4 * w)
    lw["rw_r_k"] = p["rw_r_k"][i].astype(F32).reshape(w)
    lw["rw_g2"] = bf(p["rw_g2"][i])
    for d, name in enumerate(("s5_f", "s5_b")):
        lw[name] = _s5_dir_params(p["s5_lambda_re"][i][d], p["s5_lambda_im"][i][d], p["s5_log_dt"][i][d],
                                  p["s5_b_re"][i].astype(F32), p["s5_b_im"][i].astype(F32),
                                  p["s5_c_re"][i][d], p["s5_c_im"][i][d])
    lw["s5_glu_w"] = bf(p["s5_glu_w"][i])
    for name in ("w_branch_a", "w_branch_r", "w_branch_s", "w_out"):
        lw[name] = bf(p[name][i])
    for name in ("exp_w_gate", "exp_w_up", "exp_w_down"):
        lw[name] = p[name]
    lw["layer"] = i
    d_model = w_in.shape[0]
    rw_t = p["router_w"][i].astype(F32).T
    lw["router_w_t"] = jnp.concatenate([rw_t, jnp.zeros((LANES - N_EXPERTS, d_model), F32)], axis=0)
    for name in ("sh_w_gate", "sh_w_up", "sh_w_down"):
        lw[name] = bf(p[name][i])
    return lw


def kernel(x, c, ctx, c_ctx, ada_w, ada_b, w_in, da_q_norm, da_k_norm, da_lambda_q, da_lambda_k, da_subln, rw_conv, rw_w0, rw_w2, rw_a0, rw_a2, rw_g2, rw_k_k, rw_k_a, rw_r_k, rw_gn_w, rw_gn_b, s5_lambda_re, s5_lambda_im, s5_log_dt, s5_b_re, s5_b_im, s5_c_re, s5_c_im, s5_d, s5_glu_w, s5_glu_b, w_branch_a, w_branch_r, w_branch_s, w_out, router_w, router_bias, exp_w_gate, exp_w_up, exp_w_down, sh_w_gate, sh_w_up, sh_w_down):
    params = dict(w_in=w_in, da_q_norm=da_q_norm, da_k_norm=da_k_norm, da_lambda_q=da_lambda_q,
                  da_lambda_k=da_lambda_k, da_subln=da_subln, rw_conv=rw_conv, rw_w0=rw_w0, rw_w2=rw_w2,
                  rw_a0=rw_a0, rw_a2=rw_a2, rw_g2=rw_g2, rw_k_k=rw_k_k, rw_k_a=rw_k_a, rw_r_k=rw_r_k,
                  rw_gn_w=rw_gn_w, rw_gn_b=rw_gn_b, s5_lambda_re=s5_lambda_re, s5_lambda_im=s5_lambda_im,
                  s5_log_dt=s5_log_dt, s5_b_re=s5_b_re, s5_b_im=s5_b_im, s5_c_re=s5_c_re, s5_c_im=s5_c_im,
                  s5_d=s5_d, s5_glu_w=s5_glu_w, s5_glu_b=s5_glu_b, w_branch_a=w_branch_a, w_branch_r=w_branch_r,
                  w_branch_s=w_branch_s, w_out=w_out, router_w=router_w, router_bias=router_bias,
                  exp_w_gate=exp_w_gate, exp_w_up=exp_w_up, exp_w_down=exp_w_down, sh_w_gate=sh_w_gate,
                  sh_w_up=sh_w_up, sh_w_down=sh_w_down)
    batch, t_len, d_model = x.shape
    c_len = ctx.shape[1]
    depth = ada_w.shape[0]
    assert batch <= MOD_CTX_ROW
    tabs = _rope_tables(t_len)
    cvec = jnp.zeros((2 * SUBLANES, d_model), F32).at[:batch].set(c.astype(F32)).at[MOD_CTX_ROW].set(c_ctx.astype(F32))
    x2 = x.astype(F32).reshape(batch * t_len, d_model)
    ctx2 = ctx.astype(F32).reshape(batch * c_len, d_model)
    for i in range(depth):
        lw = _prepare_layer(i, params)
        need_ctx = i < depth - 1
        lam_init = 0.8 - 0.6 * math.exp(-0.3 * i)
        mod = _ada_table(cvec, ada_w, ada_b, i)
        h = _modulate(x2, mod, 0, 1, t_len)
        hc = _modulate(ctx2, mod, 0, 1, None)
        m, m_c = _token_mixer(h, hc, lw, lam_init, tabs, need_ctx, batch, t_len, c_len)
        x2 = _out_proj_residual(m, lw["w_out"], x2, mod, 2, t_len)
        streams = [(x2, t_len)]
        if need_ctx:
            ctx2 = _out_proj_residual(m_c, lw["w_out"], ctx2, mod, 2, None)
            streams.append((ctx2, None))
        routed = [_modulate_route(xs, mod, 3, 4, rpb, lw["router_w_t"], lw["router_bias"]) for xs, rpb in streams]
        f_r = _moe_routed(jnp.concatenate([r[1] for r in routed], axis=0),
                          jnp.concatenate([r[2] for r in routed], axis=1),
                          sum(r[3] for r in routed),
                          lw["exp_w_gate"], lw["exp_w_up"], lw["exp_w_down"], lw["layer"])
        outs, row0 = [], 0
        for (xs, rpb), r in zip(streams, routed):
            outs.append(_moe_shared_residual(r[0], lw["sh_w_gate"], lw["sh_w_up"], lw["sh_w_down"], f_r, row0,
                                             xs, mod, 5, rpb))
            row0 += xs.shape[0]
        x2 = outs[0]
        if need_ctx:
            ctx2 = outs[1]
    return x2.reshape(batch, t_len, d_model).astype(x.dtype)
```
